```python
import jax, jax.numpy as jnp
from jax import lax
import numpy as np

D_MODEL = 1024
BATCH = 32
SEQ = 2048
DEPTH = 2

CHUNK = 64
N_MEM = 256
HEAD_DIM = 64
N_HEADS = D_MODEL // HEAD_DIM
DECAY_LORA = 64
AAA_LORA = 64
GATE_LORA = 128
GN_EPS = 64e-5
N_MIX = 6
SB_BLOCK = 128
X_HEADS = 4
X_HEAD_DIM = D_MODEL // X_HEADS
N_GROUPS = 4
EXPERTS_PER_GROUP = 8
TOP_K = 2
D_EXPERT = 256
NORM_EPS = 1e-6
N_RWKV = (DEPTH + 1) // 2
N_SB = DEPTH // 2

kernel_name = "rwkv7_stickbreak_memx_hmoe_trunk"


def _rms_norm(x, g):
    xf = x.astype(jnp.float32)
    y = xf * lax.rsqrt(jnp.mean(xf * xf, axis=-1, keepdims=True) + NORM_EPS)
    return (y * g.astype(jnp.float32)).astype(x.dtype)


def _rwkv7_time_mix(h, mix, wr, wk, wv, w0, w1, w2, a0, a1, a2, g1, g2,
                    k_k, k_a, r_k, lnx_w, lnx_b, wo):
    B, S, C = h.shape
    H, N = N_HEADS, HEAD_DIM
    f32 = jnp.float32
    dt = h.dtype
    h_prev = jnp.pad(h, ((0, 0), (1, 0), (0, 0)))[:, :-1]
    dx = h_prev - h
    xr, xw, xk, xv, xa, xg = [h + dx * mix[i] for i in range(N_MIX)]

    r = (xr @ wr).astype(f32)
    k = (xk @ wk).astype(f32)
    v = (xv @ wv).astype(f32)
    w_log = -jax.nn.softplus(-(w0 + jnp.tanh(xw @ w1) @ w2).astype(f32)) - 0.5
    decay = jnp.exp(-jnp.exp(w_log))
    a = jax.nn.sigmoid((a0 + (xa @ a1) @ a2).astype(f32))
    g = jax.nn.sigmoid(xg @ g1) @ g2

    kk = (k * k_k).reshape(B, S, H, N)
    kk = kk / jnp.maximum(jnp.linalg.norm(kk, axis=-1, keepdims=True), 1e-12)
    k = k * (1.0 + (a - 1.0) * k_a)

    rh = r.reshape(B, S, H, N)
    kh = k.reshape(B, S, H, N)
    vh = v.reshape(B, S, H, N)
    bh = kk * a.reshape(B, S, H, N)
    dh = decay.reshape(B, S, H, N)

    def to_chunks(t):
        return jnp.moveaxis(t, 1, 0).reshape(S // CHUNK, CHUNK, B, H, N)

    def frame_step(state, inp):
        r_t, d_t, k_t, v_t, kk_t, b_t = inp
        sa = jnp.einsum('bhvk,bhk->bhv', state, -kk_t)
        state = (state * d_t[:, :, None, :]
                 + sa[..., None] * b_t[:, :, None, :]
                 + v_t[..., None] * k_t[:, :, None, :])
        y_t = jnp.einsum('bhvk,bhk->bhv', state, r_t)
        return state, y_t

    def chunk_step(state, inp_chunk):
        return lax.scan(frame_step, state, inp_chunk)

    state0 = jnp.zeros((B, H, N, N), f32)
    _, y = lax.scan(chunk_step, state0,
                    (to_chunks(rh), to_chunks(dh), to_chunks(kh), to_chunks(vh),
                     to_chunks(kk), to_chunks(bh)))
    y = jnp.moveaxis(y.reshape(S, B, H, N), 0, 1)

    mu = jnp.mean(y, axis=-1, keepdims=True)
    var = jnp.mean(jnp.square(y - mu), axis=-1, keepdims=True)
    y = ((y - mu) * lax.rsqrt(var + GN_EPS)).reshape(B, S, C) * lnx_w + lnx_b
    bonus = jnp.sum(rh * kh * r_k, axis=-1, keepdims=True) * vh
    y = y + bonus.reshape(B, S, C)
    return (y * g).astype(dt) @ wo


def _stick_breaking_attention(h, wqkv, wo):
    B, S, C = h.shape
    H, N = N_HEADS, HEAD_DIM
    f32 = jnp.float32
    qkv = (h @ wqkv).reshape(B, S, 3, H, N).astype(f32)
    q = qkv[:, :, 0] * (N ** -0.5)
    k = qkv[:, :, 1]
    v = qkv[:, :, 2]
    outs = []
    for blk in range(S // SB_BLOCK):
        q0 = blk * SB_BLOCK
        end = q0 + SB_BLOCK
        z = jnp.einsum('bthd,bshd->bhts', q[:, q0:end], k[:, :end])
        t_idx = q0 + jnp.arange(SB_BLOCK)[:, None]
        s_idx = jnp.arange(end)[None, :]
        before = s_idx < t_idx
        log_fail = jnp.where(before, jax.nn.log_sigmoid(-z), 0.0)
        later = lax.cumsum(log_fail, axis=3, reverse=True) - log_fail
        att = jnp.where(before, jnp.exp(jax.nn.log_sigmoid(z) + later), 0.0)
        outs.append(jnp.einsum('bhts,bshd->bthd', att, v[:, :end]))
    o = jnp.concatenate(outs, axis=1).reshape(B, S, C)
    return o.astype(h.dtype) @ wo


def _memory_cross_attention(h, mem_k, mem_v, wq, wo):
    B, S, C = h.shape
    q = (h @ wq).reshape(B, S, X_HEADS, X_HEAD_DIM)
    scores = jnp.einsum('bthd,bmhd->bhtm', q, mem_k).astype(jnp.float32) * (X_HEAD_DIM ** -0.5)
    p = jax.nn.softmax(scores, axis=-1).astype(h.dtype)
    o = jnp.einsum('bhtm,bmhd->bthd', p, mem_v).reshape(B, S, C)
    return o @ wo


def _hierarchical_moe(h, wg, bg, we, be, w1, w3, w2):
    B, S, C = h.shape
    G, E = N_GROUPS, EXPERTS_PER_GROUP
    f32 = jnp.float32
    t = h.reshape(B * S, C)
    g_logits = (t @ wg).astype(f32) + bg
    g_prob = jax.nn.softmax(g_logits, axis=-1)
    group = jnp.argmax(g_logits, axis=-1)
    p_group = jnp.take_along_axis(g_prob, group[:, None], axis=-1)
    e_logits = ((t @ we).astype(f32) + be).reshape(-1, G, E)
    e_sel = jnp.take_along_axis(e_logits, group[:, None, None], axis=1)[:, 0]
    top_val, top_idx = lax.top_k(e_sel, TOP_K)
    top_w = jax.nn.softmax(top_val, axis=-1) * p_group
    gate_e = jnp.sum(jax.nn.one_hot(top_idx, E, dtype=f32) * top_w[..., None], axis=1)
    gate = jax.nn.one_hot(group, G, dtype=f32)[:, :, None] * gate_e[:, None, :]
    y = jnp.zeros((B * S, C), f32)
    for gi in range(G):
        hid = jax.nn.silu(jnp.einsum('td,edf->tef', t, w1[gi])) * jnp.einsum('td,edf->tef', t, w3[gi])
        hid = hid * gate[:, gi, :, None].astype(hid.dtype)
        y = y + jnp.einsum('tef,efd->td', hid, w2[gi]).astype(f32)
    return y.astype(h.dtype).reshape(B, S, C)


def setup_inputs(seed: int = 0) -> dict:
    key = jax.random.key(seed)
    ks = iter(jax.random.split(key, 48))
    f32 = jnp.float32
    C, H, N = D_MODEL, N_HEADS, HEAD_DIM
    G, E, F = N_GROUPS, EXPERTS_PER_GROUP, D_EXPERT

    def nrm(shape, scale):
        return jax.random.normal(next(ks), shape, f32) * scale

    def gain(shape):
        return 1.0 + nrm(shape, 0.02)

    return {
        "x": nrm((BATCH, SEQ, C), 1.0),
        "mem": nrm((BATCH, N_MEM, C), 1.0),
        "norm_mix": gain((DEPTH, C)),
        "norm_cross": gain((DEPTH, C)),
        "norm_ffn": gain((DEPTH, C)),
        "norm_mem": gain((C,)),
        "norm_final": gain((C,)),
        "rw_mix": jax.random.uniform(next(ks), (N_RWKV, N_MIX, C), f32),
        "rw_wr": nrm((N_RWKV, C, C), C ** -0.5),
        "rw_wk": nrm((N_RWKV, C, C), C ** -0.5),
        "rw_wv": nrm((N_RWKV, C, C), C ** -0.5),
        "rw_w0": jax.random.uniform(next(ks), (N_RWKV, C), f32, -2.5, 1.0),
        "rw_w1": nrm((N_RWKV, C, DECAY_LORA), C ** -0.5),
        "rw_w2": nrm((N_RWKV, DECAY_LORA, C), 0.5 * DECAY_LORA ** -0.5),
        "rw_a0": nrm((N_RWKV, C), 0.5),
        "rw_a1": nrm((N_RWKV, C, AAA_LORA), C ** -0.5),
        "rw_a2": nrm((N_RWKV, AAA_LORA, C), 0.5 * AAA_LORA ** -0.5),
        "rw_g1": nrm((N_RWKV, C, GATE_LORA), C ** -0.5),
        "rw_g2": nrm((N_RWKV, GATE_LORA, C), GATE_LORA ** -0.5),
        "rw_kk": 1.0 + nrm((N_RWKV, C), 0.1),
        "rw_ka": 1.0 + nrm((N_RWKV, C), 0.1),
        "rw_rk": nrm((N_RWKV, H, N), 0.1),
        "rw_lnx_w": gain((N_RWKV, C)),
        "rw_lnx_b": nrm((N_RWKV, C), 0.02),
        "rw_wo": nrm((N_RWKV, C, C), C ** -0.5),
        "sb_wqkv": nrm((N_SB, C, 3 * C), C ** -0.5),
        "sb_wo": nrm((N_SB, C, C), C ** -0.5),
        "xa_wq": nrm((DEPTH, C, C), C ** -0.5),
        "xa_wkv": nrm((C, 2 * C), C ** -0.5),
        "xa_wo": nrm((DEPTH, C, C), C ** -0.5),
        "moe_wg": nrm((DEPTH, C, G), C ** -0.5),
        "moe_bg": nrm((DEPTH, G), 0.01),
        "moe_we": nrm((DEPTH, C, G * E), C ** -0.5),
        "moe_be": nrm((DEPTH, G * E), 0.01),
        "moe_w1": nrm((DEPTH, G, E, C, F), C ** -0.5),
        "moe_w3": nrm((DEPTH, G, E, C, F), C ** -0.5),
        "moe_w2": nrm((DEPTH, G, E, F, C), F ** -0.5),
    }


def reference(x, mem, norm_mix, norm_cross, norm_ffn, norm_mem, norm_final,
              rw_mix, rw_wr, rw_wk, rw_wv, rw_w0, rw_w1, rw_w2, rw_a0, rw_a1, rw_a2,
              rw_g1, rw_g2, rw_kk, rw_ka, rw_rk, rw_lnx_w, rw_lnx_b, rw_wo,
              sb_wqkv, sb_wo, xa_wq, xa_wkv, xa_wo,
              moe_wg, moe_bg, moe_we, moe_be, moe_w1, moe_w3, moe_w2):
    B = mem.shape[0]
    mem_kv = (_rms_norm(mem, norm_mem) @ xa_wkv).reshape(B, N_MEM, 2, X_HEADS, X_HEAD_DIM)
    mem_k = mem_kv[:, :, 0]
    mem_v = mem_kv[:, :, 1]
    h = x
    for i in range(DEPTH):
        hn = _rms_norm(h, norm_mix[i])
        j = i // 2
        if i % 2 == 0:
            h = h + _rwkv7_time_mix(hn, rw_mix[j], rw_wr[j], rw_wk[j], rw_wv[j],
                                    rw_w0[j], rw_w1[j], rw_w2[j], rw_a0[j], rw_a1[j], rw_a2[j],
                                    rw_g1[j], rw_g2[j], rw_kk[j], rw_ka[j], rw_rk[j],
                                    rw_lnx_w[j], rw_lnx_b[j], rw_wo[j])
        else:
            h = h + _stick_breaking_attention(hn, sb_wqkv[j], sb_wo[j])
        h = h + _memory_cross_attention(_rms_norm(h, norm_cross[i]), mem_k, mem_v,
                                        xa_wq[i], xa_wo[i])
        h = h + _hierarchical_moe(_rms_norm(h, norm_ffn[i]), moe_wg[i], moe_bg[i],
                                  moe_we[i], moe_be[i], moe_w1[i], moe_w3[i], moe_w2[i])
    return _rms_norm(h, norm_final)
```

```python
import functools

import jax
import jax.numpy as jnp
from jax import lax
from jax.experimental import pallas as pl
from jax.experimental.pallas import tpu as pltpu

F32 = jnp.float32
BF16 = jnp.bfloat16

HEAD_DIM = 64
N_MEM = 256
X_HEADS = 4
N_GROUPS = 4
EXPERTS_PER_GROUP = 8
N_EXPERTS = N_GROUPS * EXPERTS_PER_GROUP
D_EXPERT = 256
GN_EPS = 64e-5
NORM_EPS = 1e-6
SB_BLOCK = 128
CHUNK = 64

LANES = 128
MXU_DIM = 256
VMEM_LIMIT = 48 * 1024 * 1024

ROUTER_LANES = LANES
EXPERT_LANE0 = N_GROUPS


def _cparams(*sem):
    return pltpu.CompilerParams(dimension_semantics=sem, vmem_limit_bytes=VMEM_LIMIT)


def _dot(a, b):
    return jnp.dot(a, b, preferred_element_type=F32)


def _dot_nt(a, b):
    return lax.dot_general(a, b, (((1,), (1,)), ((), ())), preferred_element_type=F32)


def _split2(x):
    hi = x.astype(BF16)
    lo = (x - hi.astype(F32)).astype(BF16)
    return hi, lo


def _split3(x):
    hi = x.astype(BF16)
    r1 = x - hi.astype(F32)
    mid = r1.astype(BF16)
    lo = (r1 - mid.astype(F32)).astype(BF16)
    return hi, mid, lo


def _dot_exact_rhs(x, m_bf16, parts):
    pieces = _split3(x) if parts == 3 else _split2(x)
    acc = _dot(pieces[0], m_bf16)
    for p in pieces[1:]:
        acc = acc + _dot(p, m_bf16)
    return acc


def _dot_exact_lhs(m_bf16, x, parts):
    pieces = _split3(x) if parts == 3 else _split2(x)
    acc = _dot(m_bf16, pieces[0])
    for p in pieces[1:]:
        acc = acc + _dot(m_bf16, p)
    return acc


def _dot_f32(a, b):
    a1, a2, a3 = _split3(a)
    b1, b2, b3 = _split3(b)
    acc = _dot(a1, b3) + _dot(a3, b1) + _dot(a2, b2)
    acc = acc + _dot(a1, b2) + _dot(a2, b1)
    return acc + _dot(a1, b1)


def _rms(x, g):
    ms = jnp.mean(x * x, axis=-1, keepdims=True)
    return x * lax.rsqrt(ms + NORM_EPS) * g


def _head_sum(x, ones_bd):
    c = x.shape[-1]
    outs = []
    for j in range(c // MXU_DIM):
        outs.append(_dot_exact_rhs(x[:, j * MXU_DIM:(j + 1) * MXU_DIM], ones_bd, 2))
    return jnp.concatenate(outs, axis=-1) if len(outs) > 1 else outs[0]


def _norm_mm_kernel(x_ref, g_ref, w_ref, o_ref, xn_ref):
    @pl.when(pl.program_id(1) == 0)
    def _():
        xn_ref[...] = _rms(x_ref[...], g_ref[...]).astype(BF16)

    o_ref[...] = _dot(xn_ref[...], w_ref[...]).astype(o_ref.dtype)


def _norm_matmul(x, g, w, out_dtype, tm, tn):
    m, c = x.shape
    n = w.shape[1]
    return pl.pallas_call(
        _norm_mm_kernel,
        grid=(m // tm, n // tn),
        in_specs=[
            pl.BlockSpec((tm, c), lambda i, j: (i, 0)),
            pl.BlockSpec((1, c), lambda i, j: (0, 0)),
            pl.BlockSpec((c, tn), lambda i, j: (0, j)),
        ],
        out_specs=pl.BlockSpec((tm, tn), lambda i, j: (i, j)),
        out_shape=jax.ShapeDtypeStruct((m, n), out_dtype),
        scratch_shapes=[pltpu.VMEM((tm, c), BF16)],
        compiler_params=_cparams("parallel", "arbitrary"),
        name="norm_matmul",
    )(x, g, w)


def _mm_res_kernel(x_ref, w_ref, r_ref, o_ref):
    o_ref[...] = r_ref[...] + _dot(x_ref[...], w_ref[...])


def _matmul_residual(x, w, res, tm):
    m, k = x.shape
    n = w.shape[1]
    return pl.pallas_call(
        _mm_res_kernel,
        grid=(m // tm,),
        in_specs=[
            pl.BlockSpec((tm, k), lambda i: (i, 0)),
            pl.BlockSpec((k, n), lambda i: (0, 0)),
            pl.BlockSpec((tm, n), lambda i: (i, 0)),
        ],
        out_specs=pl.BlockSpec((tm, n), lambda i: (i, 0)),
        out_shape=jax.ShapeDtypeStruct((m, n), F32),
        compiler_params=_cparams("parallel"),
        name="matmul_residual",
    )(x, w, res)


def _softplus(x):
    return jnp.maximum(x, 0.0) + jnp.log1p(jnp.exp(-jnp.abs(x)))


def _rwkv_proj_kernel(h_ref, gn_ref, mix_ref, wr_ref, wk_ref, wv_ref, w1_ref, w2_ref, a1_ref, a2_ref,
                      g1_ref, g2_ref, w0_ref, a0_ref, kk_ref, ka_ref, ones_ref,
                      r_out, lw_out, k_out, v_out, kkn_out, b_out, g_out, carry_ref):
    ts = h_ref.shape[1]
    hn = _rms(h_ref[0], gn_ref[...])

    @pl.when(pl.program_id(1) == 0)
    def _():
        carry_ref[...] = jnp.zeros_like(carry_ref)

    prev_last = carry_ref[0:1, :]
    row = lax.broadcasted_iota(jnp.int32, hn.shape, 0)
    hp = jnp.where(row == 0, prev_last, pltpu.roll(hn, 1, 0))
    carry_ref[0:1, :] = hn[ts - 1:ts, :]
    dx = hp - hn

    def mixed(i):
        return (hn + dx * mix_ref[i:i + 1, :]).astype(BF16)

    r = _dot(mixed(0), wr_ref[...])
    lora_w = _dot(jnp.tanh(_dot(mixed(1), w1_ref[...])).astype(BF16), w2_ref[...])
    k = _dot(mixed(2), wk_ref[...])
    v = _dot(mixed(3), wv_ref[...])
    lora_a = _dot(_dot(mixed(4), a1_ref[...]).astype(BF16), a2_ref[...])
    g = _dot(jax.nn.sigmoid(_dot(mixed(5), g1_ref[...])).astype(BF16), g2_ref[...])

    w_log = -_softplus(-(w0_ref[...] + lora_w)) - 0.5
    lw = -jnp.exp(w_log)
    a = jax.nn.sigmoid(a0_ref[...] + lora_a)

    kk = k * kk_ref[...]
    nrm = jnp.sqrt(_head_sum(kk * kk, ones_ref[...]))
    kkn = kk / jnp.maximum(nrm, 1e-12)
    k_mod = k * (1.0 + (a - 1.0) * ka_ref[...])

    r_out[0] = r
    lw_out[0] = lw
    k_out[0] = k_mod
    v_out[0] = v
    kkn_out[0] = kkn
    b_out[0] = kkn * a
    g_out[0] = g


def _rwkv_proj(h, gn, mix, wr, wk, wv, w1, w2, a1, a2, g1, g2, w0, a0, k_k, k_a, ones_bd, ts):
    b, s, c = h.shape
    full = lambda arr: pl.BlockSpec(arr.shape, lambda i, j: (0,) * arr.ndim)
    tok = pl.BlockSpec((1, ts, c), lambda i, j: (i, j, 0))
    params = (gn, mix, wr, wk, wv, w1, w2, a1, a2, g1, g2, w0, a0, k_k, k_a, ones_bd)
    return pl.pallas_call(
        _rwkv_proj_kernel,
        grid=(b, s // ts),
        in_specs=[tok] + [full(p) for p in params],
        out_specs=[tok] * 7,
        out_shape=[jax.ShapeDtypeStruct((b, s, c), F32)] * 7,
        scratch_shapes=[pltpu.VMEM((8, c), F32)],
        compiler_params=_cparams("parallel", "arbitrary"),
        name="rwkv_proj",
    )(h, *params)


def _rwkv_rec_kernel(r_ref, lw_ref, k_ref, v_ref, kk_ref, b_ref, g_ref, rk_ref, lnw_ref, lnb_ref,
                     tri_ref, ones_ref, o_ref, state_ref):
    ts, lw_lanes = r_ref.shape[1], r_ref.shape[2]
    n_chunks = ts // CHUNK
    n_groups = lw_lanes // LANES
    L = CHUNK

    @pl.when(pl.program_id(2) == 0)
    def _():
        state_ref[...] = jnp.zeros_like(state_ref)

    lane = lax.broadcasted_iota(jnp.int32, (L, LANES), 1)
    head0 = lane < HEAD_DIM
    rr = lax.broadcasted_iota(jnp.int32, (2 * L, 2 * L), 0)
    cc = lax.broadcasted_iota(jnp.int32, (2 * L, 2 * L), 1)
    same_head = (rr // L) == (cc // L)
    strict = same_head & ((cc % L) < (rr % L))
    incl = same_head & ((cc % L) <= (rr % L))
    eye = (rr == cc).astype(F32)
    tri_incl = tri_ref[...]
    ones_bd = ones_ref[...]

    def stack_heads(x):
        return jnp.concatenate([jnp.where(head0, x, 0.0), jnp.where(head0, 0.0, x)], axis=0)

    def unstack(x):
        return x[:L] + x[L:]

    def chunk_body(ci, carry):
        r0 = pl.multiple_of(ci * L, L)
        for gi in range(n_groups):
            ls = slice(gi * LANES, (gi + 1) * LANES)
            r = r_ref[0, pl.ds(r0, L), ls]
            lw = lw_ref[0, pl.ds(r0, L), ls]
            k = k_ref[0, pl.ds(r0, L), ls]
            v = v_ref[0, pl.ds(r0, L), ls]
            kk = kk_ref[0, pl.ds(r0, L), ls]
            bb = b_ref[0, pl.ds(r0, L), ls]
            g = g_ref[0, pl.ds(r0, L), ls]
            s_t = state_ref[gi]

            cum = _dot_exact_lhs(tri_incl, lw, 3)
            d_in = jnp.exp(cum)
            d_prev = jnp.exp(cum - lw)
            d_inv = jnp.exp(-cum)
            d_last = jnp.exp(cum[L - 1:L, :])

            xa = stack_heads(-(d_prev * kk)).astype(BF16)
            xr = stack_heads(d_in * r).astype(BF16)
            bt = (bb * d_inv).astype(BF16)
            kt = (k * d_inv).astype(BF16)
            z_all = jnp.concatenate([bt, bt, kt, kt], axis=0)
            x_all = jnp.concatenate([xa, xr], axis=0)
            gmat = _dot_nt(x_all, z_all)
            m_ab = jnp.where(strict, gmat[:2 * L, :2 * L], 0.0)
            m_ak = jnp.where(strict, gmat[:2 * L, 2 * L:], 0.0)
            m_rb = jnp.where(incl, gmat[2 * L:, :2 * L], 0.0)
            m_rk = jnp.where(incl, gmat[2 * L:, 2 * L:], 0.0)

            p = m_ab
            t_inv = eye + m_ab
            for _ in range(5):
                pb = p.astype(BF16)
                p = _dot(pb, pb)
                t_inv = t_inv + _dot(t_inv.astype(BF16), p.astype(BF16))

            v_st = stack_heads(v).astype(BF16)
            xs = _dot_nt(x_all, s_t.astype(BF16))
            rhs = xs[:2 * L] + _dot(m_ak.astype(BF16), v_st)
            u_st = _dot(t_inv.astype(BF16), rhs.astype(BF16))
            u16 = u_st.astype(BF16)
            y_st = xs[2 * L:] + _dot(m_rb.astype(BF16), u16) + _dot(m_rk.astype(BF16), v_st)
            y = unstack(y_st)
            u = unstack(u_st)

            bd = (bb * d_inv * d_last)
            kd = (k * d_inv * d_last)
            zk = jnp.concatenate([bd, kd], axis=0).astype(BF16)
            uv_t = jnp.concatenate([u, v], axis=0).T.astype(BF16)
            upd = _dot(uv_t, zk)
            state_ref[gi] = s_t * d_last + jnp.where(same_head, upd, 0.0)

            inv_n = 1.0 / HEAD_DIM
            mu = _dot_exact_rhs(y, ones_bd, 2) * inv_n
            yc = y - mu
            var = _dot_exact_rhs(yc * yc, ones_bd, 2) * inv_n
            yn = yc * lax.rsqrt(var + GN_EPS) * lnw_ref[:, ls] + lnb_ref[:, ls]
            bonus = _dot_exact_rhs(r * k * rk_ref[:, ls], ones_bd, 2) * v
            o_ref[0, pl.ds(r0, L), ls] = ((yn + bonus) * g).astype(o_ref.dtype)
        return carry

    lax.fori_loop(0, n_chunks, chunk_body, 0)


def _rwkv_recurrence(r, lw, k, v, kkn, bb, g, r_k, lnx_w, lnx_b, tri, ones_pair, ts, lane_w):
    b, s, c = r.shape
    tok = pl.BlockSpec((1, ts, lane_w), lambda i, j, t: (i, t, j))
    vec = pl.BlockSpec((1, lane_w), lambda i, j, t: (0, j))
    small = lambda arr: pl.BlockSpec(arr.shape, lambda i, j, t: (0,) * arr.ndim)
    return pl.pallas_call(
        _rwkv_rec_kernel,
        grid=(b, c // lane_w, s // ts),
        in_specs=[tok] * 7 + [vec, vec, vec, small(tri), small(ones_pair)],
        out_specs=tok,
        out_shape=jax.ShapeDtypeStruct((b, s, c), BF16),
        scratch_shapes=[pltpu.VMEM((lane_w // LANES, LANES, LANES), F32)],
        compiler_params=_cparams("parallel", "parallel", "arbitrary"),
        name="rwkv_recurrence",
    )(r, lw, k, v, kkn, bb, g, r_k, lnx_w, lnx_b, tri, ones_pair)


def _sb_attn_kernel(q_ref, k_ref, v_ref, cs_ref, o_ref):
    qi = pl.program_id(2)
    tb = SB_BLOCK
    lane = lax.broadcasted_iota(jnp.int32, (tb, LANES), 1)
    head0 = lane < HEAD_DIM
    q = q_ref[0] * (HEAD_DIM ** -0.5)
    qh = (jnp.where(head0, q, jnp.zeros_like(q)), jnp.where(head0, jnp.zeros_like(q), q))
    cs = cs_ref[...]
    t_idx = lax.broadcasted_iota(jnp.int32, (tb, tb), 0)
    s_idx = lax.broadcasted_iota(jnp.int32, (tb, tb), 1)
    before = s_idx < t_idx

    def block(j, carry, diagonal):
        k0 = pl.multiple_of(j * tb, tb)
        kb = k_ref[0, pl.ds(k0, tb), :]
        vb = v_ref[0, pl.ds(k0, tb), :]
        new = []
        for h in range(2):
            acc, run = carry[h]
            z = _dot_nt(qh[h], kb)
            sp = jnp.log1p(jnp.exp(-jnp.abs(z)))
            log_beta = jnp.minimum(z, 0.0) - sp
            log_fail = -jnp.maximum(z, 0.0) - sp
            if diagonal:
                log_fail = jnp.where(before, log_fail, 0.0)
            sums = _dot_exact_rhs(log_fail, cs, 2)
            later = run + sums[:, :tb]
            att = jnp.exp(log_beta + later)
            if diagonal:
                att = jnp.where(before, att, 0.0)
            acc = acc + _dot(att.astype(BF16), vb)
            new.append((acc, run + sums[:, tb:]))
        return tuple(new)

    zero = jnp.zeros((tb, LANES), F32)
    carry = block(qi, ((zero, zero), (zero, zero)), True)

    def body(i, c):
        return block(qi - 1 - i, c, False)

    carry = lax.fori_loop(0, qi, body, carry)
    o_ref[0] = jnp.where(head0, carry[0][0], carry[1][0]).astype(o_ref.dtype)


def _sb_attention(qkv, cs):
    b, s, c3 = qkv.shape
    c = c3 // 3
    n_pairs = c // LANES
    tb = SB_BLOCK
    return pl.pallas_call(
        _sb_attn_kernel,
        grid=(b, n_pairs, s // tb),
        in_specs=[
            pl.BlockSpec((1, tb, LANES), lambda i, p, t: (i, t, p)),
            pl.BlockSpec((1, s, LANES), lambda i, p, t: (i, 0, n_pairs + p)),
            pl.BlockSpec((1, s, LANES), lambda i, p, t: (i, 0, 2 * n_pairs + p)),
            pl.BlockSpec(cs.shape, lambda i, p, t: (0, 0)),
        ],
        out_specs=pl.BlockSpec((1, tb, LANES), lambda i, p, t: (i, t, p)),
        out_shape=jax.ShapeDtypeStruct((b, s, c), BF16),
        compiler_params=_cparams("parallel", "parallel", "arbitrary"),
        name="sb_attention",
    )(qkv, qkv, qkv, cs)


def _cross_kernel(h_ref, gn_ref, wq_ref, kv_ref, wo_ref, o_ref):
    h = h_ref[0]
    c = h.shape[-1]
    xhd = c // X_HEADS
    q = _dot(_rms(h, gn_ref[...]).astype(BF16), wq_ref[...]).astype(BF16)
    outs = []
    for hd in range(X_HEADS):
        qh = q[:, hd * xhd:(hd + 1) * xhd]
        kh = kv_ref[0, :, hd * xhd:(hd + 1) * xhd]
        vh = kv_ref[0, :, c + hd * xhd:c + (hd + 1) * xhd]
        sc = _dot_nt(qh, kh) * (xhd ** -0.5)
        sc = sc - jnp.max(sc, axis=-1, keepdims=True)
        e = jnp.exp(sc)
        p = e / jnp.sum(e, axis=-1, keepdims=True)
        outs.append(_dot(p.astype(BF16), vh).astype(BF16))
    o = jnp.concatenate(outs, axis=-1)
    o_ref[0] = h + _dot(o, wo_ref[...])


def _cross_block(h, gn, wq, memkv, wo, tq):
    b, s, c = h.shape
    tok = pl.BlockSpec((1, tq, c), lambda i, t: (i, t, 0))
    full = lambda arr: pl.BlockSpec(arr.shape, lambda i, t: (0,) * arr.ndim)
    return pl.pallas_call(
        _cross_kernel,
        grid=(b, s // tq),
        in_specs=[tok, full(gn), full(wq),
                  pl.BlockSpec((1,) + memkv.shape[1:], lambda i, t: (i, 0, 0)), full(wo)],
        out_specs=tok,
        out_shape=jax.ShapeDtypeStruct((b, s, c), F32),
        compiler_params=_cparams("parallel", "parallel"),
        name="cross_attention",
    )(h, gn, wq, memkv, wo)


def _lane_min_index(mask, lane):
    return jnp.min(jnp.where(mask, lane, float(ROUTER_LANES)), axis=-1, keepdims=True)


def _moe_gates(logits):
    lane_i = lax.broadcasted_iota(jnp.int32, logits.shape, 1)
    lane = lane_i.astype(F32)
    neg = jnp.float32(-jnp.inf)
    gl = jnp.where(lane_i < N_GROUPS, logits, neg)
    gmax = jnp.max(gl, axis=-1, keepdims=True)
    group = _lane_min_index(gl == gmax, lane)
    p_group = 1.0 / jnp.sum(jnp.exp(gl - gmax), axis=-1, keepdims=True)
    e_idx = lane_i - EXPERT_LANE0
    e_group = jnp.right_shift(e_idx, 3).astype(F32)
    in_group = (e_idx >= 0) & (e_idx < N_EXPERTS) & (e_group == group)
    el = jnp.where(in_group, logits, neg)
    top1 = jnp.max(el, axis=-1, keepdims=True)
    idx1 = _lane_min_index(el == top1, lane)
    el2 = jnp.where(lane == idx1, neg, el)
    top2 = jnp.max(el2, axis=-1, keepdims=True)
    idx2 = _lane_min_index(el2 == top2, lane)
    e2 = jnp.exp(top2 - top1)
    w1 = p_group / (1.0 + e2)
    w2 = p_group * e2 / (1.0 + e2)
    return jnp.where(lane == idx1, w1, jnp.where(lane == idx2, w2, 0.0))


def _moe_dense_kernel(h_ref, gn_ref, wr_ref, br_ref, w1_ref, w3_ref, w2_ref, gf_ref, o_ref,
                      xn_ref, gate_ref, acc_ref, *, final_norm):
    e = pl.program_id(1)

    @pl.when(e == 0)
    def _():
        hn = _rms(h_ref[...], gn_ref[...])
        xn_ref[...] = hn.astype(BF16)
        gate_ref[...] = _moe_gates(_dot_f32(hn, wr_ref[...]) + br_ref[...])
        acc_ref[...] = jnp.zeros_like(acc_ref)

    x = xn_ref[...]
    h1 = _dot(x, w1_ref[0])
    h3 = _dot(x, w3_ref[0])
    lane = lax.broadcasted_iota(jnp.int32, gate_ref.shape, 1)
    ge = jnp.sum(jnp.where(lane == e + EXPERT_LANE0, gate_ref[...], 0.0), axis=-1, keepdims=True)
    hid = (h1 * jax.nn.sigmoid(h1)) * h3 * ge
    acc_ref[...] += _dot(hid.astype(BF16), w2_ref[0])

    @pl.when(e == pl.num_programs(1) - 1)
    def _():
        out = h_ref[...] + acc_ref[...]
        if final_norm:
            out = _rms(out, gf_ref[...])
        o_ref[...] = out


def _moe_dense(h, gn, w_router, b_router, w1, w3, w2, g_final, final_norm, tm):
    t, c = h.shape
    ne, _, f = w1.shape
    tok = pl.BlockSpec((tm, c), lambda i, e: (i, 0))
    full = lambda arr: pl.BlockSpec(arr.shape, lambda i, e: (0,) * arr.ndim)
    return pl.pallas_call(
        functools.partial(_moe_dense_kernel, final_norm=final_norm),
        grid=(t // tm, ne),
        in_specs=[tok, full(gn), full(w_router), full(b_router),
                  pl.BlockSpec((1, c, f), lambda i, e: (e, 0, 0)),
                  pl.BlockSpec((1, c, f), lambda i, e: (e, 0, 0)),
                  pl.BlockSpec((1, f, c), lambda i, e: (e, 0, 0)),
                  full(g_final)],
        out_specs=tok,
        out_shape=jax.ShapeDtypeStruct((t, c), F32),
        scratch_shapes=[pltpu.VMEM((tm, c), BF16), pltpu.VMEM((tm, ROUTER_LANES), F32),
                        pltpu.VMEM((tm, c), F32)],
        compiler_params=_cparams("parallel", "arbitrary"),
        name="moe_dense",
    )(h, gn, w_router, b_router, w1, w3, w2, g_final)


def _block_ones(n, blk):
    i = jnp.arange(n)
    return ((i[:, None] // blk) == (i[None, :] // blk)).astype(BF16)


def _tile(n, pref):
    return pref if n % pref == 0 else n


def kernel(x, mem, norm_mix, norm_cross, norm_ffn, norm_mem, norm_final, rw_mix, rw_wr, rw_wk, rw_wv, rw_w0, rw_w1, rw_w2, rw_a0, rw_a1, rw_a2, rw_g1, rw_g2, rw_kk, rw_ka, rw_rk, rw_lnx_w, rw_lnx_b, rw_wo, sb_wqkv, sb_wo, xa_wq, xa_wkv, xa_wo, moe_wg, moe_bg, moe_we, moe_be, moe_w1, moe_w3, moe_w2):
    b, s, c = x.shape
    t = b * s
    depth = norm_mix.shape[0]
    row = lambda vec: vec.reshape(1, -1).astype(F32)
    bf = lambda w: w.astype(BF16)

    ones_quad = _block_ones(MXU_DIM, HEAD_DIM)
    ones_pair = _block_ones(LANES, HEAD_DIM)
    tri_incl = (jnp.arange(CHUNK)[:, None] >= jnp.arange(CHUNK)[None, :]).astype(BF16)
    jj = jnp.arange(SB_BLOCK)
    sb_cs = jnp.concatenate([(jj[:, None] > jj[None, :]).astype(BF16),
                             jnp.ones((SB_BLOCK, SB_BLOCK), BF16)], axis=1)

    memkv = _norm_matmul(mem.reshape(b * N_MEM, c), row(norm_mem), bf(xa_wkv), BF16,
                         _tile(b * N_MEM, 512), _tile(2 * c, 1024)).reshape(b, N_MEM, 2 * c)

    h = x
    for i in range(depth):
        j = i // 2
        if i % 2 == 0:
            r, lw, k, v, kkn, bb, g = _rwkv_proj(
                h, row(norm_mix[i]), rw_mix[j].astype(F32), bf(rw_wr[j]), bf(rw_wk[j]), bf(rw_wv[j]),
                bf(rw_w1[j]), bf(rw_w2[j]), bf(rw_a1[j]), bf(rw_a2[j]), bf(rw_g1[j]), bf(rw_g2[j]),
                row(rw_w0[j]), row(rw_a0[j]), row(rw_kk[j]), row(rw_ka[j]), ones_quad, _tile(s, 256))
            yg = _rwkv_recurrence(r, lw, k, v, kkn, bb, g, row(rw_rk[j]), row(rw_lnx_w[j]),
                                  row(rw_lnx_b[j]), tri_incl, ones_pair, _tile(s, 512), 2 * LANES)
            h = _matmul_residual(yg.reshape(t, c), bf(rw_wo[j]), h.reshape(t, c),
                                 _tile(t, 1024)).reshape(b, s, c)
        else:
            qkv = _norm_matmul(h.reshape(t, c), row(norm_mix[i]), bf(sb_wqkv[j]), BF16,
                               _tile(t, 1024), _tile(3 * c, 1024)).reshape(b, s, 3 * c)
            o = _sb_attention(qkv, sb_cs)
            h = _matmul_residual(o.reshape(t, c), bf(sb_wo[j]), h.reshape(t, c),
                                 _tile(t, 1024)).reshape(b, s, c)
        h = _cross_block(h, row(norm_cross[i]), bf(xa_wq[i]), memkv, bf(xa_wo[i]), _tile(s, 512))

        w_router = jnp.zeros((c, ROUTER_LANES), F32)
        w_router = w_router.at[:, :N_GROUPS].set(moe_wg[i]).at[:, EXPERT_LANE0:EXPERT_LANE0 + N_EXPERTS].set(moe_we[i])
        b_router = jnp.zeros((1, ROUTER_LANES), F32)
        b_router = b_router.at[0, :N_GROUPS].set(moe_bg[i]).at[0, EXPERT_LANE0:EXPERT_LANE0 + N_EXPERTS].set(moe_be[i])
        f = moe_w1.shape[-1]
        h = _moe_dense(h.reshape(t, c), row(norm_ffn[i]), w_router, b_router,
                       bf(moe_w1[i]).reshape(N_EXPERTS, c, f), bf(moe_w3[i]).reshape(N_EXPERTS, c, f),
                       bf(moe_w2[i]).reshape(N_EXPERTS, f, c), row(norm_final),
                       i == depth - 1, _tile(t, 1024)).reshape(b, s, c)
    return h
```

```python
import functools

import jax
import jax.numpy as jnp
from jax import lax
from jax.experimental import pallas as pl
from jax.experimental.pallas import tpu as pltpu

F32 = jnp.float32
BF16 = jnp.bfloat16

HEAD_DIM = 64
N_MEM = 256
X_HEADS = 4
N_GROUPS = 4
EXPERTS_PER_GROUP = 8
N_EXPERTS = N_GROUPS * EXPERTS_PER_GROUP
D_EXPERT = 256
GN_EPS = 64e-5
NORM_EPS = 1e-6
SB_BLOCK = 128
CHUNK = 64

LANES = 128
MXU_DIM = 256
VMEM_LIMIT = 48 * 1024 * 1024

ROUTER_LANES = LANES
EXPERT_LANE0 = N_GROUPS


def _cparams(*sem):
    return pltpu.CompilerParams(dimension_semantics=sem, vmem_limit_bytes=VMEM_LIMIT)


def _dot(a, b):
    return jnp.dot(a, b, preferred_element_type=F32)


def _dot_nt(a, b):
    return lax.dot_general(a, b, (((1,), (1,)), ((), ())), preferred_element_type=F32)


def _split2(x):
    hi = x.astype(BF16)
    lo = (x - hi.astype(F32)).astype(BF16)
    return hi, lo


def _split3(x):
    hi = x.astype(BF16)
    r1 = x - hi.astype(F32)
    mid = r1.astype(BF16)
    lo = (r1 - mid.astype(F32)).astype(BF16)
    return hi, mid, lo


def _dot_exact_rhs(x, m_bf16, parts):
    pieces = _split3(x) if parts == 3 else _split2(x)
    if parts * x.shape[1] <= MXU_DIM:
        return _dot(jnp.concatenate(pieces, axis=1), jnp.concatenate([m_bf16] * parts, axis=0))
    acc = _dot(pieces[0], m_bf16)
    for p in pieces[1:]:
        acc = acc + _dot(p, m_bf16)
    return acc


def _dot_exact_lhs3(m3_bf16, x):
    return _dot(m3_bf16, jnp.concatenate(_split3(x), axis=0))


def _dot_f32(a, b):
    a1, a2, a3 = _split3(a)
    b1, b2, b3 = _split3(b)
    acc = _dot(a1, b3) + _dot(a3, b1) + _dot(a2, b2)
    acc = acc + _dot(a1, b2) + _dot(a2, b1)
    return acc + _dot(a1, b1)


def _rms(x, g):
    ms = jnp.mean(x * x, axis=-1, keepdims=True)
    return x * lax.rsqrt(ms + NORM_EPS) * g


def _head_sum(x, ones_bd):
    c = x.shape[-1]
    outs = []
    for j in range(c // MXU_DIM):
        outs.append(_dot_exact_rhs(x[:, j * MXU_DIM:(j + 1) * MXU_DIM], ones_bd, 2))
    return jnp.concatenate(outs, axis=-1) if len(outs) > 1 else outs[0]


def _norm_mm_kernel(x_ref, g_ref, w_ref, o_ref, xn_ref):
    @pl.when(pl.program_id(1) == 0)
    def _():
        xn_ref[...] = _rms(x_ref[...], g_ref[...]).astype(BF16)

    o_ref[...] = _dot(xn_ref[...], w_ref[...]).astype(o_ref.dtype)


def _norm_matmul(x, g, w, out_dtype, tm, tn):
    m, c = x.shape
    n = w.shape[1]
    return pl.pallas_call(
        _norm_mm_kernel,
        grid=(m // tm, n // tn),
        in_specs=[
            pl.BlockSpec((tm, c), lambda i, j: (i, 0)),
            pl.BlockSpec((1, c), lambda i, j: (0, 0)),
            pl.BlockSpec((c, tn), lambda i, j: (0, j)),
        ],
        out_specs=pl.BlockSpec((tm, tn), lambda i, j: (i, j)),
        out_shape=jax.ShapeDtypeStruct((m, n), out_dtype),
        scratch_shapes=[pltpu.VMEM((tm, c), BF16)],
        compiler_params=_cparams("parallel", "arbitrary"),
        name="norm_matmul",
    )(x, g, w)


def _mm_res_kernel(x_ref, w_ref, r_ref, o_ref):
    o_ref[...] = r_ref[...] + _dot(x_ref[...], w_ref[...])


def _matmul_residual(x, w, res, tm):
    m, k = x.shape
    n = w.shape[1]
    return pl.pallas_call(
        _mm_res_kernel,
        grid=(m // tm,),
        in_specs=[
            pl.BlockSpec((tm, k), lambda i: (i, 0)),
            pl.BlockSpec((k, n), lambda i: (0, 0)),
            pl.BlockSpec((tm, n), lambda i: (i, 0)),
        ],
        out_specs=pl.BlockSpec((tm, n), lambda i: (i, 0)),
        out_shape=jax.ShapeDtypeStruct((m, n), F32),
        compiler_params=_cparams("parallel"),
        name="matmul_residual",
    )(x, w, res)


def _softplus(x):
    return jnp.maximum(x, 0.0) + jnp.log1p(jnp.exp(-jnp.abs(x)))


def _rwkv_proj_kernel(h_ref, gn_ref, mix_ref, wr_ref, wk_ref, wv_ref, w1_ref, w2_ref, a1_ref, a2_ref,
                      g1_ref, g2_ref, w0_ref, a0_ref, kk_ref, ka_ref, ones_ref,
                      r_out, lw_out, k_out, v_out, kkn_out, b_out, g_out, carry_ref):
    ts = h_ref.shape[1]
    hn = _rms(h_ref[0], gn_ref[...])

    @pl.when(pl.program_id(1) == 0)
    def _():
        carry_ref[...] = jnp.zeros_like(carry_ref)

    prev_last = carry_ref[0:1, :]
    row = lax.broadcasted_iota(jnp.int32, hn.shape, 0)
    hp = jnp.where(row == 0, prev_last, pltpu.roll(hn, 1, 0))
    carry_ref[0:1, :] = hn[ts - 1:ts, :]
    dx = hp - hn

    def mixed(i):
        return (hn + dx * mix_ref[i:i + 1, :]).astype(BF16)

    r = _dot(mixed(0), wr_ref[...])
    lora_w = _dot(jnp.tanh(_dot(mixed(1), w1_ref[...])).astype(BF16), w2_ref[...])
    k = _dot(mixed(2), wk_ref[...])
    v = _dot(mixed(3), wv_ref[...])
    lora_a = _dot(_dot(mixed(4), a1_ref[...]).astype(BF16), a2_ref[...])
    g = _dot(jax.nn.sigmoid(_dot(mixed(5), g1_ref[...])).astype(BF16), g2_ref[...])

    w_log = -_softplus(-(w0_ref[...] + lora_w)) - 0.5
    lw = -jnp.exp(w_log)
    a = jax.nn.sigmoid(a0_ref[...] + lora_a)

    kk = k * kk_ref[...]
    nrm = jnp.sqrt(_head_sum(kk * kk, ones_ref[...]))
    kkn = kk / jnp.maximum(nrm, 1e-12)
    k_mod = k * (1.0 + (a - 1.0) * ka_ref[...])

    r_out[0] = r
    lw_out[0] = lw
    k_out[0] = k_mod
    v_out[0] = v
    kkn_out[0] = kkn
    b_out[0] = kkn * a
    g_out[0] = g


def _rwkv_proj(h, gn, mix, wr, wk, wv, w1, w2, a1, a2, g1, g2, w0, a0, k_k, k_a, ones_bd, ts):
    b, s, c = h.shape
    full = lambda arr: pl.BlockSpec(arr.shape, lambda i, j: (0,) * arr.ndim)
    tok = pl.BlockSpec((1, ts, c), lambda i, j: (i, j, 0))
    params = (gn, mix, wr, wk, wv, w1, w2, a1, a2, g1, g2, w0, a0, k_k, k_a, ones_bd)
    return pl.pallas_call(
        _rwkv_proj_kernel,
        grid=(b, s // ts),
        in_specs=[tok] + [full(p) for p in params],
        out_specs=[tok] * 7,
        out_shape=[jax.ShapeDtypeStruct((b, s, c), F32)] * 7,
        scratch_shapes=[pltpu.VMEM((8, c), F32)],
        compiler_params=_cparams("parallel", "arbitrary"),
        name="rwkv_proj",
    )(h, *params)


def _rwkv_rec_kernel(r_ref, lw_ref, k_ref, v_ref, kk_ref, b_ref, g_ref, rk_ref, lnw_ref, lnb_ref,
                     tri_ref, ones_ref, o_ref,
                     state_ref, xp_ref, vp_ref, y0_ref, bdt_ref, svt_ref, dl_ref, bon_ref):
    ts, lw_lanes = r_ref.shape[1], r_ref.shape[2]
    n_chunks = ts // CHUNK
    n_groups = lw_lanes // LANES
    L = CHUNK
    prep_chunks = 2 if n_chunks % 2 == 0 else 1

    @pl.when(pl.program_id(2) == 0)
    def _():
        state_ref[...] = jnp.zeros_like(state_ref)

    lane = lax.broadcasted_iota(jnp.int32, (L, LANES), 1)
    head0 = lane < HEAD_DIM
    rr = lax.broadcasted_iota(jnp.int32, (2 * L, 2 * L), 0)
    cc = lax.broadcasted_iota(jnp.int32, (2 * L, 2 * L), 1)
    same_head = (rr // L) == (cc // L)
    strict = same_head & ((cc % L) < (rr % L))
    incl = same_head & ((cc % L) <= (rr % L))
    eye = (rr == cc).astype(F32)
    tri_incl = tri_ref[...]
    ones_bd = ones_ref[...]

    def stack_heads(x):
        return jnp.concatenate([jnp.where(head0, x, 0.0), jnp.where(head0, 0.0, x)], axis=0)

    def unstack(x):
        return x[:L] + x[L:]

    def prepare(ci, carry):
        chains = [(ci * prep_chunks + cc, gi) for cc in range(prep_chunks) for gi in range(n_groups)]
        rows = [pl.ds(pl.multiple_of(c * L, L), L) for c, _ in chains]
        lanes = [slice(gi * LANES, (gi + 1) * LANES) for _, gi in chains]
        idxs = [c * n_groups + gi for c, gi in chains]
        each = lambda f, *cols: [f(*a) for a in zip(*cols)]
        load = lambda ref: each(lambda rw, ls: ref[0, rw, ls], rows, lanes)
        r, lw, k, v, kk, bb = (load(ref) for ref in (r_ref, lw_ref, k_ref, v_ref, kk_ref, b_ref))

        cum = each(lambda x: _dot_exact_lhs3(tri_incl, x), lw)
        d_inv = each(lambda c_: jnp.exp(-c_), cum)
        d_last = each(lambda c_: jnp.exp(c_[L - 1:L, :]), cum)
        xa = each(lambda c_, l_, kk_: stack_heads(-(jnp.exp(c_ - l_) * kk_)).astype(BF16), cum, lw, kk)
        xr32 = each(lambda c_, r_: stack_heads(jnp.exp(c_) * r_), cum, r)
        bt32 = each(lambda b_, d_: b_ * d_, bb, d_inv)
        kt32 = each(lambda k_, d_: k_ * d_, k, d_inv)

        def gram(xa_, xr_, bt_, kt_):
            bt16, kt16 = bt_.astype(BF16), kt_.astype(BF16)
            z_all = jnp.concatenate([bt16, bt16, kt16, kt16], axis=0)
            x_all = jnp.concatenate([xa_, xr_.astype(BF16)], axis=0)
            return _dot_nt(x_all, z_all)

        gmat = each(gram, xa, xr32, bt32, kt32)
        m_ab = each(lambda g_: jnp.where(strict, g_[:2 * L, :2 * L], 0.0), gmat)
        m_ak = each(lambda g_: jnp.where(strict, g_[:2 * L, 2 * L:], 0.0).astype(BF16), gmat)
        m_rb = each(lambda g_: jnp.where(incl, g_[2 * L:, :2 * L], 0.0).astype(BF16), gmat)
        m_rk = each(lambda g_: jnp.where(incl, g_[2 * L:, 2 * L:], 0.0).astype(BF16), gmat)

        p = m_ab
        t_inv = each(lambda m_: eye + m_, m_ab)
        for _ in range(5):
            p = each(lambda p_: _dot(p_.astype(BF16), p_.astype(BF16)), p)
            t_inv = each(lambda t_, p_: t_ + _dot(t_.astype(BF16), p_.astype(BF16)), t_inv, p)
        t16 = each(lambda t_: t_.astype(BF16), t_inv)

        v_st = each(lambda v_: stack_heads(v_).astype(BF16), v)
        mv = each(_dot, m_ak, v_st)
        tav = each(lambda t_, xa_, mv_: _dot(t_, jnp.concatenate([xa_, mv_.astype(BF16)], axis=1)),
                   t16, xa, mv)
        rby = each(lambda m_, tav_: _dot(m_, tav_.astype(BF16)), m_rb, tav)
        y0 = each(lambda rby_, m_, v_: rby_[:, LANES:] + _dot(m_, v_), rby, m_rk, v_st)
        svt = each(lambda kt_, dl_, v_: _dot(stack_heads(kt_ * dl_).T.astype(BF16), v_), kt32, d_last, v_st)
        for n, idx in enumerate(idxs):
            xp_ref[idx] = jnp.concatenate([tav[n][:, :LANES], xr32[n] + rby[n][:, :LANES]], axis=0).astype(BF16)
            vp_ref[idx] = tav[n][:, LANES:]
            y0_ref[idx] = y0[n]
            bdt_ref[idx] = stack_heads(bt32[n] * d_last[n]).T.astype(BF16)
            svt_ref[idx] = svt[n]
            dl_ref[idx] = jnp.broadcast_to(d_last[n], (LANES, LANES)).T
            bon_ref[rows[n], lanes[n]] = _dot_exact_rhs(r[n] * k[n] * rk_ref[:, lanes[n]], ones_bd, 2) * v[n]
        return carry

    def advance(ci, carry):
        rows = pl.ds(pl.multiple_of(ci * L, L), L)
        groups = list(range(n_groups))
        lanes = [slice(gi * LANES, (gi + 1) * LANES) for gi in groups]
        idxs = [ci * n_groups + gi for gi in groups]
        each = lambda f, *cols: [f(*a) for a in zip(*cols)]
        s_kv = each(lambda gi: state_ref[gi], groups)
        xs = each(lambda i_, s_: _dot(xp_ref[i_], s_.astype(BF16)), idxs, s_kv)
        u16 = each(lambda x_, i_: (x_[:2 * L] + vp_ref[i_]).astype(BF16), xs, idxs)
        upd = each(lambda i_, u_: _dot(bdt_ref[i_], u_), idxs, u16)
        for gi in groups:
            state_ref[gi] = s_kv[gi] * dl_ref[idxs[gi]] + svt_ref[idxs[gi]] + upd[gi]

        inv_n = 1.0 / HEAD_DIM
        y = each(lambda x_, i_: unstack(x_[2 * L:] + y0_ref[i_]), xs, idxs)
        yc = each(lambda y_: y_ - _dot_exact_rhs(y_, ones_bd, 2) * inv_n, y)
        var = each(lambda yc_: _dot_exact_rhs(yc_ * yc_, ones_bd, 2) * inv_n, yc)
        for gi in groups:
            ls = lanes[gi]
            yn = yc[gi] * lax.rsqrt(var[gi] + GN_EPS) * lnw_ref[:, ls] + lnb_ref[:, ls]
            o_ref[0, rows, ls] = ((yn + bon_ref[rows, ls]) * g_ref[0, rows, ls]).astype(o_ref.dtype)
        return carry

    lax.fori_loop(0, n_chunks // prep_chunks, prepare, 0)
    lax.fori_loop(0, n_chunks, advance, 0)


def _rwkv_recurrence(r, lw, k, v, kkn, bb, g, r_k, lnx_w, lnx_b, tri, ones_pair, ts, lane_w):
    b, s, c = r.shape
    tok = pl.BlockSpec((1, ts, lane_w), lambda i, j, t: (i, t, j))
    vec = pl.BlockSpec((1, lane_w), lambda i, j, t: (0, j))
    small = lambda arr: pl.BlockSpec(arr.shape, lambda i, j, t: (0,) * arr.ndim)
    n_groups = lane_w // LANES
    ncg = (ts // CHUNK) * n_groups
    return pl.pallas_call(
        _rwkv_rec_kernel,
        grid=(b, c // lane_w, s // ts),
        in_specs=[tok] * 7 + [vec, vec, vec, small(tri), small(ones_pair)],
        out_specs=tok,
        out_shape=jax.ShapeDtypeStruct((b, s, c), BF16),
        scratch_shapes=[pltpu.VMEM((n_groups, LANES, LANES), F32),
                        pltpu.VMEM((ncg, 4 * CHUNK, LANES), BF16),
                        pltpu.VMEM((ncg, 2 * CHUNK, LANES), F32),
                        pltpu.VMEM((ncg, 2 * CHUNK, LANES), F32),
                        pltpu.VMEM((ncg, LANES, 2 * CHUNK), BF16),
                        pltpu.VMEM((ncg, LANES, LANES), F32),
                        pltpu.VMEM((ncg, LANES, LANES), F32),
                        pltpu.VMEM((ts, lane_w), F32)],
        compiler_params=_cparams("parallel", "parallel", "arbitrary"),
        name="rwkv_recurrence",
    )(r, lw, k, v, kkn, bb, g, r_k, lnx_w, lnx_b, tri, ones_pair)


SB_QTILE = 512
SB_SUBTILE = 128
SB_STAGE_LAG = 2


def _sb_attn_kernel(q_ref, k_ref, v_ref, cs_ref, o_ref, qs_ref, acc_ref, run_ref):
    s = q_ref.shape[1]
    tb = SB_BLOCK
    tq = min(SB_QTILE, s)
    rs = SB_SUBTILE
    kpb = 2 if (tq // tb) % 2 == 0 else 1
    nb, nt = s // tb, s // tq
    shift = (tq // tb).bit_length() - 1
    lane = lax.broadcasted_iota(jnp.int32, (s, LANES), 1)
    head0 = lane < HEAD_DIM
    q = q_ref[0] * (HEAD_DIM ** -0.5)
    qs_ref[0] = jnp.where(head0, q, jnp.zeros_like(q))
    qs_ref[1] = jnp.where(head0, jnp.zeros_like(q), q)
    acc_ref[...] = jnp.zeros_like(acc_ref)
    run_ref[...] = jnp.zeros_like(run_ref)
    cs = cs_ref[...]
    cs2 = jnp.concatenate([cs, cs], axis=0)

    def tile_step(ti, blocks, masked):
        r0 = ti * tq
        subs = [(kbi, h, pl.ds(pl.multiple_of(r0 + i * rs, rs), rs), r0 + i * rs)
                for kbi in range(len(blocks)) for i in range(tq // rs) for h in range(2)]
        n = len(subs)
        lag = SB_STAGE_LAG
        z, lb, parts, sums, att, before = ({} for _ in range(6))
        for step in range(n + 2 + 2 * lag):
            if step < n:
                kbi, h, rows, _ = subs[step]
                z[step] = _dot_nt(qs_ref[h, rows, :], blocks[kbi][1])
            i = step - 1
            if 0 <= i < n:
                zz = z.pop(i)
                sp = jnp.log(1.0 + jnp.exp(-jnp.abs(zz)))
                lb[i] = jnp.minimum(zz, 0.0) - sp
                log_fail = lb[i] - zz
                if masked:
                    t_idx = subs[i][3] + lax.broadcasted_iota(jnp.int32, (rs, tb), 0)
                    s_idx = blocks[subs[i][0]][0] + lax.broadcasted_iota(jnp.int32, (rs, tb), 1)
                    before[i] = s_idx < t_idx
                    log_fail = jnp.where(before[i], log_fail, 0.0)
                parts[i] = _split2(log_fail)
            i = step - 1 - lag
            if 0 <= i < n:
                hi, lo = parts.pop(i)
                sums[i] = _dot(jnp.concatenate([hi, lo], axis=1), cs2)
            i = step - 2 - lag
            if 0 <= i < n:
                _, h, rows, _ = subs[i]
                sm = sums.pop(i)
                a = jnp.exp(lb.pop(i) + run_ref[h, rows, :] + sm[:, :tb])
                if masked:
                    a = jnp.where(before.pop(i), a, 0.0)
                att[i] = a.astype(BF16)
                run_ref[h, rows, :] += sm[:, tb:]
            i = step - 2 - 2 * lag
            if 0 <= i < n:
                kbi, h, rows, _ = subs[i]
                acc_ref[h, rows, :] += _dot(att.pop(i), blocks[kbi][2])

    def key_blocks(jj, carry):
        j = nb - 1 - kpb * jj
        blocks = []
        for d in range(kpb):
            k0 = pl.multiple_of((j - d) * tb, tb)
            blocks.append((k0, k_ref[0, pl.ds(k0, tb), :], v_ref[0, pl.ds(k0, tb), :]))
        t0 = lax.shift_right_logical(j, shift)
        tile_step(t0, blocks, True)

        def below(ti, c):
            tile_step(ti, blocks, False)
            return c

        lax.fori_loop(t0 + 1, nt, below, 0)
        return carry

    lax.fori_loop(0, nb // kpb, key_blocks, 0)
    o_ref[0] = jnp.where(head0, acc_ref[0], acc_ref[1]).astype(o_ref.dtype)


def _sb_attention(qkv, cs):
    b, s, c3 = qkv.shape
    c = c3 // 3
    n_pairs = c // LANES
    return pl.pallas_call(
        _sb_attn_kernel,
        grid=(b, n_pairs),
        in_specs=[
            pl.BlockSpec((1, s, LANES), lambda i, p: (i, 0, p)),
            pl.BlockSpec((1, s, LANES), lambda i, p: (i, 0, n_pairs + p)),
            pl.BlockSpec((1, s, LANES), lambda i, p: (i, 0, 2 * n_pairs + p)),
            pl.BlockSpec(cs.shape, lambda i, p: (0, 0)),
        ],
        out_specs=pl.BlockSpec((1, s, LANES), lambda i, p: (i, 0, p)),
        out_shape=jax.ShapeDtypeStruct((b, s, c), BF16),
        scratch_shapes=[pltpu.VMEM((2, s, LANES), BF16), pltpu.VMEM((2, s, LANES), F32),
                        pltpu.VMEM((2, s, SB_BLOCK), F32)],
        compiler_params=_cparams("parallel", "parallel"),
        name="sb_attention",
    )(qkv, qkv, qkv, cs)


def _cross_kernel(h_ref, gn_ref, wq_ref, kv_ref, wo_ref, o_ref):
    h = h_ref[0]
    c = h.shape[-1]
    xhd = c // X_HEADS
    q = _dot(_rms(h, gn_ref[...]).astype(BF16), wq_ref[...]).astype(BF16)
    outs = []
    for hd in range(X_HEADS):
        qh = q[:, hd * xhd:(hd + 1) * xhd]
        kh = kv_ref[0, :, hd * xhd:(hd + 1) * xhd]
        vh = kv_ref[0, :, c + hd * xhd:c + (hd + 1) * xhd]
        sc = _dot_nt(qh, kh) * (xhd ** -0.5)
        sc = sc - jnp.max(sc, axis=-1, keepdims=True)
        e = jnp.exp(sc)
        p = e / jnp.sum(e, axis=-1, keepdims=True)
        outs.append(_dot(p.astype(BF16), vh).astype(BF16))
    o = jnp.concatenate(outs, axis=-1)
    o_ref[0] = h + _dot(o, wo_ref[...])


def _cross_block(h, gn, wq, memkv, wo, tq):
    b, s, c = h.shape
    tok = pl.BlockSpec((1, tq, c), lambda i, t: (i, t, 0))
    full = lambda arr: pl.BlockSpec(arr.shape, lambda i, t: (0,) * arr.ndim)
    return pl.pallas_call(
        _cross_kernel,
        grid=(b, s // tq),
        in_specs=[tok, full(gn), full(wq),
                  pl.BlockSpec((1,) + memkv.shape[1:], lambda i, t: (i, 0, 0)), full(wo)],
        out_specs=tok,
        out_shape=jax.ShapeDtypeStruct((b, s, c), F32),
        compiler_params=_cparams("parallel", "parallel"),
        name="cross_attention",
    )(h, gn, wq, memkv, wo)


def _lane_min_index(mask, lane):
    return jnp.min(jnp.where(mask, lane, float(ROUTER_LANES)), axis=-1, keepdims=True)


def _moe_gates(logits):
    lane_i = lax.broadcasted_iota(jnp.int32, logits.shape, 1)
    lane = lane_i.astype(F32)
    neg = jnp.float32(-jnp.inf)
    gl = jnp.where(lane_i < N_GROUPS, logits, neg)
    gmax = jnp.max(gl, axis=-1, keepdims=True)
    group = _lane_min_index(gl == gmax, lane)
    p_group = 1.0 / jnp.sum(jnp.exp(gl - gmax), axis=-1, keepdims=True)
    e_idx = lane_i - EXPERT_LANE0
    e_group = jnp.right_shift(e_idx, 3).astype(F32)
    in_group = (e_idx >= 0) & (e_idx < N_EXPERTS) & (e_group == group)
    el = jnp.where(in_group, logits, neg)
    top1 = jnp.max(el, axis=-1, keepdims=True)
    idx1 = _lane_min_index(el == top1, lane)
    el2 = jnp.where(lane == idx1, neg, el)
    top2 = jnp.max(el2, axis=-1, keepdims=True)
    idx2 = _lane_min_index(el2 == top2, lane)
    e2 = jnp.exp(top2 - top1)
    w1 = p_group / (1.0 + e2)
    w2 = p_group * e2 / (1.0 + e2)
    return jnp.where(lane == idx1, w1, jnp.where(lane == idx2, w2, 0.0))


def _moe_dense_kernel(h_ref, gn_ref, wr_ref, br_ref, w1_ref, w3_ref, w2_ref, gf_ref, o_ref,
                      xn_ref, gate_ref, acc_ref, *, final_norm):
    e = pl.program_id(1)

    @pl.when(e == 0)
    def _():
        hn = _rms(h_ref[...], gn_ref[...])
        xn_ref[...] = hn.astype(BF16)
        gate_ref[...] = _moe_gates(_dot_f32(hn, wr_ref[...]) + br_ref[...])
        acc_ref[...] = jnp.zeros_like(acc_ref)

    x = xn_ref[...]
    h1 = _dot(x, w1_ref[0])
    h3 = _dot(x, w3_ref[0])
    lane = lax.broadcasted_iota(jnp.int32, gate_ref.shape, 1)
    ge = jnp.sum(jnp.where(lane == e + EXPERT_LANE0, gate_ref[...], 0.0), axis=-1, keepdims=True)
    hid = (h1 * jax.nn.sigmoid(h1)) * h3 * ge
    acc_ref[...] += _dot(hid.astype(BF16), w2_ref[0])

    @pl.when(e == pl.num_programs(1) - 1)
    def _():
        out = h_ref[...] + acc_ref[...]
        if final_norm:
            out = _rms(out, gf_ref[...])
        o_ref[...] = out


def _moe_dense(h, gn, w_router, b_router, w1, w3, w2, g_final, final_norm, tm):
    t, c = h.shape
    ne, _, f = w1.shape
    tok = pl.BlockSpec((tm, c), lambda i, e: (i, 0))
    full = lambda arr: pl.BlockSpec(arr.shape, lambda i, e: (0,) * arr.ndim)
    return pl.pallas_call(
        functools.partial(_moe_dense_kernel, final_norm=final_norm),
        grid=(t // tm, ne),
        in_specs=[tok, full(gn), full(w_router), full(b_router),
                  pl.BlockSpec((1, c, f), lambda i, e: (e, 0, 0)),
                  pl.BlockSpec((1, c, f), lambda i, e: (e, 0, 0)),
                  pl.BlockSpec((1, f, c), lambda i, e: (e, 0, 0)),
                  full(g_final)],
        out_specs=tok,
        out_shape=jax.ShapeDtypeStruct((t, c), F32),
        scratch_shapes=[pltpu.VMEM((tm, c), BF16), pltpu.VMEM((tm, ROUTER_LANES), F32),
                        pltpu.VMEM((tm, c), F32)],
        compiler_params=_cparams("parallel", "arbitrary"),
        name="moe_dense",
    )(h, gn, w_router, b_router, w1, w3, w2, g_final)


def _block_ones(n, blk):
    i = jnp.arange(n)
    return ((i[:, None] // blk) == (i[None, :] // blk)).astype(BF16)


def _tile(n, pref):
    return pref if n % pref == 0 else n


def kernel(x, mem, norm_mix, norm_cross, norm_ffn, norm_mem, norm_final, rw_mix, rw_wr, rw_wk, rw_wv, rw_w0, rw_w1, rw_w2, rw_a0, rw_a1, rw_a2, rw_g1, rw_g2, rw_kk, rw_ka, rw_rk, rw_lnx_w, rw_lnx_b, rw_wo, sb_wqkv, sb_wo, xa_wq, xa_wkv, xa_wo, moe_wg, moe_bg, moe_we, moe_be, moe_w1, moe_w3, moe_w2):
    b, s, c = x.shape
    t = b * s
    depth = norm_mix.shape[0]
    row = lambda vec: vec.reshape(1, -1).astype(F32)
    bf = lambda w: w.astype(BF16)

    ones_quad = _block_ones(MXU_DIM, HEAD_DIM)
    ones_pair = _block_ones(LANES, HEAD_DIM)
    tri_incl = jnp.tile((jnp.arange(CHUNK)[:, None] >= jnp.arange(CHUNK)[None, :]).astype(BF16), (1, 3))
    jj = jnp.arange(SB_BLOCK)
    sb_cs = jnp.concatenate([(jj[:, None] > jj[None, :]).astype(BF16),
                             jnp.ones((SB_BLOCK, SB_BLOCK), BF16)], axis=1)

    memkv = _norm_matmul(mem.reshape(b * N_MEM, c), row(norm_mem), bf(xa_wkv), BF16,
                         _tile(b * N_MEM, 512), _tile(2 * c, 1024)).reshape(b, N_MEM, 2 * c)

    h = x
    for i in range(depth):
        j = i // 2
        if i % 2 == 0:
            r, lw, k, v, kkn, bb, g = _rwkv_proj(
                h, row(norm_mix[i]), rw_mix[j].astype(F32), bf(rw_wr[j]), bf(rw_wk[j]), bf(rw_wv[j]),
                bf(rw_w1[j]), bf(rw_w2[j]), bf(rw_a1[j]), bf(rw_a2[j]), bf(rw_g1[j]), bf(rw_g2[j]),
                row(rw_w0[j]), row(rw_a0[j]), row(rw_kk[j]), row(rw_ka[j]), ones_quad, _tile(s, 256))
            yg = _rwkv_recurrence(r, lw, k, v, kkn, bb, g, row(rw_rk[j]), row(rw_lnx_w[j]),
                                  row(rw_lnx_b[j]), tri_incl, ones_pair, _tile(s, 512), 4 * LANES)
            h = _matmul_residual(yg.reshape(t, c), bf(rw_wo[j]), h.reshape(t, c),
                                 _tile(t, 1024)).reshape(b, s, c)
        else:
            qkv = _norm_matmul(h.reshape(t, c), row(norm_mix[i]), bf(sb_wqkv[j]), BF16,
                               _tile(t, 1024), _tile(3 * c, 1024)).reshape(b, s, 3 * c)
            o = _sb_attention(qkv, sb_cs)
            h = _matmul_residual(o.reshape(t, c), bf(sb_wo[j]), h.reshape(t, c),
                                 _tile(t, 1024)).reshape(b, s, c)
        h = _cross_block(h, row(norm_cross[i]), bf(xa_wq[i]), memkv, bf(xa_wo[i]), _tile(s, 512))

        w_router = jnp.zeros((c, ROUTER_LANES), F32)
        w_router = w_router.at[:, :N_GROUPS].set(moe_wg[i]).at[:, EXPERT_LANE0:EXPERT_LANE0 + N_EXPERTS].set(moe_we[i])
        b_router = jnp.zeros((1, ROUTER_LANES), F32)
        b_router = b_router.at[0, :N_GROUPS].set(moe_bg[i]).at[0, EXPERT_LANE0:EXPERT_LANE0 + N_EXPERTS].set(moe_be[i])
        f = moe_w1.shape[-1]
        h = _moe_dense(h.reshape(t, c), row(norm_ffn[i]), w_router, b_router,
                       bf(moe_w1[i]).reshape(N_EXPERTS, c, f), bf(moe_w3[i]).reshape(N_EXPERTS, c, f),
                       bf(moe_w2[i]).reshape(N_EXPERTS, f, c), row(norm_final),
                       i == depth - 1, _tile(t, 1024)).reshape(b, s, c)
    return h
```

```python
import functools

import jax
import jax.numpy as jnp
from jax import lax
from jax.experimental import pallas as pl
from jax.experimental.pallas import tpu as pltpu

F32 = jnp.float32
BF16 = jnp.bfloat16

HEAD_DIM = 64
N_MEM = 256
X_HEADS = 4
N_GROUPS = 4
EXPERTS_PER_GROUP = 8
N_EXPERTS = N_GROUPS * EXPERTS_PER_GROUP
D_EXPERT = 256
GN_EPS = 64e-5
NORM_EPS = 1e-6
SB_BLOCK = 128
CHUNK = 64

LANES = 128
MXU_DIM = 256
VMEM_LIMIT = 48 * 1024 * 1024

ROUTER_LANES = LANES


def _cparams(*sem):
    return pltpu.CompilerParams(dimension_semantics=sem, vmem_limit_bytes=VMEM_LIMIT)


def _dot(a, b):
    return jnp.dot(a, b, preferred_element_type=F32)


def _dot_nt(a, b):
    return lax.dot_general(a, b, (((1,), (1,)), ((), ())), preferred_element_type=F32)


def _split2(x):
    hi = x.astype(BF16)
    lo = (x - hi.astype(F32)).astype(BF16)
    return hi, lo


def _split3(x):
    hi = x.astype(BF16)
    r1 = x - hi.astype(F32)
    mid = r1.astype(BF16)
    lo = (r1 - mid.astype(F32)).astype(BF16)
    return hi, mid, lo


def _dot_exact_rhs(x, m_bf16, parts):
    pieces = _split3(x) if parts == 3 else _split2(x)
    if parts * x.shape[1] <= MXU_DIM:
        return _dot(jnp.concatenate(pieces, axis=1), jnp.concatenate([m_bf16] * parts, axis=0))
    acc = _dot(pieces[0], m_bf16)
    for p in pieces[1:]:
        acc = acc + _dot(p, m_bf16)
    return acc


def _dot_exact_lhs3(m3_bf16, x):
    return _dot(m3_bf16, jnp.concatenate(_split3(x), axis=0))


def _rms(x, g):
    ms = jnp.mean(x * x, axis=-1, keepdims=True)
    return x * lax.rsqrt(ms + NORM_EPS) * g


def _head_sum(x, ones_bd):
    c = x.shape[-1]
    outs = []
    for j in range(c // MXU_DIM):
        outs.append(_dot_exact_rhs(x[:, j * MXU_DIM:(j + 1) * MXU_DIM], ones_bd, 2))
    return jnp.concatenate(outs, axis=-1) if len(outs) > 1 else outs[0]


def _norm_mm_kernel(x_ref, g_ref, w_ref, o_ref, xn_ref):
    @pl.when(pl.program_id(1) == 0)
    def _():
        xn_ref[...] = _rms(x_ref[...], g_ref[...]).astype(BF16)

    o_ref[...] = _dot(xn_ref[...], w_ref[...]).astype(o_ref.dtype)


def _norm_matmul(x, g, w, out_dtype, tm, tn):
    m, c = x.shape
    n = w.shape[1]
    return pl.pallas_call(
        _norm_mm_kernel,
        grid=(m // tm, n // tn),
        in_specs=[
            pl.BlockSpec((tm, c), lambda i, j: (i, 0)),
            pl.BlockSpec((1, c), lambda i, j: (0, 0)),
            pl.BlockSpec((c, tn), lambda i, j: (0, j)),
        ],
        out_specs=pl.BlockSpec((tm, tn), lambda i, j: (i, j)),
        out_shape=jax.ShapeDtypeStruct((m, n), out_dtype),
        scratch_shapes=[pltpu.VMEM((tm, c), BF16)],
        compiler_params=_cparams("parallel", "arbitrary"),
        name="norm_matmul",
    )(x, g, w)


def _mm_res_kernel(x_ref, w_ref, r_ref, o_ref):
    o_ref[...] = r_ref[...] + _dot(x_ref[...], w_ref[...])


def _matmul_residual(x, w, res, tm):
    m, k = x.shape
    n = w.shape[1]
    return pl.pallas_call(
        _mm_res_kernel,
        grid=(m // tm,),
        in_specs=[
            pl.BlockSpec((tm, k), lambda i: (i, 0)),
            pl.BlockSpec((k, n), lambda i: (0, 0)),
            pl.BlockSpec((tm, n), lambda i: (i, 0)),
        ],
        out_specs=pl.BlockSpec((tm, n), lambda i: (i, 0)),
        out_shape=jax.ShapeDtypeStruct((m, n), F32),
        compiler_params=_cparams("parallel"),
        name="matmul_residual",
    )(x, w, res)


def _softplus(x):
    return jnp.maximum(x, 0.0) + jnp.log1p(jnp.exp(-jnp.abs(x)))


def _rwkv_proj_kernel(h_ref, gn_ref, mix_ref, wr_ref, wk_ref, wv_ref, w1_ref, w2_ref, a1_ref, a2_ref,
                      g1_ref, g2_ref, w0_ref, a0_ref, kk_ref, ka_ref, ones_ref,
                      r_out, lw_out, k_out, v_out, kkn_out, b_out, g_out, carry_ref):
    ts = h_ref.shape[1]
    hn = _rms(h_ref[0], gn_ref[...])

    @pl.when(pl.program_id(1) == 0)
    def _():
        carry_ref[...] = jnp.zeros_like(carry_ref)

    prev_last = carry_ref[0:1, :]
    row = lax.broadcasted_iota(jnp.int32, hn.shape, 0)
    hp = jnp.where(row == 0, prev_last, pltpu.roll(hn, 1, 0))
    carry_ref[0:1, :] = hn[ts - 1:ts, :]
    dx = hp - hn

    def mixed(i):
        return (hn + dx * mix_ref[i:i + 1, :]).astype(BF16)

    r = _dot(mixed(0), wr_ref[...])
    lora_w = _dot(jnp.tanh(_dot(mixed(1), w1_ref[...])).astype(BF16), w2_ref[...])
    k = _dot(mixed(2), wk_ref[...])
    v = _dot(mixed(3), wv_ref[...])
    lora_a = _dot(_dot(mixed(4), a1_ref[...]).astype(BF16), a2_ref[...])
    g = _dot(jax.nn.sigmoid(_dot(mixed(5), g1_ref[...])).astype(BF16), g2_ref[...])

    w_log = -_softplus(-(w0_ref[...] + lora_w)) - 0.5
    lw = -jnp.exp(w_log)
    a = jax.nn.sigmoid(a0_ref[...] + lora_a)

    kk = k * kk_ref[...]
    nrm = jnp.sqrt(_head_sum(kk * kk, ones_ref[...]))
    kkn = kk / jnp.maximum(nrm, 1e-12)
    k_mod = k * (1.0 + (a - 1.0) * ka_ref[...])

    r_out[0] = r
    lw_out[0] = lw
    k_out[0] = k_mod
    v_out[0] = v
    kkn_out[0] = kkn
    b_out[0] = kkn * a
    g_out[0] = g


def _rwkv_proj(h, gn, mix, wr, wk, wv, w1, w2, a1, a2, g1, g2, w0, a0, k_k, k_a, ones_bd, ts):
    b, s, c = h.shape
    full = lambda arr: pl.BlockSpec(arr.shape, lambda i, j: (0,) * arr.ndim)
    tok = pl.BlockSpec((1, ts, c), lambda i, j: (i, j, 0))
    params = (gn, mix, wr, wk, wv, w1, w2, a1, a2, g1, g2, w0, a0, k_k, k_a, ones_bd)
    return pl.pallas_call(
        _rwkv_proj_kernel,
        grid=(b, s // ts),
        in_specs=[tok] + [full(p) for p in params],
        out_specs=[tok] * 7,
        out_shape=[jax.ShapeDtypeStruct((b, s, c), F32)] * 7,
        scratch_shapes=[pltpu.VMEM((8, c), F32)],
        compiler_params=_cparams("parallel", "arbitrary"),
        name="rwkv_proj",
    )(h, *params)


def _rwkv_rec_kernel(r_ref, lw_ref, k_ref, v_ref, kk_ref, b_ref, g_ref, rk_ref, lnw_ref, lnb_ref,
                     tri_ref, ones_ref, o_ref,
                     state_ref, xp_ref, vp_ref, y0_ref, bdt_ref, svt_ref, dl_ref, bon_ref):
    ts, lw_lanes = r_ref.shape[1], r_ref.shape[2]
    n_chunks = ts // CHUNK
    n_groups = lw_lanes // LANES
    L = CHUNK
    prep_chunks = 2 if n_chunks % 2 == 0 else 1

    @pl.when(pl.program_id(2) == 0)
    def _():
        state_ref[...] = jnp.zeros_like(state_ref)

    lane = lax.broadcasted_iota(jnp.int32, (L, LANES), 1)
    head0 = lane < HEAD_DIM
    rr = lax.broadcasted_iota(jnp.int32, (2 * L, 2 * L), 0)
    cc = lax.broadcasted_iota(jnp.int32, (2 * L, 2 * L), 1)
    same_head = (rr // L) == (cc // L)
    strict = same_head & ((cc % L) < (rr % L))
    incl = same_head & ((cc % L) <= (rr % L))
    eye = (rr == cc).astype(F32)
    tri_incl = tri_ref[...]
    ones_bd = ones_ref[...]

    def stack_heads(x):
        return jnp.concatenate([jnp.where(head0, x, 0.0), jnp.where(head0, 0.0, x)], axis=0)

    def unstack(x):
        return x[:L] + x[L:]

    def prepare(ci, carry):
        chains = [(ci * prep_chunks + cc, gi) for cc in range(prep_chunks) for gi in range(n_groups)]
        rows = [pl.ds(pl.multiple_of(c * L, L), L) for c, _ in chains]
        lanes = [slice(gi * LANES, (gi + 1) * LANES) for _, gi in chains]
        idxs = [c * n_groups + gi for c, gi in chains]
        each = lambda f, *cols: [f(*a) for a in zip(*cols)]
        load = lambda ref: each(lambda rw, ls: ref[0, rw, ls], rows, lanes)
        r, lw, k, v, kk, bb = (load(ref) for ref in (r_ref, lw_ref, k_ref, v_ref, kk_ref, b_ref))

        cum = each(lambda x: _dot_exact_lhs3(tri_incl, x), lw)
        d_inv = each(lambda c_: jnp.exp(-c_), cum)
        d_last = each(lambda c_: jnp.exp(c_[L - 1:L, :]), cum)
        xa = each(lambda c_, l_, kk_: stack_heads(-(jnp.exp(c_ - l_) * kk_)).astype(BF16), cum, lw, kk)
        xr32 = each(lambda c_, r_: stack_heads(jnp.exp(c_) * r_), cum, r)
        bt32 = each(lambda b_, d_: b_ * d_, bb, d_inv)
        kt32 = each(lambda k_, d_: k_ * d_, k, d_inv)

        def gram(xa_, xr_, bt_, kt_):
            bt16, kt16 = bt_.astype(BF16), kt_.astype(BF16)
            z_all = jnp.concatenate([bt16, bt16, kt16, kt16], axis=0)
            x_all = jnp.concatenate([xa_, xr_.astype(BF16)], axis=0)
            return _dot_nt(x_all, z_all)

        gmat = each(gram, xa, xr32, bt32, kt32)
        m_ab = each(lambda g_: jnp.where(strict, g_[:2 * L, :2 * L], 0.0), gmat)
        m_ak = each(lambda g_: jnp.where(strict, g_[:2 * L, 2 * L:], 0.0).astype(BF16), gmat)
        m_rb = each(lambda g_: jnp.where(incl, g_[2 * L:, :2 * L], 0.0).astype(BF16), gmat)
        m_rk = each(lambda g_: jnp.where(incl, g_[2 * L:, 2 * L:], 0.0).astype(BF16), gmat)

        p = m_ab
        t_inv = each(lambda m_: eye + m_, m_ab)
        for _ in range(5):
            p = each(lambda p_: _dot(p_.astype(BF16), p_.astype(BF16)), p)
            t_inv = each(lambda t_, p_: t_ + _dot(t_.astype(BF16), p_.astype(BF16)), t_inv, p)
        t16 = each(lambda t_: t_.astype(BF16), t_inv)

        v_st = each(lambda v_: stack_heads(v_).astype(BF16), v)
        mv = each(_dot, m_ak, v_st)
        tav = each(lambda t_, xa_, mv_: _dot(t_, jnp.concatenate([xa_, mv_.astype(BF16)], axis=1)),
                   t16, xa, mv)
        rby = each(lambda m_, tav_: _dot(m_, tav_.astype(BF16)), m_rb, tav)
        y0 = each(lambda rby_, m_, v_: rby_[:, LANES:] + _dot(m_, v_), rby, m_rk, v_st)
        svt = each(lambda kt_, dl_, v_: _dot(stack_heads(kt_ * dl_).T.astype(BF16), v_), kt32, d_last, v_st)
        for n, idx in enumerate(idxs):
            xp_ref[idx] = jnp.concatenate([tav[n][:, :LANES], xr32[n] + rby[n][:, :LANES]], axis=0).astype(BF16)
            vp_ref[idx] = tav[n][:, LANES:]
            y0_ref[idx] = y0[n]
            bdt_ref[idx] = stack_heads(bt32[n] * d_last[n]).T.astype(BF16)
            svt_ref[idx] = svt[n]
            dl_ref[idx] = jnp.broadcast_to(d_last[n], (LANES, LANES)).T
            bon_ref[rows[n], lanes[n]] = _dot_exact_rhs(r[n] * k[n] * rk_ref[:, lanes[n]], ones_bd, 2) * v[n]
        return carry

    def advance(ci, carry):
        rows = pl.ds(pl.multiple_of(ci * L, L), L)
        groups = list(range(n_groups))
        lanes = [slice(gi * LANES, (gi + 1) * LANES) for gi in groups]
        idxs = [ci * n_groups + gi for gi in groups]
        each = lambda f, *cols: [f(*a) for a in zip(*cols)]
        s_kv = each(lambda gi: state_ref[gi], groups)
        xs = each(lambda i_, s_: _dot(xp_ref[i_], s_.astype(BF16)), idxs, s_kv)
        u16 = each(lambda x_, i_: (x_[:2 * L] + vp_ref[i_]).astype(BF16), xs, idxs)
        upd = each(lambda i_, u_: _dot(bdt_ref[i_], u_), idxs, u16)
        for gi in groups:
            state_ref[gi] = s_kv[gi] * dl_ref[idxs[gi]] + svt_ref[idxs[gi]] + upd[gi]

        inv_n = 1.0 / HEAD_DIM
        y = each(lambda x_, i_: unstack(x_[2 * L:] + y0_ref[i_]), xs, idxs)
        yc = each(lambda y_: y_ - _dot_exact_rhs(y_, ones_bd, 2) * inv_n, y)
        var = each(lambda yc_: _dot_exact_rhs(yc_ * yc_, ones_bd, 2) * inv_n, yc)
        for gi in groups:
            ls = lanes[gi]
            yn = yc[gi] * lax.rsqrt(var[gi] + GN_EPS) * lnw_ref[:, ls] + lnb_ref[:, ls]
            o_ref[0, rows, ls] = ((yn + bon_ref[rows, ls]) * g_ref[0, rows, ls]).astype(o_ref.dtype)
        return carry

    lax.fori_loop(0, n_chunks // prep_chunks, prepare, 0)
    lax.fori_loop(0, n_chunks, advance, 0)


def _rwkv_recurrence(r, lw, k, v, kkn, bb, g, r_k, lnx_w, lnx_b, tri, ones_pair, ts, lane_w):
    b, s, c = r.shape
    tok = pl.BlockSpec((1, ts, lane_w), lambda i, j, t: (i, t, j))
    vec = pl.BlockSpec((1, lane_w), lambda i, j, t: (0, j))
    small = lambda arr: pl.BlockSpec(arr.shape, lambda i, j, t: (0,) * arr.ndim)
    n_groups = lane_w // LANES
    ncg = (ts // CHUNK) * n_groups
    return pl.pallas_call(
        _rwkv_rec_kernel,
        grid=(b, c // lane_w, s // ts),
        in_specs=[tok] * 7 + [vec, vec, vec, small(tri), small(ones_pair)],
        out_specs=tok,
        out_shape=jax.ShapeDtypeStruct((b, s, c), BF16),
        scratch_shapes=[pltpu.VMEM((n_groups, LANES, LANES), F32),
                        pltpu.VMEM((ncg, 4 * CHUNK, LANES), BF16),
                        pltpu.VMEM((ncg, 2 * CHUNK, LANES), F32),
                        pltpu.VMEM((ncg, 2 * CHUNK, LANES), F32),
                        pltpu.VMEM((ncg, LANES, 2 * CHUNK), BF16),
                        pltpu.VMEM((ncg, LANES, LANES), F32),
                        pltpu.VMEM((ncg, LANES, LANES), F32),
                        pltpu.VMEM((ts, lane_w), F32)],
        compiler_params=_cparams("parallel", "parallel", "arbitrary"),
        name="rwkv_recurrence",
    )(r, lw, k, v, kkn, bb, g, r_k, lnx_w, lnx_b, tri, ones_pair)


SB_QTILE = 512
SB_SUBTILE = 128
SB_STAGE_LAG = 2
SB_SKIP_BELOW = -104.0


def _sb_attn_kernel(q_ref, k_ref, v_ref, cs_ref, o_ref, qs_ref, acc_ref, run_ref, tmax_ref):
    s = q_ref.shape[1]
    tb = SB_BLOCK
    tq = min(SB_QTILE, s)
    rs = SB_SUBTILE
    bpt = tq // tb
    kpb = 2 if bpt % 2 == 0 else 1
    nb, nt = s // tb, s // tq
    shift = bpt.bit_length() - 1
    lane = lax.broadcasted_iota(jnp.int32, (s, LANES), 1)
    head0 = lane < HEAD_DIM
    q = q_ref[0] * (HEAD_DIM ** -0.5)
    qs_ref[0] = jnp.where(head0, q, jnp.zeros_like(q))
    qs_ref[1] = jnp.where(head0, jnp.zeros_like(q), q)
    acc_ref[...] = jnp.zeros_like(acc_ref)
    run_ref[...] = jnp.zeros_like(run_ref)
    for ti in range(nt):
        tmax_ref[ti] = 0.0
    cs = cs_ref[...]
    cs2 = jnp.concatenate([cs, cs], axis=0)

    def tile_step(ti, blocks, pieces, track_max):
        r0 = ti * tq
        subs = [(kbi, h, pl.ds(pl.multiple_of(r0 + i * rs, rs), rs), r0 + i * rs, diag)
                for kbi, i, diag in pieces for h in range(2)]
        n = len(subs)
        lag = SB_STAGE_LAG
        z, lb, parts, sums, att, before = ({} for _ in range(6))
        top = None
        for step in range(n + 2 + 2 * lag):
            if step < n:
                kbi, h, rows, _, _ = subs[step]
                z[step] = _dot_nt(qs_ref[h, rows, :], blocks[kbi][1])
            i = step - 1
            if 0 <= i < n:
                zz = z.pop(i)
                sp = jnp.log(1.0 + jnp.exp(-jnp.abs(zz)))
                lb[i] = jnp.minimum(zz, 0.0) - sp
                log_fail = lb[i] - zz
                if subs[i][4]:
                    t_idx = subs[i][3] + lax.broadcasted_iota(jnp.int32, (rs, tb), 0)
                    s_idx = blocks[subs[i][0]][0] + lax.broadcasted_iota(jnp.int32, (rs, tb), 1)
                    before[i] = s_idx < t_idx
                    log_fail = jnp.where(before[i], log_fail, 0.0)
                parts[i] = _split2(log_fail)
            i = step - 1 - lag
            if 0 <= i < n:
                hi, lo = parts.pop(i)
                sums[i] = _dot(jnp.concatenate([hi, lo], axis=1), cs2)
            i = step - 2 - lag
            if 0 <= i < n:
                kbi, h, rows, _, diag = subs[i]
                sm = sums.pop(i)
                run = run_ref[h, rows, :]
                a = jnp.exp(lb.pop(i) + run + sm[:, :tb])
                if diag:
                    a = jnp.where(before.pop(i), a, 0.0)
                att[i] = a.astype(BF16)
                run = run + sm[:, tb:]
                run_ref[h, rows, :] = run
                if track_max and kbi == len(blocks) - 1:
                    top = run if top is None else jnp.maximum(top, run)
            i = step - 2 - 2 * lag
            if 0 <= i < n:
                kbi, h, rows, _, _ = subs[i]
                acc_ref[h, rows, :] += _dot(att.pop(i), blocks[kbi][2])
        if track_max:
            tmax_ref[ti] = jnp.max(top)

    sub_per_block = tb // rs
    all_pieces = [(kbi, i, False) for kbi in range(kpb) for i in range(tq // rs)]

    def diagonal_pieces(u):
        jl = kpb * u + kpb - 1
        out = []
        for kbi in range(kpb):
            kl = jl - kbi
            for qb in range(kl, bpt):
                for i in range(qb * sub_per_block, (qb + 1) * sub_per_block):
                    out.append((kbi, i, qb == kl))
        return out

    def key_blocks(jj, carry):
        j = nb - 1 - kpb * jj
        blocks = []
        for d in range(kpb):
            k0 = pl.multiple_of((j - d) * tb, tb)
            blocks.append((k0, k_ref[0, pl.ds(k0, tb), :], v_ref[0, pl.ds(k0, tb), :]))
        t0 = lax.shift_right_logical(j, shift)
        case = lax.shift_right_logical(j - t0 * bpt, kpb.bit_length() - 1)
        for u in range(bpt // kpb):
            @pl.when(case == u)
            def _():
                tile_step(t0, blocks, diagonal_pieces(u), False)

        def below(ti, c):
            @pl.when(tmax_ref[ti] >= SB_SKIP_BELOW)
            def _():
                tile_step(ti, blocks, all_pieces, True)
            return c

        lax.fori_loop(t0 + 1, nt, below, 0)
        return carry

    lax.fori_loop(0, nb // kpb, key_blocks, 0)
    o_ref[0] = jnp.where(head0, acc_ref[0], acc_ref[1]).astype(o_ref.dtype)


def _sb_attention(qkv, cs):
    b, s, c3 = qkv.shape
    c = c3 // 3
    n_pairs = c // LANES
    return pl.pallas_call(
        _sb_attn_kernel,
        grid=(b, n_pairs),
        in_specs=[
            pl.BlockSpec((1, s, LANES), lambda i, p: (i, 0, p)),
            pl.BlockSpec((1, s, LANES), lambda i, p: (i, 0, n_pairs + p)),
            pl.BlockSpec((1, s, LANES), lambda i, p: (i, 0, 2 * n_pairs + p)),
            pl.BlockSpec(cs.shape, lambda i, p: (0, 0)),
        ],
        out_specs=pl.BlockSpec((1, s, LANES), lambda i, p: (i, 0, p)),
        out_shape=jax.ShapeDtypeStruct((b, s, c), BF16),
        scratch_shapes=[pltpu.VMEM((2, s, LANES), BF16), pltpu.VMEM((2, s, LANES), F32),
                        pltpu.VMEM((2, s, SB_BLOCK), F32),
                        pltpu.SMEM((max(s // SB_QTILE, 1),), F32)],
        compiler_params=_cparams("parallel", "parallel"),
        name="sb_attention",
    )(qkv, qkv, qkv, cs)


def _cross_kernel(h_ref, gn_ref, wq_ref, kv_ref, wo_ref, o_ref):
    h = h_ref[0]
    c = h.shape[-1]
    xhd = c // X_HEADS
    q = _dot(_rms(h, gn_ref[...]).astype(BF16), wq_ref[...]).astype(BF16)
    outs = []
    for hd in range(X_HEADS):
        qh = q[:, hd * xhd:(hd + 1) * xhd]
        kh = kv_ref[0, :, hd * xhd:(hd + 1) * xhd]
        vh = kv_ref[0, :, c + hd * xhd:c + (hd + 1) * xhd]
        sc = _dot_nt(qh, kh) * (xhd ** -0.5)
        sc = sc - jnp.max(sc, axis=-1, keepdims=True)
        e = jnp.exp(sc)
        p = e / jnp.sum(e, axis=-1, keepdims=True)
        outs.append(_dot(p.astype(BF16), vh).astype(BF16))
    o = jnp.concatenate(outs, axis=-1)
    o_ref[0] = h + _dot(o, wo_ref[...])


def _cross_block(h, gn, wq, memkv, wo, tq):
    b, s, c = h.shape
    tok = pl.BlockSpec((1, tq, c), lambda i, t: (i, t, 0))
    full = lambda arr: pl.BlockSpec(arr.shape, lambda i, t: (0,) * arr.ndim)
    return pl.pallas_call(
        _cross_kernel,
        grid=(b, s // tq),
        in_specs=[tok, full(gn), full(wq),
                  pl.BlockSpec((1,) + memkv.shape[1:], lambda i, t: (i, 0, 0)), full(wo)],
        out_specs=tok,
        out_shape=jax.ShapeDtypeStruct((b, s, c), F32),
        compiler_params=_cparams("parallel", "parallel"),
        name="cross_attention",
    )(h, gn, wq, memkv, wo)


def _lane_min_index(mask, lane):
    return jnp.min(jnp.where(mask, lane, float(ROUTER_LANES)), axis=-1, keepdims=True)


MOE_ROUTE_TILE = 512
MOE_ROW_TILE = 256
DMA_UNROLL = 8


def _dot_3pass(a, b):
    ah, al = _split2(a)
    bh, bl = _split2(b)
    return _dot(ah, bl) + _dot(al, bh) + _dot(ah, bh)


def _moe_route_kernel(h_ref, gn_ref, wg_ref, bg_ref, tri_ref, dest_ref, pg_ref, cnt_ref, base_ref, *, n_tokens):
    @pl.when(pl.program_id(0) == 0)
    def _():
        base_ref[...] = jnp.zeros_like(base_ref)

    hn = _rms(h_ref[...], gn_ref[...])
    gl = _dot_3pass(hn, wg_ref[...]) + bg_ref[...]
    lane = lax.broadcasted_iota(jnp.int32, gl.shape, 1).astype(F32)
    gmax = jnp.max(gl, axis=-1, keepdims=True)
    group = _lane_min_index(gl == gmax, lane)
    p_group = 1.0 / jnp.sum(jnp.exp(gl - gmax), axis=-1, keepdims=True)
    onehot = (lane == group).astype(F32)
    earlier = _dot(tri_ref[...], onehot.astype(BF16))
    base = base_ref[0:1, :]
    rank = jnp.sum(onehot * (earlier + base), axis=-1, keepdims=True)
    dest = group * float(n_tokens) + rank
    tm = dest.shape[0]
    dest_row = jnp.broadcast_to(dest, (tm, LANES)).T[0:1, :]
    dest_ref[0] = dest_row.astype(jnp.int32)
    pg_ref[...] = jnp.broadcast_to(p_group, pg_ref.shape)
    new_base = base + earlier[tm - 1:tm, :] + onehot[tm - 1:tm, :]
    base_ref[0:1, :] = new_base
    cnt_ref[...] = jnp.broadcast_to(new_base, cnt_ref.shape)


def _moe_route(h, gn, wg128, bg128, tri, tm):
    t, c = h.shape
    full = lambda arr: pl.BlockSpec(arr.shape, lambda i: (0,) * arr.ndim)
    return pl.pallas_call(
        functools.partial(_moe_route_kernel, n_tokens=t),
        grid=(t // tm,),
        in_specs=[pl.BlockSpec((tm, c), lambda i: (i, 0)), full(gn), full(wg128), full(bg128), full(tri)],
        out_specs=[pl.BlockSpec((1, 1, tm), lambda i: (i, 0, 0)),
                   pl.BlockSpec((tm, LANES), lambda i: (i, 0)),
                   pl.BlockSpec((8, LANES), lambda i: (0, 0))],
        out_shape=[jax.ShapeDtypeStruct((t // tm, 1, tm), jnp.int32),
                   jax.ShapeDtypeStruct((t, LANES), F32),
                   jax.ShapeDtypeStruct((8, LANES), F32)],
        scratch_shapes=[pltpu.VMEM((8, LANES), F32)],
        compiler_params=_cparams("arbitrary"),
        name="moe_route",
    )(h, gn, wg128, bg128, tri)


def _moe_scatter_kernel(dest_ref, h_ref, gn_ref, init_ref, xs_ref, buf_ref, sem_ref):
    del init_ref
    i = pl.program_id(0)
    n = pl.num_programs(0)
    tm = h_ref.shape[0]
    slot = lax.rem(i, 2)

    def row_copy(s, r, d):
        return pltpu.make_async_copy(buf_ref.at[s, pl.ds(r, 1), :], xs_ref.at[pl.ds(d, 1), :], sem_ref.at[s])

    def wait_all(s):
        pltpu.make_async_copy(buf_ref.at[s], xs_ref.at[pl.ds(0, tm), :], sem_ref.at[s]).wait()

    buf_ref[slot] = _rms(h_ref[...], gn_ref[...])

    def start(r, c):
        row_copy(slot, r, dest_ref[0, 0, r]).start()
        return c
    lax.fori_loop(0, tm, start, 0, unroll=DMA_UNROLL)

    @pl.when(i > 0)
    def _():
        wait_all(1 - slot)

    @pl.when(i == n - 1)
    def _():
        wait_all(slot)


def _moe_scatter(h, gn, dest3, xs_init, tm):
    t, c = h.shape
    return pl.pallas_call(
        _moe_scatter_kernel,
        grid=(t // tm,),
        in_specs=[pl.BlockSpec((1, 1, tm), lambda i: (i, 0, 0), memory_space=pltpu.SMEM),
                  pl.BlockSpec((tm, c), lambda i: (i, 0)),
                  pl.BlockSpec(gn.shape, lambda i: (0, 0)),
                  pl.BlockSpec(memory_space=pl.ANY)],
        out_specs=pl.BlockSpec(memory_space=pl.ANY),
        out_shape=jax.ShapeDtypeStruct(xs_init.shape, F32),
        scratch_shapes=[pltpu.VMEM((2, tm, c), F32), pltpu.SemaphoreType.DMA((2,))],
        input_output_aliases={3: 0},
        compiler_params=_cparams("arbitrary"),
        name="moe_scatter",
    )(dest3, h, gn, xs_init)


def _moe_group_ffn_kernel(tg_ref, x_ref, we_ref, be_ref, w1_ref, w3_ref, w2_ref, o_ref):
    del tg_ref
    x = x_ref[...]
    xh, xl = _split2(x)
    el = be_ref[0] + _dot(xh, we_ref[0, 1]) + _dot(xl, we_ref[0, 0]) + _dot(xh, we_ref[0, 0])
    lane = lax.broadcasted_iota(jnp.int32, el.shape, 1).astype(F32)
    top1 = jnp.max(el, axis=-1, keepdims=True)
    idx1 = _lane_min_index(el == top1, lane)
    el2 = jnp.where(lane == idx1, -jnp.inf, el)
    top2 = jnp.max(el2, axis=-1, keepdims=True)
    idx2 = _lane_min_index(el2 == top2, lane)
    e2 = jnp.exp(top2 - top1)
    s1 = 1.0 / (1.0 + e2)
    s2 = e2 / (1.0 + e2)
    hids = []
    for e in range(EXPERTS_PER_GROUP):
        ge = jnp.where(idx1 == float(e), s1, jnp.where(idx2 == float(e), s2, 0.0))
        h1 = _dot(xh, w1_ref[0, e])
        h3 = _dot(xh, w3_ref[0, e])
        hids.append(((h1 * jax.nn.sigmoid(h1)) * h3 * ge).astype(BF16))
    o_ref[...] = _dot(jnp.concatenate(hids, axis=1), w2_ref[0])


def _moe_group_ffn(tile_group, xs, we2, be, w1, w3, w2, tm):
    rows, c = xs.shape
    g, e, _, f = w1.shape
    grid_spec = pltpu.PrefetchScalarGridSpec(
        num_scalar_prefetch=1,
        grid=(rows // tm,),
        in_specs=[pl.BlockSpec((tm, c), lambda i, tg: (i, 0)),
                  pl.BlockSpec((1, 2, c, LANES), lambda i, tg: (tg[i], 0, 0, 0)),
                  pl.BlockSpec((1, 1, LANES), lambda i, tg: (tg[i], 0, 0)),
                  pl.BlockSpec((1, e, c, f), lambda i, tg: (tg[i], 0, 0, 0)),
                  pl.BlockSpec((1, e, c, f), lambda i, tg: (tg[i], 0, 0, 0)),
                  pl.BlockSpec((1, e * f, c), lambda i, tg: (tg[i], 0, 0))],
        out_specs=pl.BlockSpec((tm, c), lambda i, tg: (i, 0)),
    )
    return pl.pallas_call(
        _moe_group_ffn_kernel,
        grid_spec=grid_spec,
        out_shape=jax.ShapeDtypeStruct((rows, c), F32),
        compiler_params=_cparams("arbitrary"),
        name="moe_group_ffn",
    )(tile_group, xs, we2, be, w1, w3, w2)


def _moe_gather_kernel(dest_ref, nxt_ref, h_ref, pg_ref, gf_ref, ys_ref, o_ref, buf_ref, sem_ref, *, final_norm):
    i = pl.program_id(0)
    n = pl.num_programs(0)
    tm = h_ref.shape[0]
    slot = lax.rem(i, 2)

    def row_copy(s, r, d):
        return pltpu.make_async_copy(ys_ref.at[pl.ds(d, 1), :], buf_ref.at[s, pl.ds(r, 1), :], sem_ref.at[s])

    def start_all(s, idx_ref):
        def body(r, c):
            row_copy(s, r, idx_ref[0, 0, r]).start()
            return c
        lax.fori_loop(0, tm, body, 0, unroll=DMA_UNROLL)

    @pl.when(i == 0)
    def _():
        start_all(0, dest_ref)

    @pl.when(i + 1 < n)
    def _():
        start_all(1 - slot, nxt_ref)

    pltpu.make_async_copy(ys_ref.at[pl.ds(0, tm), :], buf_ref.at[slot], sem_ref.at[slot]).wait()

    y = buf_ref[slot]
    pg = pg_ref[...]
    c = y.shape[1]
    out = h_ref[...] + y * jnp.concatenate([pg] * (c // LANES), axis=1)
    if final_norm:
        out = _rms(out, gf_ref[...])
    o_ref[...] = out


def _moe_gather(h, pg, dest3, ys, g_final, final_norm, tm):
    t, c = h.shape
    n = t // tm
    return pl.pallas_call(
        functools.partial(_moe_gather_kernel, final_norm=final_norm),
        grid=(n,),
        in_specs=[pl.BlockSpec((1, 1, tm), lambda i: (i, 0, 0), memory_space=pltpu.SMEM),
                  pl.BlockSpec((1, 1, tm), lambda i: (jnp.minimum(i + 1, n - 1), 0, 0), memory_space=pltpu.SMEM),
                  pl.BlockSpec((tm, c), lambda i: (i, 0)),
                  pl.BlockSpec((tm, LANES), lambda i: (i, 0)),
                  pl.BlockSpec(g_final.shape, lambda i: (0, 0)),
                  pl.BlockSpec(memory_space=pl.ANY)],
        out_specs=pl.BlockSpec((tm, c), lambda i: (i, 0)),
        out_shape=jax.ShapeDtypeStruct((t, c), F32),
        scratch_shapes=[pltpu.VMEM((2, tm, c), F32), pltpu.SemaphoreType.DMA((2,))],
        compiler_params=_cparams("arbitrary"),
        name="moe_gather",
    )(dest3, dest3, h, pg, g_final, ys)


def _moe_layer(h, gn, wg, bg, we, be, w1, w3, w2, g_final, final_norm):
    t, c = h.shape
    g, e = N_GROUPS, EXPERTS_PER_GROUP
    f = w1.shape[-1]
    tm_r = MOE_ROUTE_TILE if t % MOE_ROUTE_TILE == 0 else t
    tm = MOE_ROW_TILE if t % MOE_ROW_TILE == 0 else t
    neg = -1e30
    wg128 = jnp.zeros((c, LANES), F32).at[:, :g].set(wg)
    bg128 = jnp.full((1, LANES), neg, F32).at[0, :g].set(bg)
    jj = jnp.arange(tm_r)
    tri = (jj[:, None] > jj[None, :]).astype(BF16)
    dest_local, pg, counts = _moe_route(h, gn, wg128, bg128, tri, tm_r)

    cnt = counts[0, :g].astype(jnp.int32)
    tiles = (cnt + tm - 1) // tm
    tile_end = jnp.cumsum(tiles)
    offs = (tile_end - tiles) * tm
    n_tiles = t // tm + g
    dl = dest_local.reshape(t)
    grp = dl // t
    dest = (dl - grp * t + offs[grp]).reshape(t // tm, 1, tm)
    tile_group = jnp.minimum(jnp.sum(jnp.arange(n_tiles)[:, None] >= tile_end[None, :], axis=1), g - 1).astype(jnp.int32)

    xs = _moe_scatter(h, gn, dest, jnp.zeros((n_tiles * tm, c), F32), tm)
    we_g = jnp.zeros((g, c, LANES), F32).at[:, :, :e].set(we.reshape(c, g, e).transpose(1, 0, 2))
    we2 = jnp.stack(_split2(we_g), axis=1)
    be_g = jnp.full((g, 1, LANES), neg, F32).at[:, 0, :e].set(be.reshape(g, e))
    ys = _moe_group_ffn(tile_group, xs, we2, be_g, w1.astype(BF16), w3.astype(BF16),
                        w2.astype(BF16).reshape(g, e * f, c), tm)
    return _moe_gather(h, pg, dest, ys, g_final, final_norm, tm)


def _block_ones(n, blk):
    i = jnp.arange(n)
    return ((i[:, None] // blk) == (i[None, :] // blk)).astype(BF16)


def _tile(n, pref):
    return pref if n % pref == 0 else n


def kernel(x, mem, norm_mix, norm_cross, norm_ffn, norm_mem, norm_final, rw_mix, rw_wr, rw_wk, rw_wv, rw_w0, rw_w1, rw_w2, rw_a0, rw_a1, rw_a2, rw_g1, rw_g2, rw_kk, rw_ka, rw_rk, rw_lnx_w, rw_lnx_b, rw_wo, sb_wqkv, sb_wo, xa_wq, xa_wkv, xa_wo, moe_wg, moe_bg, moe_we, moe_be, moe_w1, moe_w3, moe_w2):
    b, s, c = x.shape
    t = b * s
    depth = norm_mix.shape[0]
    row = lambda vec: vec.reshape(1, -1).astype(F32)
    bf = lambda w: w.astype(BF16)

    ones_quad = _block_ones(MXU_DIM, HEAD_DIM)
    ones_pair = _block_ones(LANES, HEAD_DIM)
    tri_incl = jnp.tile((jnp.arange(CHUNK)[:, None] >= jnp.arange(CHUNK)[None, :]).astype(BF16), (1, 3))
    jj = jnp.arange(SB_BLOCK)
    sb_cs = jnp.concatenate([(jj[:, None] > jj[None, :]).astype(BF16),
                             jnp.ones((SB_BLOCK, SB_BLOCK), BF16)], axis=1)

    memkv = _norm_matmul(mem.reshape(b * N_MEM, c), row(norm_mem), bf(xa_wkv), BF16,
                         _tile(b * N_MEM, 512), _tile(2 * c, 1024)).reshape(b, N_MEM, 2 * c)

    h = x
    for i in range(depth):
        j = i // 2
        if i % 2 == 0:
            r, lw, k, v, kkn, bb, g = _rwkv_proj(
                h, row(norm_mix[i]), rw_mix[j].astype(F32), bf(rw_wr[j]), bf(rw_wk[j]), bf(rw_wv[j]),
                bf(rw_w1[j]), bf(rw_w2[j]), bf(rw_a1[j]), bf(rw_a2[j]), bf(rw_g1[j]), bf(rw_g2[j]),
                row(rw_w0[j]), row(rw_a0[j]), row(rw_kk[j]), row(rw_ka[j]), ones_quad, _tile(s, 256))
            yg = _rwkv_recurrence(r, lw, k, v, kkn, bb, g, row(rw_rk[j]), row(rw_lnx_w[j]),
                                  row(rw_lnx_b[j]), tri_incl, ones_pair, _tile(s, 512), 4 * LANES)
            h = _matmul_residual(yg.reshape(t, c), bf(rw_wo[j]), h.reshape(t, c),
                                 _tile(t, 1024)).reshape(b, s, c)
        else:
            qkv = _norm_matmul(h.reshape(t, c), row(norm_mix[i]), bf(sb_wqkv[j]), BF16,
                               _tile(t, 1024), _tile(3 * c, 1024)).reshape(b, s, 3 * c)
            o = _sb_attention(qkv, sb_cs)
            h = _matmul_residual(o.reshape(t, c), bf(sb_wo[j]), h.reshape(t, c),
                                 _tile(t, 1024)).reshape(b, s, c)
        h = _cross_block(h, row(norm_cross[i]), bf(xa_wq[i]), memkv, bf(xa_wo[i]), _tile(s, 512))

        h = _moe_layer(h.reshape(t, c), row(norm_ffn[i]), moe_wg[i], moe_bg[i], moe_we[i], moe_be[i],
                       moe_w1[i], moe_w3[i], moe_w2[i], row(norm_final), i == depth - 1).reshape(b, s, c)
    return h
```

```python
import functools

import jax
import jax.numpy as jnp
from jax import lax
from jax.experimental import pallas as pl
from jax.experimental.pallas import tpu as pltpu

F32 = jnp.float32
BF16 = jnp.bfloat16

HEAD_DIM = 64
N_MEM = 256
X_HEADS = 4
N_GROUPS = 4
EXPERTS_PER_GROUP = 8
N_EXPERTS = N_GROUPS * EXPERTS_PER_GROUP
D_EXPERT = 256
GN_EPS = 64e-5
NORM_EPS = 1e-6
SB_BLOCK = 128
CHUNK = 64

LANES = 128
MXU_DIM = 256
VMEM_LIMIT = 48 * 1024 * 1024

ROUTER_LANES = LANES


def _cparams(*sem):
    return pltpu.CompilerParams(dimension_semantics=sem, vmem_limit_bytes=VMEM_LIMIT)


def _dot(a, b):
    return jnp.dot(a, b, preferred_element_type=F32)


def _dot_nt(a, b):
    return lax.dot_general(a, b, (((1,), (1,)), ((), ())), preferred_element_type=F32)


def _split2(x):
    hi = x.astype(BF16)
    lo = (x - hi.astype(F32)).astype(BF16)
    return hi, lo


def _split3(x):
    hi = x.astype(BF16)
    r1 = x - hi.astype(F32)
    mid = r1.astype(BF16)
    lo = (r1 - mid.astype(F32)).astype(BF16)
    return hi, mid, lo


def _dot_exact_rhs(x, m_bf16, parts):
    pieces = _split3(x) if parts == 3 else _split2(x)
    if parts * x.shape[1] <= MXU_DIM:
        return _dot(jnp.concatenate(pieces, axis=1), jnp.concatenate([m_bf16] * parts, axis=0))
    acc = _dot(pieces[0], m_bf16)
    for p in pieces[1:]:
        acc = acc + _dot(p, m_bf16)
    return acc


def _dot_exact_lhs3(m3_bf16, x):
    return _dot(m3_bf16, jnp.concatenate(_split3(x), axis=0))


def _rms(x, g):
    ms = jnp.mean(x * x, axis=-1, keepdims=True)
    return x * lax.rsqrt(ms + NORM_EPS) * g


def _head_sum(x, ones_bd):
    c = x.shape[-1]
    outs = []
    for j in range(c // MXU_DIM):
        outs.append(_dot_exact_rhs(x[:, j * MXU_DIM:(j + 1) * MXU_DIM], ones_bd, 2))
    return jnp.concatenate(outs, axis=-1) if len(outs) > 1 else outs[0]


def _norm_mm_kernel(x_ref, g_ref, w_ref, o_ref, xn_ref):
    @pl.when(pl.program_id(1) == 0)
    def _():
        xn_ref[...] = _rms(x_ref[...], g_ref[...]).astype(BF16)

    o_ref[...] = _dot(xn_ref[...], w_ref[...]).astype(o_ref.dtype)


def _norm_matmul(x, g, w, out_dtype, tm, tn):
    m, c = x.shape
    n = w.shape[1]
    return pl.pallas_call(
        _norm_mm_kernel,
        grid=(m // tm, n // tn),
        in_specs=[
            pl.BlockSpec((tm, c), lambda i, j: (i, 0)),
            pl.BlockSpec((1, c), lambda i, j: (0, 0)),
            pl.BlockSpec((c, tn), lambda i, j: (0, j)),
        ],
        out_specs=pl.BlockSpec((tm, tn), lambda i, j: (i, j)),
        out_shape=jax.ShapeDtypeStruct((m, n), out_dtype),
        scratch_shapes=[pltpu.VMEM((tm, c), BF16)],
        compiler_params=_cparams("parallel", "arbitrary"),
        name="norm_matmul",
    )(x, g, w)


PROJ_SUBTILE = 256


def _softplus(x):
    return jnp.maximum(x, 0.0) + jnp.log(1.0 + jnp.exp(-jnp.abs(x)))


def _rwkv_proj_kernel(h_ref, gn_ref, mix_ref, wr_ref, wk_ref, wv_ref, w1_ref, w2_ref, a1_ref, a2_ref,
                      g1_ref, g2_ref, w0_ref, a0_ref, kk_ref, ka_ref, ones_ref,
                      r_out, lw_out, k_out, v_out, kkn_out, b_out, g_out, carry_ref):
    ts = h_ref.shape[1]
    hn = _rms(h_ref[0], gn_ref[...])

    @pl.when(pl.program_id(1) == 0)
    def _():
        carry_ref[...] = jnp.zeros_like(carry_ref)

    prev_last = carry_ref[0:1, :]
    row = lax.broadcasted_iota(jnp.int32, hn.shape, 0)
    hp = jnp.where(row == 0, prev_last, pltpu.roll(hn, 1, 0))
    carry_ref[0:1, :] = hn[ts - 1:ts, :]
    dx = hp - hn

    def project(rows):
        hn_s, dx_s = hn[rows], dx[rows]
        mixed = lambda i: (hn_s + dx_s * mix_ref[i:i + 1, :]).astype(BF16)
        r = _dot(mixed(0), wr_ref[...])
        lora_w = _dot(jnp.tanh(_dot(mixed(1), w1_ref[...])).astype(BF16), w2_ref[...])
        k = _dot(mixed(2), wk_ref[...])
        v = _dot(mixed(3), wv_ref[...])
        lora_a = _dot(_dot(mixed(4), a1_ref[...]).astype(BF16), a2_ref[...])
        g = _dot(jax.nn.sigmoid(_dot(mixed(5), g1_ref[...])).astype(BF16), g2_ref[...])
        return r, lora_w, k, v, lora_a, g

    def finish(rows, r, lora_w, k, v, lora_a, g):
        w_log = -_softplus(-(w0_ref[...] + lora_w)) - 0.5
        a = jax.nn.sigmoid(a0_ref[...] + lora_a)
        kk = k * kk_ref[...]
        nrm = jnp.sqrt(_head_sum(kk * kk, ones_ref[...]))
        kkn = kk / jnp.maximum(nrm, 1e-12)
        r_out[0, rows] = r.astype(r_out.dtype)
        lw_out[0, rows] = -jnp.exp(w_log)
        k_out[0, rows] = (k * (1.0 + (a - 1.0) * ka_ref[...])).astype(k_out.dtype)
        v_out[0, rows] = v.astype(v_out.dtype)
        kkn_out[0, rows] = kkn.astype(kkn_out.dtype)
        b_out[0, rows] = (kkn * a).astype(b_out.dtype)
        g_out[0, rows] = g.astype(g_out.dtype)

    sub = min(PROJ_SUBTILE, ts)
    pending = None
    for i in range(ts // sub + 1):
        rows = slice(i * sub, (i + 1) * sub)
        cur = (rows,) + project(rows) if i < ts // sub else None
        if pending is not None:
            finish(*pending)
        pending = cur


def _rwkv_proj(h, gn, mix, wr, wk, wv, w1, w2, a1, a2, g1, g2, w0, a0, k_k, k_a, ones_bd, ts):
    b, s, c = h.shape
    full = lambda arr: pl.BlockSpec(arr.shape, lambda i, j: (0,) * arr.ndim)
    tok = pl.BlockSpec((1, ts, c), lambda i, j: (i, j, 0))
    params = (gn, mix, wr, wk, wv, w1, w2, a1, a2, g1, g2, w0, a0, k_k, k_a, ones_bd)
    return pl.pallas_call(
        _rwkv_proj_kernel,
        grid=(b, s // ts),
        in_specs=[tok] + [full(p) for p in params],
        out_specs=[tok] * 7,
        out_shape=[jax.ShapeDtypeStruct((b, s, c), F32 if n == 1 else BF16) for n in range(7)],
        scratch_shapes=[pltpu.VMEM((8, c), F32)],
        compiler_params=_cparams("parallel", "arbitrary"),
        name="rwkv_proj",
    )(h, *params)


def _rwkv_rec_kernel(r_ref, lw_ref, k_ref, v_ref, kk_ref, b_ref, g_ref, rk_ref, lnw_ref, lnb_ref,
                     tri_ref, ones_ref, o_ref,
                     state_ref, xp_ref, vp_ref, y0_ref, bdt_ref, svt_ref, dl_ref, bon_ref):
    ts, lw_lanes = r_ref.shape[1], r_ref.shape[2]
    n_chunks = ts // CHUNK
    n_groups = lw_lanes // LANES
    L = CHUNK
    prep_chunks = 2 if n_chunks % 2 == 0 else 1

    @pl.when(pl.program_id(2) == 0)
    def _():
        state_ref[...] = jnp.zeros_like(state_ref)

    lane = lax.broadcasted_iota(jnp.int32, (L, LANES), 1)
    head0 = lane < HEAD_DIM
    rr = lax.broadcasted_iota(jnp.int32, (2 * L, 2 * L), 0)
    cc = lax.broadcasted_iota(jnp.int32, (2 * L, 2 * L), 1)
    same_head = (rr // L) == (cc // L)
    strict = same_head & ((cc % L) < (rr % L))
    incl = same_head & ((cc % L) <= (rr % L))
    eye = (rr == cc).astype(F32)
    tri_incl = tri_ref[...]
    ones_bd = ones_ref[...]

    def stack_heads(x):
        return jnp.concatenate([jnp.where(head0, x, 0.0), jnp.where(head0, 0.0, x)], axis=0)

    def unstack(x):
        return x[:L] + x[L:]

    def prepare(ci, carry):
        chains = [(ci * prep_chunks + cc, gi) for cc in range(prep_chunks) for gi in range(n_groups)]
        rows = [pl.ds(pl.multiple_of(c * L, L), L) for c, _ in chains]
        lanes = [slice(gi * LANES, (gi + 1) * LANES) for _, gi in chains]
        idxs = [c * n_groups + gi for c, gi in chains]
        each = lambda f, *cols: [f(*a) for a in zip(*cols)]
        load = lambda ref: each(lambda rw, ls: ref[0, rw, ls].astype(F32), rows, lanes)
        r, lw, k, v, kk, bb = (load(ref) for ref in (r_ref, lw_ref, k_ref, v_ref, kk_ref, b_ref))

        cum = each(lambda x: _dot_exact_lhs3(tri_incl, x), lw)
        d_inv = each(lambda c_: jnp.exp(-c_), cum)
        d_last = each(lambda c_: jnp.exp(c_[L - 1:L, :]), cum)
        xa = each(lambda c_, l_, kk_: stack_heads(-(jnp.exp(c_ - l_) * kk_)).astype(BF16), cum, lw, kk)
        xr32 = each(lambda c_, r_: stack_heads(jnp.exp(c_) * r_), cum, r)
        bt32 = each(lambda b_, d_: b_ * d_, bb, d_inv)
        kt32 = each(lambda k_, d_: k_ * d_, k, d_inv)

        def gram(xa_, xr_, bt_, kt_):
            bt16, kt16 = bt_.astype(BF16), kt_.astype(BF16)
            z_all = jnp.concatenate([bt16, bt16, kt16, kt16], axis=0)
            x_all = jnp.concatenate([xa_, xr_.astype(BF16)], axis=0)
            return _dot_nt(x_all, z_all)

        gmat = each(gram, xa, xr32, bt32, kt32)
        m_ab = each(lambda g_: jnp.where(strict, g_[:2 * L, :2 * L], 0.0), gmat)
        m_ak = each(lambda g_: jnp.where(strict, g_[:2 * L, 2 * L:], 0.0).astype(BF16), gmat)
        m_rb = each(lambda g_: jnp.where(incl, g_[2 * L:, :2 * L], 0.0).astype(BF16), gmat)
        m_rk = each(lambda g_: jnp.where(incl, g_[2 * L:, 2 * L:], 0.0).astype(BF16), gmat)

        p = m_ab
        t_inv = each(lambda m_: eye + m_, m_ab)
        for _ in range(5):
            p = each(lambda p_: _dot(p_.astype(BF16), p_.astype(BF16)), p)
            t_inv = each(lambda t_, p_: t_ + _dot(t_.astype(BF16), p_.astype(BF16)), t_inv, p)
        t16 = each(lambda t_: t_.astype(BF16), t_inv)

        v_st = each(lambda v_: stack_heads(v_).astype(BF16), v)
        mv = each(_dot, m_ak, v_st)
        tav = each(lambda t_, xa_, mv_: _dot(t_, jnp.concatenate([xa_, mv_.astype(BF16)], axis=1)),
                   t16, xa, mv)
        rby = each(lambda m_, tav_: _dot(m_, tav_.astype(BF16)), m_rb, tav)
        y0 = each(lambda rby_, m_, v_: rby_[:, LANES:] + _dot(m_, v_), rby, m_rk, v_st)
        svt = each(lambda kt_, dl_, v_: _dot(stack_heads(kt_ * dl_).T.astype(BF16), v_), kt32, d_last, v_st)
        for n, idx in enumerate(idxs):
            xp_ref[idx] = jnp.concatenate([tav[n][:, :LANES], xr32[n] + rby[n][:, :LANES]], axis=0).astype(BF16)
            vp_ref[idx] = tav[n][:, LANES:]
            y0_ref[idx] = y0[n]
            bdt_ref[idx] = stack_heads(bt32[n] * d_last[n]).T.astype(BF16)
            svt_ref[idx] = svt[n]
            dl_ref[idx] = jnp.broadcast_to(d_last[n], (LANES, LANES)).T
            bon_ref[rows[n], lanes[n]] = _dot_exact_rhs(r[n] * k[n] * rk_ref[:, lanes[n]], ones_bd, 2) * v[n]
        return carry

    def advance(ci, carry):
        rows = pl.ds(pl.multiple_of(ci * L, L), L)
        groups = list(range(n_groups))
        lanes = [slice(gi * LANES, (gi + 1) * LANES) for gi in groups]
        idxs = [ci * n_groups + gi for gi in groups]
        each = lambda f, *cols: [f(*a) for a in zip(*cols)]
        s_kv = each(lambda gi: state_ref[gi], groups)
        xs = each(lambda i_, s_: _dot(xp_ref[i_], s_.astype(BF16)), idxs, s_kv)
        u16 = each(lambda x_, i_: (x_[:2 * L] + vp_ref[i_]).astype(BF16), xs, idxs)
        upd = each(lambda i_, u_: _dot(bdt_ref[i_], u_), idxs, u16)
        for gi in groups:
            state_ref[gi] = s_kv[gi] * dl_ref[idxs[gi]] + svt_ref[idxs[gi]] + upd[gi]

        inv_n = 1.0 / HEAD_DIM
        y = each(lambda x_, i_: unstack(x_[2 * L:] + y0_ref[i_]), xs, idxs)
        yc = each(lambda y_: y_ - _dot_exact_rhs(y_, ones_bd, 2) * inv_n, y)
        var = each(lambda yc_: _dot_exact_rhs(yc_ * yc_, ones_bd, 2) * inv_n, yc)
        for gi in groups:
            ls = lanes[gi]
            yn = yc[gi] * lax.rsqrt(var[gi] + GN_EPS) * lnw_ref[:, ls] + lnb_ref[:, ls]
            o_ref[0, rows, ls] = ((yn + bon_ref[rows, ls]) * g_ref[0, rows, ls].astype(F32)).astype(o_ref.dtype)
        return carry

    lax.fori_loop(0, n_chunks // prep_chunks, prepare, 0)
    lax.fori_loop(0, n_chunks, advance, 0)


def _rwkv_recurrence(r, lw, k, v, kkn, bb, g, r_k, lnx_w, lnx_b, tri, ones_pair, ts, lane_w):
    b, s, c = r.shape
    tok = pl.BlockSpec((1, ts, lane_w), lambda i, j, t: (i, t, j))
    vec = pl.BlockSpec((1, lane_w), lambda i, j, t: (0, j))
    small = lambda arr: pl.BlockSpec(arr.shape, lambda i, j, t: (0,) * arr.ndim)
    n_groups = lane_w // LANES
    ncg = (ts // CHUNK) * n_groups
    return pl.pallas_call(
        _rwkv_rec_kernel,
        grid=(b, c // lane_w, s // ts),
        in_specs=[tok] * 7 + [vec, vec, vec, small(tri), small(ones_pair)],
        out_specs=tok,
        out_shape=jax.ShapeDtypeStruct((b, s, c), BF16),
        scratch_shapes=[pltpu.VMEM((n_groups, LANES, LANES), F32),
                        pltpu.VMEM((ncg, 4 * CHUNK, LANES), BF16),
                        pltpu.VMEM((ncg, 2 * CHUNK, LANES), F32),
                        pltpu.VMEM((ncg, 2 * CHUNK, LANES), F32),
                        pltpu.VMEM((ncg, LANES, 2 * CHUNK), BF16),
                        pltpu.VMEM((ncg, LANES, LANES), F32),
                        pltpu.VMEM((ncg, LANES, LANES), F32),
                        pltpu.VMEM((ts, lane_w), F32)],
        compiler_params=_cparams("parallel", "parallel", "arbitrary"),
        name="rwkv_recurrence",
    )(r, lw, k, v, kkn, bb, g, r_k, lnx_w, lnx_b, tri, ones_pair)


SB_QTILE = 512
SB_SUBTILE = 128
SB_STAGE_LAG = 3
SB_SKIP_BELOW = -104.0


def _sb_attn_kernel(q_ref, k_ref, v_ref, cs_ref, o_ref, qs_ref, acc_ref, run_ref, tmax_ref):
    s = q_ref.shape[1]
    tb = SB_BLOCK
    tq = min(SB_QTILE, s)
    rs = SB_SUBTILE
    bpt = tq // tb
    kpb = 2 if bpt % 2 == 0 else 1
    nb, nt = s // tb, s // tq
    shift = bpt.bit_length() - 1
    lane = lax.broadcasted_iota(jnp.int32, (s, LANES), 1)
    head0 = lane < HEAD_DIM
    q = q_ref[0] * (HEAD_DIM ** -0.5)
    qs_ref[0] = jnp.where(head0, q, jnp.zeros_like(q))
    qs_ref[1] = jnp.where(head0, jnp.zeros_like(q), q)
    acc_ref[...] = jnp.zeros_like(acc_ref)
    run_ref[...] = jnp.zeros_like(run_ref)
    for ti in range(nt):
        tmax_ref[ti] = 0.0
    cs = cs_ref[...]
    cs2 = jnp.concatenate([cs, cs], axis=0)

    def tile_step(ti, blocks, pieces, track_max):
        r0 = ti * tq
        subs = [(kbi, h, pl.ds(pl.multiple_of(r0 + i * rs, rs), rs), r0 + i * rs, diag)
                for kbi, i, diag in pieces for h in range(2)]
        n = len(subs)
        lag = SB_STAGE_LAG
        z, lb, parts, sums, att, before = ({} for _ in range(6))
        top = None
        for step in range(n + 2 + 2 * lag):
            if step < n:
                kbi, h, rows, _, _ = subs[step]
                z[step] = _dot_nt(qs_ref[h, rows, :], blocks[kbi][1])
            i = step - 1
            if 0 <= i < n:
                zz = z.pop(i)
                sp = jnp.log(1.0 + jnp.exp(-jnp.abs(zz)))
                lb[i] = jnp.minimum(zz, 0.0) - sp
                log_fail = lb[i] - zz
                if subs[i][4]:
                    t_idx = subs[i][3] + lax.broadcasted_iota(jnp.int32, (rs, tb), 0)
                    s_idx = blocks[subs[i][0]][0] + lax.broadcasted_iota(jnp.int32, (rs, tb), 1)
                    before[i] = s_idx < t_idx
                    log_fail = jnp.where(before[i], log_fail, 0.0)
                parts[i] = _split2(log_fail)
            i = step - 1 - lag
            if 0 <= i < n:
                hi, lo = parts.pop(i)
                sums[i] = _dot(jnp.concatenate([hi, lo], axis=1), cs2)
            i = step - 2 - lag
            if 0 <= i < n:
                kbi, h, rows, _, diag = subs[i]
                sm = sums.pop(i)
                run = run_ref[h, rows, :]
                a = jnp.exp(lb.pop(i) + run + sm[:, :tb])
                if diag:
                    a = jnp.where(before.pop(i), a, 0.0)
                att[i] = a.astype(BF16)
                run = run + sm[:, tb:]
                run_ref[h, rows, :] = run
                if track_max and kbi == len(blocks) - 1:
                    top = run if top is None else jnp.maximum(top, run)
            i = step - 2 - 2 * lag
            if 0 <= i < n:
                kbi, h, rows, _, _ = subs[i]
                acc_ref[h, rows, :] += _dot(att.pop(i), blocks[kbi][2])
        if track_max:
            tmax_ref[ti] = jnp.max(top)

    sub_per_block = tb // rs
    all_pieces = [(kbi, i, False) for kbi in range(kpb) for i in range(tq // rs)]

    def diagonal_pieces(u):
        jl = kpb * u + kpb - 1
        out = []
        for kbi in range(kpb):
            kl = jl - kbi
            for qb in range(kl, bpt):
                for i in range(qb * sub_per_block, (qb + 1) * sub_per_block):
                    out.append((kbi, i, qb == kl))
        return out

    def key_blocks(jj, carry):
        j = nb - 1 - kpb * jj
        blocks = []
        for d in range(kpb):
            k0 = pl.multiple_of((j - d) * tb, tb)
            blocks.append((k0, k_ref[0, pl.ds(k0, tb), :], v_ref[0, pl.ds(k0, tb), :]))
        t0 = lax.shift_right_logical(j, shift)
        case = lax.shift_right_logical(j - t0 * bpt, kpb.bit_length() - 1)
        for u in range(bpt // kpb):
            @pl.when(case == u)
            def _():
                tile_step(t0, blocks, diagonal_pieces(u), False)

        def below(ti, c):
            @pl.when(tmax_ref[ti] >= SB_SKIP_BELOW)
            def _():
                tile_step(ti, blocks, all_pieces, True)
            return c

        lax.fori_loop(t0 + 1, nt, below, 0)
        return carry

    lax.fori_loop(0, nb // kpb, key_blocks, 0)
    o_ref[0] = jnp.where(head0, acc_ref[0], acc_ref[1]).astype(o_ref.dtype)


def _sb_attention(qkv, cs):
    b, s, c3 = qkv.shape
    c = c3 // 3
    n_pairs = c // LANES
    return pl.pallas_call(
        _sb_attn_kernel,
        grid=(b, n_pairs),
        in_specs=[
            pl.BlockSpec((1, s, LANES), lambda i, p: (i, 0, p)),
            pl.BlockSpec((1, s, LANES), lambda i, p: (i, 0, n_pairs + p)),
            pl.BlockSpec((1, s, LANES), lambda i, p: (i, 0, 2 * n_pairs + p)),
            pl.BlockSpec(cs.shape, lambda i, p: (0, 0)),
        ],
        out_specs=pl.BlockSpec((1, s, LANES), lambda i, p: (i, 0, p)),
        out_shape=jax.ShapeDtypeStruct((b, s, c), BF16),
        scratch_shapes=[pltpu.VMEM((2, s, LANES), BF16), pltpu.VMEM((2, s, LANES), F32),
                        pltpu.VMEM((2, s, SB_BLOCK), F32),
                        pltpu.SMEM((max(s // SB_QTILE, 1),), F32)],
        compiler_params=_cparams("parallel", "parallel"),
        name="sb_attention",
    )(qkv, qkv, qkv, cs)


def _cross_kernel(y_ref, wy_ref, h_ref, gn_ref, wq_ref, kv_ref, wo_ref, o_ref):
    h = h_ref[0] + _dot(y_ref[0], wy_ref[...])
    c = h.shape[-1]
    xhd = c // X_HEADS
    q = _dot(_rms(h, gn_ref[...]).astype(BF16), wq_ref[...]).astype(BF16)
    outs = []
    for hd in range(X_HEADS):
        qh = q[:, hd * xhd:(hd + 1) * xhd]
        kh = kv_ref[0, :, hd * xhd:(hd + 1) * xhd]
        vh = kv_ref[0, :, c + hd * xhd:c + (hd + 1) * xhd]
        sc = _dot_nt(qh, kh) * (xhd ** -0.5)
        sc = sc - jnp.max(sc, axis=-1, keepdims=True)
        e = jnp.exp(sc)
        p = e / jnp.sum(e, axis=-1, keepdims=True)
        outs.append(_dot(p.astype(BF16), vh).astype(BF16))
    o = jnp.concatenate(outs, axis=-1)
    o_ref[0] = h + _dot(o, wo_ref[...])


def _cross_block(y, wy, h, gn, wq, memkv, wo, tq):
    b, s, c = h.shape
    tok = pl.BlockSpec((1, tq, c), lambda i, t: (i, t, 0))
    full = lambda arr: pl.BlockSpec(arr.shape, lambda i, t: (0,) * arr.ndim)
    return pl.pallas_call(
        _cross_kernel,
        grid=(b, s // tq),
        in_specs=[tok, full(wy), tok, full(gn), full(wq),
                  pl.BlockSpec((1,) + memkv.shape[1:], lambda i, t: (i, 0, 0)), full(wo)],
        out_specs=tok,
        out_shape=jax.ShapeDtypeStruct((b, s, c), F32),
        compiler_params=_cparams("parallel", "parallel"),
        name="cross_attention",
    )(y, wy, h, gn, wq, memkv, wo)


def _lane_min_index(mask, lane):
    return jnp.min(jnp.where(mask, lane, float(ROUTER_LANES)), axis=-1, keepdims=True)


MOE_ROUTE_TILE = 512
MOE_ROW_TILE = 256


def _dot_3pass(a, b):
    ah, al = _split2(a)
    bh, bl = _split2(b)
    return _dot(ah, bl) + _dot(al, bh) + _dot(ah, bh)


def _moe_route_kernel(h_ref, gn_ref, wg_ref, bg_ref, tri_ref, dest_ref, pg_ref, cnt_ref, base_ref, *, n_tokens):
    @pl.when(pl.program_id(0) == 0)
    def _():
        base_ref[...] = jnp.zeros_like(base_ref)

    hn = _rms(h_ref[...], gn_ref[...])
    gl = _dot_3pass(hn, wg_ref[...]) + bg_ref[...]
    lane = lax.broadcasted_iota(jnp.int32, gl.shape, 1).astype(F32)
    gmax = jnp.max(gl, axis=-1, keepdims=True)
    group = _lane_min_index(gl == gmax, lane)
    p_group = 1.0 / jnp.sum(jnp.exp(gl - gmax), axis=-1, keepdims=True)
    onehot = (lane == group).astype(F32)
    earlier = _dot(tri_ref[...], onehot.astype(BF16))
    base = base_ref[0:1, :]
    rank = jnp.sum(onehot * (earlier + base), axis=-1, keepdims=True)
    dest = group * float(n_tokens) + rank
    tm = dest.shape[0]
    dest_row = jnp.broadcast_to(dest, (tm, LANES)).T[0:1, :]
    dest_ref[0] = dest_row.astype(jnp.int32)
    pg_ref[...] = jnp.broadcast_to(p_group, pg_ref.shape)
    new_base = base + earlier[tm - 1:tm, :] + onehot[tm - 1:tm, :]
    base_ref[0:1, :] = new_base
    cnt_ref[...] = jnp.broadcast_to(new_base, cnt_ref.shape)


def _moe_route(h, gn, wg128, bg128, tri, tm):
    t, c = h.shape
    full = lambda arr: pl.BlockSpec(arr.shape, lambda i: (0,) * arr.ndim)
    return pl.pallas_call(
        functools.partial(_moe_route_kernel, n_tokens=t),
        grid=(t // tm,),
        in_specs=[pl.BlockSpec((tm, c), lambda i: (i, 0)), full(gn), full(wg128), full(bg128), full(tri)],
        out_specs=[pl.BlockSpec((1, 1, tm), lambda i: (i, 0, 0)),
                   pl.BlockSpec((tm, LANES), lambda i: (i, 0)),
                   pl.BlockSpec((8, LANES), lambda i: (0, 0))],
        out_shape=[jax.ShapeDtypeStruct((t // tm, 1, tm), jnp.int32),
                   jax.ShapeDtypeStruct((t, LANES), F32),
                   jax.ShapeDtypeStruct((8, LANES), F32)],
        scratch_shapes=[pltpu.VMEM((8, LANES), F32)],
        compiler_params=_cparams("arbitrary"),
        name="moe_route",
    )(h, gn, wg128, bg128, tri)


def _moe_scatter_kernel(dest_ref, h_ref, gn_ref, init_ref, xs_ref, buf_ref, sem_ref):
    del init_ref
    i = pl.program_id(0)
    n = pl.num_programs(0)
    tm = h_ref.shape[0]
    slot = lax.rem(i, 2)

    def row_copy(s, r, d):
        return pltpu.make_async_copy(buf_ref.at[s, pl.ds(r, 1), :], xs_ref.at[pl.ds(d, 1), :], sem_ref.at[s])

    def wait_all(s):
        pltpu.make_async_copy(buf_ref.at[s], xs_ref.at[pl.ds(0, tm), :], sem_ref.at[s]).wait()

    buf_ref[slot] = _rms(h_ref[...], gn_ref[...])

    for r in range(tm):
        row_copy(slot, r, dest_ref[0, 0, r]).start()

    @pl.when(i > 0)
    def _():
        wait_all(1 - slot)

    @pl.when(i == n - 1)
    def _():
        wait_all(slot)


def _moe_scatter(h, gn, dest3, xs_init, tm):
    t, c = h.shape
    return pl.pallas_call(
        _moe_scatter_kernel,
        grid=(t // tm,),
        in_specs=[pl.BlockSpec((1, 1, tm), lambda i: (i, 0, 0), memory_space=pltpu.SMEM),
                  pl.BlockSpec((tm, c), lambda i: (i, 0)),
                  pl.BlockSpec(gn.shape, lambda i: (0, 0)),
                  pl.BlockSpec(memory_space=pl.ANY)],
        out_specs=pl.BlockSpec(memory_space=pl.ANY),
        out_shape=jax.ShapeDtypeStruct(xs_init.shape, F32),
        scratch_shapes=[pltpu.VMEM((2, tm, c), F32), pltpu.SemaphoreType.DMA((2,))],
        input_output_aliases={3: 0},
        compiler_params=_cparams("arbitrary"),
        name="moe_scatter",
    )(dest3, h, gn, xs_init)


def _moe_group_ffn_kernel(tg_ref, x_ref, we_ref, be_ref, w1_ref, w3_ref, w2_ref, o_ref):
    del tg_ref
    x = x_ref[...]
    xh, xl = _split2(x)
    el = be_ref[0] + _dot(xh, we_ref[0, 1]) + _dot(xl, we_ref[0, 0]) + _dot(xh, we_ref[0, 0])
    lane = lax.broadcasted_iota(jnp.int32, el.shape, 1).astype(F32)
    top1 = jnp.max(el, axis=-1, keepdims=True)
    idx1 = _lane_min_index(el == top1, lane)
    el2 = jnp.where(lane == idx1, -jnp.inf, el)
    top2 = jnp.max(el2, axis=-1, keepdims=True)
    idx2 = _lane_min_index(el2 == top2, lane)
    e2 = jnp.exp(top2 - top1)
    s1 = 1.0 / (1.0 + e2)
    s2 = e2 / (1.0 + e2)
    hids = []
    for e in range(EXPERTS_PER_GROUP):
        ge = jnp.where(idx1 == float(e), s1, jnp.where(idx2 == float(e), s2, 0.0))
        h1 = _dot(xh, w1_ref[0, e])
        h3 = _dot(xh, w3_ref[0, e])
        hids.append(((h1 * jax.nn.sigmoid(h1)) * h3 * ge).astype(BF16))
    o_ref[...] = _dot(jnp.concatenate(hids, axis=1), w2_ref[0])


def _moe_group_ffn(tile_group, xs, we2, be, w1, w3, w2, tm):
    rows, c = xs.shape
    g, e, _, f = w1.shape
    grid_spec = pltpu.PrefetchScalarGridSpec(
        num_scalar_prefetch=1,
        grid=(rows // tm,),
        in_specs=[pl.BlockSpec((tm, c), lambda i, tg: (i, 0)),
                  pl.BlockSpec((1, 2, c, LANES), lambda i, tg: (tg[i], 0, 0, 0)),
                  pl.BlockSpec((1, 1, LANES), lambda i, tg: (tg[i], 0, 0)),
                  pl.BlockSpec((1, e, c, f), lambda i, tg: (tg[i], 0, 0, 0)),
                  pl.BlockSpec((1, e, c, f), lambda i, tg: (tg[i], 0, 0, 0)),
                  pl.BlockSpec((1, e * f, c), lambda i, tg: (tg[i], 0, 0))],
        out_specs=pl.BlockSpec((tm, c), lambda i, tg: (i, 0)),
    )
    return pl.pallas_call(
        _moe_group_ffn_kernel,
        grid_spec=grid_spec,
        out_shape=jax.ShapeDtypeStruct((rows, c), F32),
        compiler_params=_cparams("arbitrary"),
        name="moe_group_ffn",
    )(tile_group, xs, we2, be, w1, w3, w2)


def _moe_gather_kernel(dest_ref, nxt_ref, h_ref, pg_ref, gf_ref, ys_ref, o_ref, buf_ref, sem_ref, *, final_norm):
    i = pl.program_id(0)
    n = pl.num_programs(0)
    tm = h_ref.shape[0]
    slot = lax.rem(i, 2)

    def row_copy(s, r, d):
        return pltpu.make_async_copy(ys_ref.at[pl.ds(d, 1), :], buf_ref.at[s, pl.ds(r, 1), :], sem_ref.at[s])

    def start_all(s, idx_ref):
        for r in range(tm):
            row_copy(s, r, idx_ref[0, 0, r]).start()

    @pl.when(i == 0)
    def _():
        start_all(0, dest_ref)

    @pl.when(i + 1 < n)
    def _():
        start_all(1 - slot, nxt_ref)

    pltpu.make_async_copy(ys_ref.at[pl.ds(0, tm), :], buf_ref.at[slot], sem_ref.at[slot]).wait()

    y = buf_ref[slot]
    pg = pg_ref[...]
    c = y.shape[1]
    out = h_ref[...] + y * jnp.concatenate([pg] * (c // LANES), axis=1)
    if final_norm:
        out = _rms(out, gf_ref[...])
    o_ref[...] = out


def _moe_gather(h, pg, dest3, ys, g_final, final_norm, tm):
    t, c = h.shape
    n = t // tm
    return pl.pallas_call(
        functools.partial(_moe_gather_kernel, final_norm=final_norm),
        grid=(n,),
        in_specs=[pl.BlockSpec((1, 1, tm), lambda i: (i, 0, 0), memory_space=pltpu.SMEM),
                  pl.BlockSpec((1, 1, tm), lambda i: (jnp.minimum(i + 1, n - 1), 0, 0), memory_space=pltpu.SMEM),
                  pl.BlockSpec((tm, c), lambda i: (i, 0)),
                  pl.BlockSpec((tm, LANES), lambda i: (i, 0)),
                  pl.BlockSpec(g_final.shape, lambda i: (0, 0)),
                  pl.BlockSpec(memory_space=pl.ANY)],
        out_specs=pl.BlockSpec((tm, c), lambda i: (i, 0)),
        out_shape=jax.ShapeDtypeStruct((t, c), F32),
        scratch_shapes=[pltpu.VMEM((2, tm, c), F32), pltpu.SemaphoreType.DMA((2,))],
        compiler_params=_cparams("arbitrary"),
        name="moe_gather",
    )(dest3, dest3, h, pg, g_final, ys)


def _moe_layer(h, gn, wg, bg, we, be, w1, w3, w2, g_final, final_norm):
    t, c = h.shape
    g, e = N_GROUPS, EXPERTS_PER_GROUP
    f = w1.shape[-1]
    tm_r = MOE_ROUTE_TILE if t % MOE_ROUTE_TILE == 0 else t
    tm = MOE_ROW_TILE if t % MOE_ROW_TILE == 0 else t
    neg = -1e30
    wg128 = jnp.zeros((c, LANES), F32).at[:, :g].set(wg)
    bg128 = jnp.full((1, LANES), neg, F32).at[0, :g].set(bg)
    jj = jnp.arange(tm_r)
    tri = (jj[:, None] > jj[None, :]).astype(BF16)
    dest_local, pg, counts = _moe_route(h, gn, wg128, bg128, tri, tm_r)

    cnt = counts[0, :g].astype(jnp.int32)
    tiles = (cnt + tm - 1) // tm
    tile_end = jnp.cumsum(tiles)
    offs = (tile_end - tiles) * tm
    n_tiles = t // tm + g
    dl = dest_local.reshape(t)
    grp = dl // t
    dest = (dl - grp * t + offs[grp]).reshape(t // tm, 1, tm)
    tile_group = jnp.minimum(jnp.sum(jnp.arange(n_tiles)[:, None] >= tile_end[None, :], axis=1), g - 1).astype(jnp.int32)

    xs = _moe_scatter(h, gn, dest, jnp.zeros((n_tiles * tm, c), F32), tm)
    we_g = jnp.zeros((g, c, LANES), F32).at[:, :, :e].set(we.reshape(c, g, e).transpose(1, 0, 2))
    we2 = jnp.stack(_split2(we_g), axis=1)
    be_g = jnp.full((g, 1, LANES), neg, F32).at[:, 0, :e].set(be.reshape(g, e))
    ys = _moe_group_ffn(tile_group, xs, we2, be_g, w1.astype(BF16), w3.astype(BF16),
                        w2.astype(BF16).reshape(g, e * f, c), tm)
    return _moe_gather(h, pg, dest, ys, g_final, final_norm, tm)


def _block_ones(n, blk):
    i = jnp.arange(n)
    return ((i[:, None] // blk) == (i[None, :] // blk)).astype(BF16)


def _tile(n, pref):
    return pref if n % pref == 0 else n


def kernel(x, mem, norm_mix, norm_cross, norm_ffn, norm_mem, norm_final, rw_mix, rw_wr, rw_wk, rw_wv, rw_w0, rw_w1, rw_w2, rw_a0, rw_a1, rw_a2, rw_g1, rw_g2, rw_kk, rw_ka, rw_rk, rw_lnx_w, rw_lnx_b, rw_wo, sb_wqkv, sb_wo, xa_wq, xa_wkv, xa_wo, moe_wg, moe_bg, moe_we, moe_be, moe_w1, moe_w3, moe_w2):
    b, s, c = x.shape
    t = b * s
    depth = norm_mix.shape[0]
    row = lambda vec: vec.reshape(1, -1).astype(F32)
    bf = lambda w: w.astype(BF16)

    ones_quad = _block_ones(MXU_DIM, HEAD_DIM)
    ones_pair = _block_ones(LANES, HEAD_DIM)
    tri_incl = jnp.tile((jnp.arange(CHUNK)[:, None] >= jnp.arange(CHUNK)[None, :]).astype(BF16), (1, 3))
    jj = jnp.arange(SB_BLOCK)
    sb_cs = jnp.concatenate([(jj[:, None] > jj[None, :]).astype(BF16),
                             jnp.ones((SB_BLOCK, SB_BLOCK), BF16)], axis=1)

    memkv = _norm_matmul(mem.reshape(b * N_MEM, c), row(norm_mem), bf(xa_wkv), BF16,
                         _tile(b * N_MEM, 512), _tile(2 * c, 1024)).reshape(b, N_MEM, 2 * c)

    h = x
    for i in range(depth):
        j = i // 2
        if i % 2 == 0:
            r, lw, k, v, kkn, bb, g = _rwkv_proj(
                h, row(norm_mix[i]), rw_mix[j].astype(F32), bf(rw_wr[j]), bf(rw_wk[j]), bf(rw_wv[j]),
                bf(rw_w1[j]), bf(rw_w2[j]), bf(rw_a1[j]), bf(rw_a2[j]), bf(rw_g1[j]), bf(rw_g2[j]),
                row(rw_w0[j]), row(rw_a0[j]), row(rw_kk[j]), row(rw_ka[j]), ones_quad, _tile(s, 512))
            y = _rwkv_recurrence(r, lw, k, v, kkn, bb, g, row(rw_rk[j]), row(rw_lnx_w[j]),
                                 row(rw_lnx_b[j]), tri_incl, ones_pair, _tile(s, 256), 8 * LANES)
            w_mix_out = bf(rw_wo[j])
        else:
            qkv = _norm_matmul(h.reshape(t, c), row(norm_mix[i]), bf(sb_wqkv[j]), BF16,
                               _tile(t, 1024), _tile(3 * c, 1024)).reshape(b, s, 3 * c)
            y = _sb_attention(qkv, sb_cs)
            w_mix_out = bf(sb_wo[j])
        h = _cross_block(y, w_mix_out, h, row(norm_cross[i]), bf(xa_wq[i]), memkv, bf(xa_wo[i]), _tile(s, 512))

        h = _moe_layer(h.reshape(t, c), row(norm_ffn[i]), moe_wg[i], moe_bg[i], moe_we[i], moe_be[i],
                       moe_w1[i], moe_w3[i], moe_w2[i], row(norm_final), i == depth - 1).reshape(b, s, c)
    return h
```

```python
import functools

import jax
import jax.numpy as jnp
from jax import lax
from jax.experimental import pallas as pl
from jax.experimental.pallas import tpu as pltpu

F32 = jnp.float32
BF16 = jnp.bfloat16

HEAD_DIM = 64
N_MEM = 256
X_HEADS = 4
N_GROUPS = 4
EXPERTS_PER_GROUP = 8
N_EXPERTS = N_GROUPS * EXPERTS_PER_GROUP
D_EXPERT = 256
GN_EPS = 64e-5
NORM_EPS = 1e-6
SB_BLOCK = 128
CHUNK = 64

LANES = 128
MXU_DIM = 256
VMEM_LIMIT = 48 * 1024 * 1024

ROUTER_LANES = LANES


def _cparams(*sem):
    return pltpu.CompilerParams(dimension_semantics=sem, vmem_limit_bytes=VMEM_LIMIT)


def _dot(a, b):
    return jnp.dot(a, b, preferred_element_type=F32)


def _dot_nt(a, b):
    return lax.dot_general(a, b, (((1,), (1,)), ((), ())), preferred_element_type=F32)


def _split2(x):
    hi = x.astype(BF16)
    lo = (x - hi.astype(F32)).astype(BF16)
    return hi, lo


def _split3(x):
    hi = x.astype(BF16)
    r1 = x - hi.astype(F32)
    mid = r1.astype(BF16)
    lo = (r1 - mid.astype(F32)).astype(BF16)
    return hi, mid, lo


def _dot_exact_rhs(x, m_bf16, parts):
    pieces = _split3(x) if parts == 3 else _split2(x)
    if parts * x.shape[1] <= MXU_DIM:
        return _dot(jnp.concatenate(pieces, axis=1), jnp.concatenate([m_bf16] * parts, axis=0))
    acc = _dot(pieces[0], m_bf16)
    for p in pieces[1:]:
        acc = acc + _dot(p, m_bf16)
    return acc


def _dot_exact_lhs3(m3_bf16, x):
    return _dot(m3_bf16, jnp.concatenate(_split3(x), axis=0))


def _rms(x, g):
    ms = jnp.mean(x * x, axis=-1, keepdims=True)
    return x * lax.rsqrt(ms + NORM_EPS) * g


def _head_sum(x, ones_bd):
    c = x.shape[-1]
    outs = []
    for j in range(c // MXU_DIM):
        outs.append(_dot_exact_rhs(x[:, j * MXU_DIM:(j + 1) * MXU_DIM], ones_bd, 2))
    return jnp.concatenate(outs, axis=-1) if len(outs) > 1 else outs[0]


def _norm_mm_kernel(x_ref, g_ref, w_ref, o_ref, xn_ref):
    @pl.when(pl.program_id(1) == 0)
    def _():
        xn_ref[...] = _rms(x_ref[...], g_ref[...]).astype(BF16)

    o_ref[...] = _dot(xn_ref[...], w_ref[...]).astype(o_ref.dtype)


def _norm_matmul(x, g, w, out_dtype, tm, tn):
    m, c = x.shape
    n = w.shape[1]
    return pl.pallas_call(
        _norm_mm_kernel,
        grid=(m // tm, n // tn),
        in_specs=[
            pl.BlockSpec((tm, c), lambda i, j: (i, 0)),
            pl.BlockSpec((1, c), lambda i, j: (0, 0)),
            pl.BlockSpec((c, tn), lambda i, j: (0, j)),
        ],
        out_specs=pl.BlockSpec((tm, tn), lambda i, j: (i, j)),
        out_shape=jax.ShapeDtypeStruct((m, n), out_dtype),
        scratch_shapes=[pltpu.VMEM((tm, c), BF16)],
        compiler_params=_cparams("parallel", "arbitrary"),
        name="norm_matmul",
    )(x, g, w)


PROJ_SUBTILE = 256


def _softplus(x):
    return jnp.maximum(x, 0.0) + jnp.log(1.0 + jnp.exp(-jnp.abs(x)))


def _rwkv_proj_kernel(h_ref, gn_ref, mix_ref, wr_ref, wk_ref, wv_ref, w1_ref, w2_ref, a1_ref, a2_ref,
                      g1_ref, g2_ref, w0_ref, a0_ref, kk_ref, ka_ref, ones_ref,
                      r_out, lw_out, k_out, v_out, kkn_out, b_out, g_out, carry_ref):
    ts = h_ref.shape[1]
    hn = _rms(h_ref[0], gn_ref[...])

    @pl.when(pl.program_id(1) == 0)
    def _():
        carry_ref[...] = jnp.zeros_like(carry_ref)

    prev_last = carry_ref[0:1, :]
    row = lax.broadcasted_iota(jnp.int32, hn.shape, 0)
    hp = jnp.where(row == 0, prev_last, pltpu.roll(hn, 1, 0))
    carry_ref[0:1, :] = hn[ts - 1:ts, :]
    dx = hp - hn

    def project(rows):
        hn_s, dx_s = hn[rows], dx[rows]
        mixed = lambda i: (hn_s + dx_s * mix_ref[i:i + 1, :]).astype(BF16)
        r = _dot(mixed(0), wr_ref[...])
        lora_w = _dot(jnp.tanh(_dot(mixed(1), w1_ref[...])).astype(BF16), w2_ref[...])
        k = _dot(mixed(2), wk_ref[...])
        v = _dot(mixed(3), wv_ref[...])
        lora_a = _dot(_dot(mixed(4), a1_ref[...]).astype(BF16), a2_ref[...])
        g = _dot(jax.nn.sigmoid(_dot(mixed(5), g1_ref[...])).astype(BF16), g2_ref[...])
        return r, lora_w, k, v, lora_a, g

    def finish(rows, r, lora_w, k, v, lora_a, g):
        w_log = -_softplus(-(w0_ref[...] + lora_w)) - 0.5
        a = jax.nn.sigmoid(a0_ref[...] + lora_a)
        kk = k * kk_ref[...]
        nrm = jnp.sqrt(_head_sum(kk * kk, ones_ref[...]))
        kkn = kk / jnp.maximum(nrm, 1e-12)
        r_out[0, rows] = r.astype(r_out.dtype)
        lw_out[0, rows] = -jnp.exp(w_log)
        k_out[0, rows] = (k * (1.0 + (a - 1.0) * ka_ref[...])).astype(k_out.dtype)
        v_out[0, rows] = v.astype(v_out.dtype)
        kkn_out[0, rows] = kkn.astype(kkn_out.dtype)
        b_out[0, rows] = (kkn * a).astype(b_out.dtype)
        g_out[0, rows] = g.astype(g_out.dtype)

    sub = min(PROJ_SUBTILE, ts)
    pending = None
    for i in range(ts // sub + 1):
        rows = slice(i * sub, (i + 1) * sub)
        cur = (rows,) + project(rows) if i < ts // sub else None
        if pending is not None:
            finish(*pending)
        pending = cur


def _rwkv_proj(h, gn, mix, wr, wk, wv, w1, w2, a1, a2, g1, g2, w0, a0, k_k, k_a, ones_bd, ts):
    b, s, c = h.shape
    full = lambda arr: pl.BlockSpec(arr.shape, lambda i, j: (0,) * arr.ndim)
    tok = pl.BlockSpec((1, ts, c), lambda i, j: (i, j, 0))
    params = (gn, mix, wr, wk, wv, w1, w2, a1, a2, g1, g2, w0, a0, k_k, k_a, ones_bd)
    return pl.pallas_call(
        _rwkv_proj_kernel,
        grid=(b, s // ts),
        in_specs=[tok] + [full(p) for p in params],
        out_specs=[tok] * 7,
        out_shape=[jax.ShapeDtypeStruct((b, s, c), F32 if n == 1 else BF16) for n in range(7)],
        scratch_shapes=[pltpu.VMEM((8, c), F32)],
        compiler_params=_cparams("parallel", "arbitrary"),
        name="rwkv_proj",
    )(h, *params)


def _rwkv_rec_kernel(r_ref, lw_ref, k_ref, v_ref, kk_ref, b_ref, g_ref, rk_ref, lnw_ref, lnb_ref,
                     tri_ref, ones_ref, o_ref,
                     state_ref, xp_ref, vp_ref, y0_ref, bdt_ref, svt_ref, dl_ref, bon_ref):
    ts, lw_lanes = r_ref.shape[1], r_ref.shape[2]
    n_chunks = ts // CHUNK
    n_groups = lw_lanes // LANES
    L = CHUNK
    prep_chunks = 2 if n_chunks % 2 == 0 else 1

    @pl.when(pl.program_id(2) == 0)
    def _():
        state_ref[...] = jnp.zeros_like(state_ref)

    lane = lax.broadcasted_iota(jnp.int32, (L, LANES), 1)
    head0 = lane < HEAD_DIM
    rr = lax.broadcasted_iota(jnp.int32, (2 * L, 2 * L), 0)
    cc = lax.broadcasted_iota(jnp.int32, (2 * L, 2 * L), 1)
    same_head = (rr // L) == (cc // L)
    strict = same_head & ((cc % L) < (rr % L))
    incl = same_head & ((cc % L) <= (rr % L))
    eye = (rr == cc).astype(F32)
    tri_incl = tri_ref[...]
    ones_bd = ones_ref[...]

    def stack_heads(x):
        return jnp.concatenate([jnp.where(head0, x, 0.0), jnp.where(head0, 0.0, x)], axis=0)

    def unstack(x):
        return x[:L] + x[L:]

    def prepare(ci, carry):
        chains = [(ci * prep_chunks + cc, gi) for cc in range(prep_chunks) for gi in range(n_groups)]
        rows = [pl.ds(pl.multiple_of(c * L, L), L) for c, _ in chains]
        lanes = [slice(gi * LANES, (gi + 1) * LANES) for _, gi in chains]
        idxs = [c * n_groups + gi for c, gi in chains]
        each = lambda f, *cols: [f(*a) for a in zip(*cols)]
        load = lambda ref: each(lambda rw, ls: ref[0, rw, ls].astype(F32), rows, lanes)
        r, lw, k, v, kk, bb = (load(ref) for ref in (r_ref, lw_ref, k_ref, v_ref, kk_ref, b_ref))

        cum = each(lambda x: _dot_exact_lhs3(tri_incl, x), lw)
        d_inv = each(lambda c_: jnp.exp(-c_), cum)
        d_last = each(lambda c_: jnp.exp(c_[L - 1:L, :]), cum)
        xa = each(lambda c_, l_, kk_: stack_heads(-(jnp.exp(c_ - l_) * kk_)).astype(BF16), cum, lw, kk)
        xr32 = each(lambda c_, r_: stack_heads(jnp.exp(c_) * r_), cum, r)
        bt32 = each(lambda b_, d_: b_ * d_, bb, d_inv)
        kt32 = each(lambda k_, d_: k_ * d_, k, d_inv)

        def gram(xa_, xr_, bt_, kt_):
            bt16, kt16 = bt_.astype(BF16), kt_.astype(BF16)
            z_all = jnp.concatenate([bt16, bt16, kt16, kt16], axis=0)
            x_all = jnp.concatenate([xa_, xr_.astype(BF16)], axis=0)
            return _dot_nt(x_all, z_all)

        gmat = each(gram, xa, xr32, bt32, kt32)
        m_ab = each(lambda g_: jnp.where(strict, g_[:2 * L, :2 * L], 0.0), gmat)
        m_ak = each(lambda g_: jnp.where(strict, g_[:2 * L, 2 * L:], 0.0).astype(BF16), gmat)
        m_rb = each(lambda g_: jnp.where(incl, g_[2 * L:, :2 * L], 0.0).astype(BF16), gmat)
        m_rk = each(lambda g_: jnp.where(incl, g_[2 * L:, 2 * L:], 0.0).astype(BF16), gmat)

        p = m_ab
        t_inv = each(lambda m_: eye + m_, m_ab)
        for _ in range(5):
            p = each(lambda p_: _dot(p_.astype(BF16), p_.astype(BF16)), p)
            t_inv = each(lambda t_, p_: t_ + _dot(t_.astype(BF16), p_.astype(BF16)), t_inv, p)
        t16 = each(lambda t_: t_.astype(BF16), t_inv)

        v_st = each(lambda v_: stack_heads(v_).astype(BF16), v)
        mv = each(_dot, m_ak, v_st)
        tav = each(lambda t_, xa_, mv_: _dot(t_, jnp.concatenate([xa_, mv_.astype(BF16)], axis=1)),
                   t16, xa, mv)
        rby = each(lambda m_, tav_: _dot(m_, tav_.astype(BF16)), m_rb, tav)
        y0 = each(lambda rby_, m_, v_: rby_[:, LANES:] + _dot(m_, v_), rby, m_rk, v_st)
        svt = each(lambda kt_, dl_, v_: _dot(stack_heads(kt_ * dl_).T.astype(BF16), v_), kt32, d_last, v_st)
        for n, idx in enumerate(idxs):
            xp_ref[idx] = jnp.concatenate([tav[n][:, :LANES], xr32[n] + rby[n][:, :LANES]], axis=0).astype(BF16)
            vp_ref[idx] = tav[n][:, LANES:]
            y0_ref[idx] = y0[n]
            bdt_ref[idx] = stack_heads(bt32[n] * d_last[n]).T.astype(BF16)
            svt_ref[idx] = svt[n]
            dl_ref[idx] = jnp.broadcast_to(d_last[n], (LANES, LANES)).T
            bon_ref[rows[n], lanes[n]] = _dot_exact_rhs(r[n] * k[n] * rk_ref[:, lanes[n]], ones_bd, 2) * v[n]
        return carry

    def advance(ci, carry):
        rows = pl.ds(pl.multiple_of(ci * L, L), L)
        groups = list(range(n_groups))
        lanes = [slice(gi * LANES, (gi + 1) * LANES) for gi in groups]
        idxs = [ci * n_groups + gi for gi in groups]
        each = lambda f, *cols: [f(*a) for a in zip(*cols)]
        s_kv = each(lambda gi: state_ref[gi], groups)
        xs = each(lambda i_, s_: _dot(xp_ref[i_], s_.astype(BF16)), idxs, s_kv)
        u16 = each(lambda x_, i_: (x_[:2 * L] + vp_ref[i_]).astype(BF16), xs, idxs)
        upd = each(lambda i_, u_: _dot(bdt_ref[i_], u_), idxs, u16)
        for gi in groups:
            state_ref[gi] = s_kv[gi] * dl_ref[idxs[gi]] + svt_ref[idxs[gi]] + upd[gi]

        inv_n = 1.0 / HEAD_DIM
        y = each(lambda x_, i_: unstack(x_[2 * L:] + y0_ref[i_]), xs, idxs)
        yc = each(lambda y_: y_ - _dot_exact_rhs(y_, ones_bd, 2) * inv_n, y)
        var = each(lambda yc_: _dot_exact_rhs(yc_ * yc_, ones_bd, 2) * inv_n, yc)
        for gi in groups:
            ls = lanes[gi]
            yn = yc[gi] * lax.rsqrt(var[gi] + GN_EPS) * lnw_ref[:, ls] + lnb_ref[:, ls]
            o_ref[0, rows, ls] = ((yn + bon_ref[rows, ls]) * g_ref[0, rows, ls].astype(F32)).astype(o_ref.dtype)
        return carry

    lax.fori_loop(0, n_chunks // prep_chunks, prepare, 0)
    lax.fori_loop(0, n_chunks, advance, 0)


def _rwkv_recurrence(r, lw, k, v, kkn, bb, g, r_k, lnx_w, lnx_b, tri, ones_pair, ts, lane_w):
    b, s, c = r.shape
    tok = pl.BlockSpec((1, ts, lane_w), lambda i, j, t: (i, t, j))
    vec = pl.BlockSpec((1, lane_w), lambda i, j, t: (0, j))
    small = lambda arr: pl.BlockSpec(arr.shape, lambda i, j, t: (0,) * arr.ndim)
    n_groups = lane_w // LANES
    ncg = (ts // CHUNK) * n_groups
    return pl.pallas_call(
        _rwkv_rec_kernel,
        grid=(b, c // lane_w, s // ts),
        in_specs=[tok] * 7 + [vec, vec, vec, small(tri), small(ones_pair)],
        out_specs=tok,
        out_shape=jax.ShapeDtypeStruct((b, s, c), BF16),
        scratch_shapes=[pltpu.VMEM((n_groups, LANES, LANES), F32),
                        pltpu.VMEM((ncg, 4 * CHUNK, LANES), BF16),
                        pltpu.VMEM((ncg, 2 * CHUNK, LANES), F32),
                        pltpu.VMEM((ncg, 2 * CHUNK, LANES), F32),
                        pltpu.VMEM((ncg, LANES, 2 * CHUNK), BF16),
                        pltpu.VMEM((ncg, LANES, LANES), F32),
                        pltpu.VMEM((ncg, LANES, LANES), F32),
                        pltpu.VMEM((ts, lane_w), F32)],
        compiler_params=_cparams("parallel", "parallel", "arbitrary"),
        name="rwkv_recurrence",
    )(r, lw, k, v, kkn, bb, g, r_k, lnx_w, lnx_b, tri, ones_pair)


SB_QTILE = 512
SB_SUBTILE = 128
SB_STAGE_LAG = 3
SB_SKIP_BELOW = -104.0


def _sb_attn_kernel(q_ref, k_ref, v_ref, cs_ref, o_ref, qs_ref, acc_ref, run_ref, tmax_ref):
    s = q_ref.shape[1]
    tb = SB_BLOCK
    tq = min(SB_QTILE, s)
    rs = SB_SUBTILE
    sub_per_block = tb // rs
    bpt = tq // tb
    kpb = 2 if bpt % 2 == 0 else 1
    nb, nt = s // tb, s // tq
    shift = bpt.bit_length() - 1
    lane = lax.broadcasted_iota(jnp.int32, (s, LANES), 1)
    head0 = lane < HEAD_DIM
    q = q_ref[0] * (HEAD_DIM ** -0.5)
    qs_ref[0] = jnp.where(head0, q, jnp.zeros_like(q))
    qs_ref[1] = jnp.where(head0, jnp.zeros_like(q), q)
    acc_ref[...] = jnp.zeros_like(acc_ref)
    run_ref[...] = jnp.zeros_like(run_ref)
    for qb in range(nb):
        tmax_ref[qb] = 0.0
    cs = cs_ref[...]
    cs2 = jnp.concatenate([cs, cs], axis=0)

    def tile_step(ti, blocks, pieces):
        r0 = ti * tq
        subs = [(kbi, h, pl.ds(pl.multiple_of(r0 + i * rs, rs), rs), r0 + i * rs, diag, i // sub_per_block)
                for kbi, i, diag in pieces for h in range(2)]
        n = len(subs)
        lag = SB_STAGE_LAG
        z, lb, parts, sums, att, before = ({} for _ in range(6))
        top = {}
        for step in range(n + 2 + 2 * lag):
            if step < n:
                kbi, h, rows = subs[step][:3]
                z[step] = _dot_nt(qs_ref[h, rows, :], blocks[kbi][1])
            i = step - 1
            if 0 <= i < n:
                zz = z.pop(i)
                sp = jnp.log(1.0 + jnp.exp(-jnp.abs(zz)))
                lb[i] = jnp.minimum(zz, 0.0) - sp
                log_fail = lb[i] - zz
                if subs[i][4]:
                    t_idx = subs[i][3] + lax.broadcasted_iota(jnp.int32, (rs, tb), 0)
                    s_idx = blocks[subs[i][0]][0] + lax.broadcasted_iota(jnp.int32, (rs, tb), 1)
                    before[i] = s_idx < t_idx
                    log_fail = jnp.where(before[i], log_fail, 0.0)
                parts[i] = _split2(log_fail)
            i = step - 1 - lag
            if 0 <= i < n:
                hi, lo = parts.pop(i)
                sums[i] = _dot(jnp.concatenate([hi, lo], axis=1), cs2)
            i = step - 2 - lag
            if 0 <= i < n:
                kbi, h, rows, _, diag, _ = subs[i]
                sm = sums.pop(i)
                run = run_ref[h, rows, :]
                a = jnp.exp(lb.pop(i) + run + sm[:, :tb])
                if diag:
                    a = jnp.where(before.pop(i), a, 0.0)
                att[i] = a.astype(BF16)
                run = run + sm[:, tb:]
                run_ref[h, rows, :] = run
                if kbi == len(blocks) - 1:
                    qb = subs[i][5]
                    top[qb] = run if qb not in top else jnp.maximum(top[qb], run)
            i = step - 2 - 2 * lag
            if 0 <= i < n:
                kbi, h, rows = subs[i][:3]
                acc_ref[h, rows, :] += _dot(att.pop(i), blocks[kbi][2])
        for qb, t in top.items():
            tmax_ref[ti * bpt + qb] = jnp.max(t)

    def below_pieces(n_q):
        return [(kbi, i, False) for kbi in range(kpb) for i in range(n_q * sub_per_block)]

    def diagonal_pieces(u, q_last):
        jl = kpb * u + kpb - 1
        out = []
        for kbi in range(kpb):
            kl = jl - kbi
            for qb in range(kl, q_last + 1):
                for i in range(qb * sub_per_block, (qb + 1) * sub_per_block):
                    out.append((kbi, i, qb == kl))
        return out

    def last_alive(ti, first):
        n = jnp.int32(first)
        for qb in range(first, bpt):
            n = jnp.where(tmax_ref[ti * bpt + qb] >= SB_SKIP_BELOW, qb + 1, n)
        return n

    def key_blocks(jj, carry):
        j = nb - 1 - kpb * jj
        blocks = []
        for d in range(kpb):
            k0 = pl.multiple_of((j - d) * tb, tb)
            blocks.append((k0, k_ref[0, pl.ds(k0, tb), :], v_ref[0, pl.ds(k0, tb), :]))
        t0 = lax.shift_right_logical(j, shift)
        case = lax.shift_right_logical(j - t0 * bpt, kpb.bit_length() - 1)
        for u in range(bpt // kpb):
            jl = kpb * u + kpb - 1
            n_q = last_alive(t0, jl + 1)
            for q_last in range(jl, bpt):
                @pl.when((case == u) & (n_q == q_last + 1))
                def _():
                    tile_step(t0, blocks, diagonal_pieces(u, q_last))

        def below(ti, c):
            n_q = last_alive(ti, 0)
            for m in range(1, bpt + 1):
                @pl.when(n_q == m)
                def _():
                    tile_step(ti, blocks, below_pieces(m))
            return c

        lax.fori_loop(t0 + 1, nt, below, 0)
        return carry

    lax.fori_loop(0, nb // kpb, key_blocks, 0)
    o_ref[0] = jnp.where(head0, acc_ref[0], acc_ref[1]).astype(o_ref.dtype)


def _sb_attention(qkv, cs):
    b, s, c3 = qkv.shape
    c = c3 // 3
    n_pairs = c // LANES
    return pl.pallas_call(
        _sb_attn_kernel,
        grid=(b, n_pairs),
        in_specs=[
            pl.BlockSpec((1, s, LANES), lambda i, p: (i, 0, p)),
            pl.BlockSpec((1, s, LANES), lambda i, p: (i, 0, n_pairs + p)),
            pl.BlockSpec((1, s, LANES), lambda i, p: (i, 0, 2 * n_pairs + p)),
            pl.BlockSpec(cs.shape, lambda i, p: (0, 0)),
        ],
        out_specs=pl.BlockSpec((1, s, LANES), lambda i, p: (i, 0, p)),
        out_shape=jax.ShapeDtypeStruct((b, s, c), BF16),
        scratch_shapes=[pltpu.VMEM((2, s, LANES), BF16), pltpu.VMEM((2, s, LANES), F32),
                        pltpu.VMEM((2, s, SB_BLOCK), F32),
                        pltpu.SMEM((s // SB_BLOCK,), F32)],
        compiler_params=_cparams("parallel", "parallel"),
        name="sb_attention",
    )(qkv, qkv, qkv, cs)


def _cross_kernel(y_ref, wy_ref, h_ref, gn_ref, wq_ref, kv_ref, wo_ref, o_ref):
    h = h_ref[0] + _dot(y_ref[0], wy_ref[...])
    c = h.shape[-1]
    xhd = c // X_HEADS
    q = _dot(_rms(h, gn_ref[...]).astype(BF16), wq_ref[...]).astype(BF16)
    outs = []
    for hd in range(X_HEADS):
        qh = q[:, hd * xhd:(hd + 1) * xhd]
        kh = kv_ref[0, :, hd * xhd:(hd + 1) * xhd]
        vh = kv_ref[0, :, c + hd * xhd:c + (hd + 1) * xhd]
        sc = _dot_nt(qh, kh) * (xhd ** -0.5)
        sc = sc - jnp.max(sc, axis=-1, keepdims=True)
        e = jnp.exp(sc)
        p = e / jnp.sum(e, axis=-1, keepdims=True)
        outs.append(_dot(p.astype(BF16), vh).astype(BF16))
    o = jnp.concatenate(outs, axis=-1)
    o_ref[0] = h + _dot(o, wo_ref[...])


def _cross_block(y, wy, h, gn, wq, memkv, wo, tq):
    b, s, c = h.shape
    tok = pl.BlockSpec((1, tq, c), lambda i, t: (i, t, 0))
    full = lambda arr: pl.BlockSpec(arr.shape, lambda i, t: (0,) * arr.ndim)
    return pl.pallas_call(
        _cross_kernel,
        grid=(b, s // tq),
        in_specs=[tok, full(wy), tok, full(gn), full(wq),
                  pl.BlockSpec((1,) + memkv.shape[1:], lambda i, t: (i, 0, 0)), full(wo)],
        out_specs=tok,
        out_shape=jax.ShapeDtypeStruct((b, s, c), F32),
        compiler_params=_cparams("parallel", "parallel"),
        name="cross_attention",
    )(y, wy, h, gn, wq, memkv, wo)


def _lane_min_index(mask, lane):
    return jnp.min(jnp.where(mask, lane, float(ROUTER_LANES)), axis=-1, keepdims=True)


MOE_ROUTE_TILE = 512
MOE_ROW_TILE = 256


def _dot_3pass(a, b):
    ah, al = _split2(a)
    bh, bl = _split2(b)
    return _dot(ah, bl) + _dot(al, bh) + _dot(ah, bh)


def _moe_route_kernel(h_ref, gn_ref, wg_ref, bg_ref, tri_ref, dest_ref, pg_ref, cnt_ref, base_ref, *, n_tokens):
    @pl.when(pl.program_id(0) == 0)
    def _():
        base_ref[...] = jnp.zeros_like(base_ref)

    hn = _rms(h_ref[...], gn_ref[...])
    gl = _dot_3pass(hn, wg_ref[...]) + bg_ref[...]
    lane = lax.broadcasted_iota(jnp.int32, gl.shape, 1).astype(F32)
    gmax = jnp.max(gl, axis=-1, keepdims=True)
    group = _lane_min_index(gl == gmax, lane)
    p_group = 1.0 / jnp.sum(jnp.exp(gl - gmax), axis=-1, keepdims=True)
    onehot = (lane == group).astype(F32)
    earlier = _dot(tri_ref[...], onehot.astype(BF16))
    base = base_ref[0:1, :]
    rank = jnp.sum(onehot * (earlier + base), axis=-1, keepdims=True)
    dest = group * float(n_tokens) + rank
    tm = dest.shape[0]
    dest_row = jnp.broadcast_to(dest, (tm, LANES)).T[0:1, :]
    dest_ref[0] = dest_row.astype(jnp.int32)
    pg_ref[...] = jnp.broadcast_to(p_group, pg_ref.shape)
    new_base = base + earlier[tm - 1:tm, :] + onehot[tm - 1:tm, :]
    base_ref[0:1, :] = new_base
    cnt_ref[...] = jnp.broadcast_to(new_base, cnt_ref.shape)


def _moe_route(h, gn, wg128, bg128, tri, tm):
    t, c = h.shape
    full = lambda arr: pl.BlockSpec(arr.shape, lambda i: (0,) * arr.ndim)
    return pl.pallas_call(
        functools.partial(_moe_route_kernel, n_tokens=t),
        grid=(t // tm,),
        in_specs=[pl.BlockSpec((tm, c), lambda i: (i, 0)), full(gn), full(wg128), full(bg128), full(tri)],
        out_specs=[pl.BlockSpec((1, 1, tm), lambda i: (i, 0, 0)),
                   pl.BlockSpec((tm, LANES), lambda i: (i, 0)),
                   pl.BlockSpec((8, LANES), lambda i: (0, 0))],
        out_shape=[jax.ShapeDtypeStruct((t // tm, 1, tm), jnp.int32),
                   jax.ShapeDtypeStruct((t, LANES), F32),
                   jax.ShapeDtypeStruct((8, LANES), F32)],
        scratch_shapes=[pltpu.VMEM((8, LANES), F32)],
        compiler_params=_cparams("arbitrary"),
        name="moe_route",
    )(h, gn, wg128, bg128, tri)


def _moe_scatter_kernel(dest_ref, h_ref, gn_ref, init_ref, xs_ref, buf_ref, sem_ref):
    del init_ref
    i = pl.program_id(0)
    n = pl.num_programs(0)
    tm = h_ref.shape[0]
    slot = lax.rem(i, 2)

    def row_copy(s, r, d):
        return pltpu.make_async_copy(buf_ref.at[s, pl.ds(r, 1), :], xs_ref.at[pl.ds(d, 1), :], sem_ref.at[s])

    def wait_all(s):
        pltpu.make_async_copy(buf_ref.at[s], xs_ref.at[pl.ds(0, tm), :], sem_ref.at[s]).wait()

    buf_ref[slot] = _rms(h_ref[...], gn_ref[...])

    for r in range(tm):
        row_copy(slot, r, dest_ref[0, 0, r]).start(priority=r % 2)

    @pl.when(i > 0)
    def _():
        wait_all(1 - slot)

    @pl.when(i == n - 1)
    def _():
        wait_all(slot)


def _moe_scatter(h, gn, dest3, xs_init, tm):
    t, c = h.shape
    return pl.pallas_call(
        _moe_scatter_kernel,
        grid=(t // tm,),
        in_specs=[pl.BlockSpec((1, 1, tm), lambda i: (i, 0, 0), memory_space=pltpu.SMEM),
                  pl.BlockSpec((tm, c), lambda i: (i, 0)),
                  pl.BlockSpec(gn.shape, lambda i: (0, 0)),
                  pl.BlockSpec(memory_space=pl.ANY)],
        out_specs=pl.BlockSpec(memory_space=pl.ANY),
        out_shape=jax.ShapeDtypeStruct(xs_init.shape, F32),
        scratch_shapes=[pltpu.VMEM((2, tm, c), F32), pltpu.SemaphoreType.DMA((2,))],
        input_output_aliases={3: 0},
        compiler_params=_cparams("arbitrary"),
        name="moe_scatter",
    )(dest3, h, gn, xs_init)


def _moe_group_ffn_kernel(tg_ref, x_ref, we_ref, be_ref, w1_ref, w3_ref, w2_ref, o_ref):
    del tg_ref
    x = x_ref[...]
    xh, xl = _split2(x)
    el = be_ref[0] + _dot(xh, we_ref[0, 1]) + _dot(xl, we_ref[0, 0]) + _dot(xh, we_ref[0, 0])
    lane = lax.broadcasted_iota(jnp.int32, el.shape, 1).astype(F32)
    top1 = jnp.max(el, axis=-1, keepdims=True)
    idx1 = _lane_min_index(el == top1, lane)
    el2 = jnp.where(lane == idx1, -jnp.inf, el)
    top2 = jnp.max(el2, axis=-1, keepdims=True)
    idx2 = _lane_min_index(el2 == top2, lane)
    e2 = jnp.exp(top2 - top1)
    s1 = 1.0 / (1.0 + e2)
    s2 = e2 / (1.0 + e2)
    hids = []
    for e in range(EXPERTS_PER_GROUP):
        ge = jnp.where(idx1 == float(e), s1, jnp.where(idx2 == float(e), s2, 0.0))
        h1 = _dot(xh, w1_ref[0, e])
        h3 = _dot(xh, w3_ref[0, e])
        hids.append(((h1 * jax.nn.sigmoid(h1)) * h3 * ge).astype(BF16))
    o_ref[...] = _dot(jnp.concatenate(hids, axis=1), w2_ref[0])


def _moe_group_ffn(tile_group, xs, we2, be, w1, w3, w2, tm):
    rows, c = xs.shape
    g, e, _, f = w1.shape
    grid_spec = pltpu.PrefetchScalarGridSpec(
        num_scalar_prefetch=1,
        grid=(rows // tm,),
        in_specs=[pl.BlockSpec((tm, c), lambda i, tg: (i, 0)),
                  pl.BlockSpec((1, 2, c, LANES), lambda i, tg: (tg[i], 0, 0, 0)),
                  pl.BlockSpec((1, 1, LANES), lambda i, tg: (tg[i], 0, 0)),
                  pl.BlockSpec((1, e, c, f), lambda i, tg: (tg[i], 0, 0, 0)),
                  pl.BlockSpec((1, e, c, f), lambda i, tg: (tg[i], 0, 0, 0)),
                  pl.BlockSpec((1, e * f, c), lambda i, tg: (tg[i], 0, 0))],
        out_specs=pl.BlockSpec((tm, c), lambda i, tg: (i, 0)),
    )
    return pl.pallas_call(
        _moe_group_ffn_kernel,
        grid_spec=grid_spec,
        out_shape=jax.ShapeDtypeStruct((rows, c), F32),
        compiler_params=_cparams("arbitrary"),
        name="moe_group_ffn",
    )(tile_group, xs, we2, be, w1, w3, w2)


def _moe_gather_kernel(dest_ref, nxt_ref, h_ref, pg_ref, gf_ref, ys_ref, o_ref, buf_ref, sem_ref, *, final_norm):
    i = pl.program_id(0)
    n = pl.num_programs(0)
    tm = h_ref.shape[0]
    slot = lax.rem(i, 2)

    def row_copy(s, r, d):
        return pltpu.make_async_copy(ys_ref.at[pl.ds(d, 1), :], buf_ref.at[s, pl.ds(r, 1), :], sem_ref.at[s])

    def start_all(s, idx_ref):
        for r in range(tm):
            row_copy(s, r, idx_ref[0, 0, r]).start(priority=r % 2)

    @pl.when(i == 0)
    def _():
        start_all(0, dest_ref)

    @pl.when(i + 1 < n)
    def _():
        start_all(1 - slot, nxt_ref)

    pltpu.make_async_copy(ys_ref.at[pl.ds(0, tm), :], buf_ref.at[slot], sem_ref.at[slot]).wait()

    y = buf_ref[slot]
    pg = pg_ref[...]
    c = y.shape[1]
    out = h_ref[...] + y * jnp.concatenate([pg] * (c // LANES), axis=1)
    if final_norm:
        out = _rms(out, gf_ref[...])
    o_ref[...] = out


def _moe_gather(h, pg, dest3, ys, g_final, final_norm, tm):
    t, c = h.shape
    n = t // tm
    return pl.pallas_call(
        functools.partial(_moe_gather_kernel, final_norm=final_norm),
        grid=(n,),
        in_specs=[pl.BlockSpec((1, 1, tm), lambda i: (i, 0, 0), memory_space=pltpu.SMEM),
                  pl.BlockSpec((1, 1, tm), lambda i: (jnp.minimum(i + 1, n - 1), 0, 0), memory_space=pltpu.SMEM),
                  pl.BlockSpec((tm, c), lambda i: (i, 0)),
                  pl.BlockSpec((tm, LANES), lambda i: (i, 0)),
                  pl.BlockSpec(g_final.shape, lambda i: (0, 0)),
                  pl.BlockSpec(memory_space=pl.ANY)],
        out_specs=pl.BlockSpec((tm, c), lambda i: (i, 0)),
        out_shape=jax.ShapeDtypeStruct((t, c), F32),
        scratch_shapes=[pltpu.VMEM((2, tm, c), F32), pltpu.SemaphoreType.DMA((2,))],
        compiler_params=_cparams("arbitrary"),
        name="moe_gather",
    )(dest3, dest3, h, pg, g_final, ys)


def _moe_layer(h, gn, wg, bg, we, be, w1, w3, w2, g_final, final_norm):
    t, c = h.shape
    g, e = N_GROUPS, EXPERTS_PER_GROUP
    f = w1.shape[-1]
    tm_r = MOE_ROUTE_TILE if t % MOE_ROUTE_TILE == 0 else t
    tm = MOE_ROW_TILE if t % MOE_ROW_TILE == 0 else t
    neg = -1e30
    wg128 = jnp.zeros((c, LANES), F32).at[:, :g].set(wg)
    bg128 = jnp.full((1, LANES), neg, F32).at[0, :g].set(bg)
    jj = jnp.arange(tm_r)
    tri = (jj[:, None] > jj[None, :]).astype(BF16)
    dest_local, pg, counts = _moe_route(h, gn, wg128, bg128, tri, tm_r)

    cnt = counts[0, :g].astype(jnp.int32)
    tiles = (cnt + tm - 1) // tm
    tile_end = jnp.cumsum(tiles)
    offs = (tile_end - tiles) * tm
    n_tiles = t // tm + g
    dl = dest_local.reshape(t)
    grp = dl // t
    dest = (dl - grp * t + offs[grp]).reshape(t // tm, 1, tm)
    tile_group = jnp.minimum(jnp.sum(jnp.arange(n_tiles)[:, None] >= tile_end[None, :], axis=1), g - 1).astype(jnp.int32)

    xs = _moe_scatter(h, gn, dest, jnp.zeros((n_tiles * tm, c), F32), tm)
    we_g = jnp.zeros((g, c, LANES), F32).at[:, :, :e].set(we.reshape(c, g, e).transpose(1, 0, 2))
    we2 = jnp.stack(_split2(we_g), axis=1)
    be_g = jnp.full((g, 1, LANES), neg, F32).at[:, 0, :e].set(be.reshape(g, e))
    ys = _moe_group_ffn(tile_group, xs, we2, be_g, w1.astype(BF16), w3.astype(BF16),
                        w2.astype(BF16).reshape(g, e * f, c), tm)
    return _moe_gather(h, pg, dest, ys, g_final, final_norm, tm)


def _block_ones(n, blk):
    i = jnp.arange(n)
    return ((i[:, None] // blk) == (i[None, :] // blk)).astype(BF16)


def _tile(n, pref):
    return pref if n % pref == 0 else n


def kernel(x, mem, norm_mix, norm_cross, norm_ffn, norm_mem, norm_final, rw_mix, rw_wr, rw_wk, rw_wv, rw_w0, rw_w1, rw_w2, rw_a0, rw_a1, rw_a2, rw_g1, rw_g2, rw_kk, rw_ka, rw_rk, rw_lnx_w, rw_lnx_b, rw_wo, sb_wqkv, sb_wo, xa_wq, xa_wkv, xa_wo, moe_wg, moe_bg, moe_we, moe_be, moe_w1, moe_w3, moe_w2):
    b, s, c = x.shape
    t = b * s
    depth = norm_mix.shape[0]
    row = lambda vec: vec.reshape(1, -1).astype(F32)
    bf = lambda w: w.astype(BF16)

    ones_quad = _block_ones(MXU_DIM, HEAD_DIM)
    ones_pair = _block_ones(LANES, HEAD_DIM)
    tri_incl = jnp.tile((jnp.arange(CHUNK)[:, None] >= jnp.arange(CHUNK)[None, :]).astype(BF16), (1, 3))
    jj = jnp.arange(SB_BLOCK)
    sb_cs = jnp.concatenate([(jj[:, None] > jj[None, :]).astype(BF16),
                             jnp.ones((SB_BLOCK, SB_BLOCK), BF16)], axis=1)

    memkv = _norm_matmul(mem.reshape(b * N_MEM, c), row(norm_mem), bf(xa_wkv), BF16,
                         _tile(b * N_MEM, 512), _tile(2 * c, 1024)).reshape(b, N_MEM, 2 * c)

    h = x
    for i in range(depth):
        j = i // 2
        if i % 2 == 0:
            r, lw, k, v, kkn, bb, g = _rwkv_proj(
                h, row(norm_mix[i]), rw_mix[j].astype(F32), bf(rw_wr[j]), bf(rw_wk[j]), bf(rw_wv[j]),
                bf(rw_w1[j]), bf(rw_w2[j]), bf(rw_a1[j]), bf(rw_a2[j]), bf(rw_g1[j]), bf(rw_g2[j]),
                row(rw_w0[j]), row(rw_a0[j]), row(rw_kk[j]), row(rw_ka[j]), ones_quad, _tile(s, 512))
            y = _rwkv_recurrence(r, lw, k, v, kkn, bb, g, row(rw_rk[j]), row(rw_lnx_w[j]),
                                 row(rw_lnx_b[j]), tri_incl, ones_pair, _tile(s, 256), 8 * LANES)
            w_mix_out = bf(rw_wo[j])
        else:
            qkv = _norm_matmul(h.reshape(t, c), row(norm_mix[i]), bf(sb_wqkv[j]), BF16,
                               _tile(t, 1024), _tile(3 * c, 1024)).reshape(b, s, 3 * c)
            y = _sb_attention(qkv, sb_cs)
            w_mix_out = bf(sb_wo[j])
        h = _cross_block(y, w_mix_out, h, row(norm_cross[i]), bf(xa_wq[i]), memkv, bf(xa_wo[i]), _tile(s, 512))

        h = _moe_layer(h.reshape(t, c), row(norm_ffn[i]), moe_wg[i], moe_bg[i], moe_we[i], moe_be[i],
                       moe_w1[i], moe_w3[i], moe_w2[i], row(norm_final), i == depth - 1).reshape(b, s, c)
    return h
```

```python
import functools

import jax
import jax.numpy as jnp
from jax import lax
from jax.experimental import pallas as pl
from jax.experimental.pallas import tpu as pltpu

F32 = jnp.float32
BF16 = jnp.bfloat16

HEAD_DIM = 64
N_MEM = 256
X_HEADS = 4
N_GROUPS = 4
EXPERTS_PER_GROUP = 8
N_EXPERTS = N_GROUPS * EXPERTS_PER_GROUP
D_EXPERT = 256
GN_EPS = 64e-5
NORM_EPS = 1e-6
SB_BLOCK = 128
CHUNK = 64

LANES = 128
MXU_DIM = 256
VMEM_LIMIT = 48 * 1024 * 1024

ROUTER_LANES = LANES


def _cparams(*sem):
    return pltpu.CompilerParams(dimension_semantics=sem, vmem_limit_bytes=VMEM_LIMIT)


def _dot(a, b):
    return jnp.dot(a, b, preferred_element_type=F32)


def _dot_nt(a, b):
    return lax.dot_general(a, b, (((1,), (1,)), ((), ())), preferred_element_type=F32)


def _split2(x):
    hi = x.astype(BF16)
    lo = (x - hi.astype(F32)).astype(BF16)
    return hi, lo


def _split3(x):
    hi = x.astype(BF16)
    r1 = x - hi.astype(F32)
    mid = r1.astype(BF16)
    lo = (r1 - mid.astype(F32)).astype(BF16)
    return hi, mid, lo


def _dot_exact_rhs(x, m_bf16, parts):
    pieces = _split3(x) if parts == 3 else _split2(x)
    if parts * x.shape[1] <= MXU_DIM:
        return _dot(jnp.concatenate(pieces, axis=1), jnp.concatenate([m_bf16] * parts, axis=0))
    acc = _dot(pieces[0], m_bf16)
    for p in pieces[1:]:
        acc = acc + _dot(p, m_bf16)
    return acc


def _dot_exact_lhs3(m3_bf16, x):
    return _dot(m3_bf16, jnp.concatenate(_split3(x), axis=0))


def _rms(x, g):
    ms = jnp.mean(x * x, axis=-1, keepdims=True)
    return x * lax.rsqrt(ms + NORM_EPS) * g


def _head_sum(x, ones_bd):
    c = x.shape[-1]
    outs = []
    for j in range(c // MXU_DIM):
        outs.append(_dot_exact_rhs(x[:, j * MXU_DIM:(j + 1) * MXU_DIM], ones_bd, 2))
    return jnp.concatenate(outs, axis=-1) if len(outs) > 1 else outs[0]


def _norm_mm_kernel(x_ref, g_ref, w_ref, o_ref, xn_ref):
    @pl.when(pl.program_id(1) == 0)
    def _():
        xn_ref[...] = _rms(x_ref[...], g_ref[...]).astype(BF16)

    o_ref[...] = _dot(xn_ref[...], w_ref[...]).astype(o_ref.dtype)


def _norm_matmul(x, g, w, out_dtype, tm, tn):
    m, c = x.shape
    n = w.shape[1]
    return pl.pallas_call(
        _norm_mm_kernel,
        grid=(m // tm, n // tn),
        in_specs=[
            pl.BlockSpec((tm, c), lambda i, j: (i, 0)),
            pl.BlockSpec((1, c), lambda i, j: (0, 0)),
            pl.BlockSpec((c, tn), lambda i, j: (0, j)),
        ],
        out_specs=pl.BlockSpec((tm, tn), lambda i, j: (i, j)),
        out_shape=jax.ShapeDtypeStruct((m, n), out_dtype),
        scratch_shapes=[pltpu.VMEM((tm, c), BF16)],
        compiler_params=_cparams("parallel", "arbitrary"),
        name="norm_matmul",
    )(x, g, w)


PROJ_SUBTILE = 256


def _softplus(x):
    return jnp.maximum(x, 0.0) + jnp.log(1.0 + jnp.exp(-jnp.abs(x)))


def _rwkv_proj_kernel(h_ref, gn_ref, mix_ref, wr_ref, wk_ref, wv_ref, w1_ref, w2_ref, a1_ref, a2_ref,
                      g1_ref, g2_ref, w0_ref, a0_ref, kk_ref, ka_ref, ones_ref,
                      r_out, lw_out, k_out, v_out, kkn_out, b_out, g_out, carry_ref):
    ts = h_ref.shape[1]
    hn = _rms(h_ref[0], gn_ref[...])

    @pl.when(pl.program_id(1) == 0)
    def _():
        carry_ref[...] = jnp.zeros_like(carry_ref)

    prev_last = carry_ref[0:1, :]
    row = lax.broadcasted_iota(jnp.int32, hn.shape, 0)
    hp = jnp.where(row == 0, prev_last, pltpu.roll(hn, 1, 0))
    carry_ref[0:1, :] = hn[ts - 1:ts, :]
    dx = hp - hn

    def project(rows):
        hn_s, dx_s = hn[rows], dx[rows]
        mixed = lambda i: (hn_s + dx_s * mix_ref[i:i + 1, :]).astype(BF16)
        r = _dot(mixed(0), wr_ref[...])
        lora_w = _dot(jnp.tanh(_dot(mixed(1), w1_ref[...])).astype(BF16), w2_ref[...])
        k = _dot(mixed(2), wk_ref[...])
        v = _dot(mixed(3), wv_ref[...])
        lora_a = _dot(_dot(mixed(4), a1_ref[...]).astype(BF16), a2_ref[...])
        g = _dot(jax.nn.sigmoid(_dot(mixed(5), g1_ref[...])).astype(BF16), g2_ref[...])
        return r, lora_w, k, v, lora_a, g

    def finish(rows, r, lora_w, k, v, lora_a, g):
        w_log = -_softplus(-(w0_ref[...] + lora_w)) - 0.5
        a = jax.nn.sigmoid(a0_ref[...] + lora_a)
        kk = k * kk_ref[...]
        nrm = jnp.sqrt(_head_sum(kk * kk, ones_ref[...]))
        kkn = kk / jnp.maximum(nrm, 1e-12)
        r_out[0, rows] = r.astype(r_out.dtype)
        lw_out[0, rows] = -jnp.exp(w_log)
        k_out[0, rows] = (k * (1.0 + (a - 1.0) * ka_ref[...])).astype(k_out.dtype)
        v_out[0, rows] = v.astype(v_out.dtype)
        kkn_out[0, rows] = kkn.astype(kkn_out.dtype)
        b_out[0, rows] = (kkn * a).astype(b_out.dtype)
        g_out[0, rows] = g.astype(g_out.dtype)

    sub = min(PROJ_SUBTILE, ts)
    pending = None
    for i in range(ts // sub + 1):
        rows = slice(i * sub, (i + 1) * sub)
        cur = (rows,) + project(rows) if i < ts // sub else None
        if pending is not None:
            finish(*pending)
        pending = cur


def _rwkv_proj(h, gn, mix, wr, wk, wv, w1, w2, a1, a2, g1, g2, w0, a0, k_k, k_a, ones_bd, ts):
    b, s, c = h.shape
    full = lambda arr: pl.BlockSpec(arr.shape, lambda i, j: (0,) * arr.ndim)
    tok = pl.BlockSpec((1, ts, c), lambda i, j: (i, j, 0))
    params = (gn, mix, wr, wk, wv, w1, w2, a1, a2, g1, g2, w0, a0, k_k, k_a, ones_bd)
    return pl.pallas_call(
        _rwkv_proj_kernel,
        grid=(b, s // ts),
        in_specs=[tok] + [full(p) for p in params],
        out_specs=[tok] * 7,
        out_shape=[jax.ShapeDtypeStruct((b, s, c), F32 if n == 1 else BF16) for n in range(7)],
        scratch_shapes=[pltpu.VMEM((8, c), F32)],
        compiler_params=_cparams("parallel", "arbitrary"),
        name="rwkv_proj",
    )(h, *params)


def _rwkv_rec_kernel(r_ref, lw_ref, k_ref, v_ref, kk_ref, b_ref, g_ref, rk_ref, lnw_ref, lnb_ref,
                     tri_ref, ones_ref, o_ref,
                     state_ref, xp_ref, vp_ref, y0_ref, bdt_ref, svt_ref, dl_ref, bon_ref):
    ts, lw_lanes = r_ref.shape[1], r_ref.shape[2]
    n_chunks = ts // CHUNK
    n_groups = lw_lanes // LANES
    L = CHUNK
    prep_chunks = 2 if n_chunks % 2 == 0 else 1

    @pl.when(pl.program_id(2) == 0)
    def _():
        state_ref[...] = jnp.zeros_like(state_ref)

    lane = lax.broadcasted_iota(jnp.int32, (L, LANES), 1)
    head0 = lane < HEAD_DIM
    rr = lax.broadcasted_iota(jnp.int32, (2 * L, 2 * L), 0)
    cc = lax.broadcasted_iota(jnp.int32, (2 * L, 2 * L), 1)
    same_head = (rr // L) == (cc // L)
    strict = same_head & ((cc % L) < (rr % L))
    incl = same_head & ((cc % L) <= (rr % L))
    eye = (rr == cc).astype(F32)
    tri_incl = tri_ref[...]
    ones_bd = ones_ref[...]

    def stack_heads(x):
        return jnp.concatenate([jnp.where(head0, x, 0.0), jnp.where(head0, 0.0, x)], axis=0)

    def unstack(x):
        return x[:L] + x[L:]

    def prepare(ci, carry):
        chains = [(ci * prep_chunks + cc, gi) for cc in range(prep_chunks) for gi in range(n_groups)]
        rows = [pl.ds(pl.multiple_of(c * L, L), L) for c, _ in chains]
        lanes = [slice(gi * LANES, (gi + 1) * LANES) for _, gi in chains]
        idxs = [c * n_groups + gi for c, gi in chains]
        each = lambda f, *cols: [f(*a) for a in zip(*cols)]
        load = lambda ref: each(lambda rw, ls: ref[0, rw, ls].astype(F32), rows, lanes)
        r, lw, k, v, kk, bb = (load(ref) for ref in (r_ref, lw_ref, k_ref, v_ref, kk_ref, b_ref))

        cum = each(lambda x: _dot_exact_lhs3(tri_incl, x), lw)
        d_inv = each(lambda c_: jnp.exp(-c_), cum)
        d_last = each(lambda c_: jnp.exp(c_[L - 1:L, :]), cum)
        xa = each(lambda c_, l_, kk_: stack_heads(-(jnp.exp(c_ - l_) * kk_)).astype(BF16), cum, lw, kk)
        xr32 = each(lambda c_, r_: stack_heads(jnp.exp(c_) * r_), cum, r)
        bt32 = each(lambda b_, d_: b_ * d_, bb, d_inv)
        kt32 = each(lambda k_, d_: k_ * d_, k, d_inv)

        def gram(xa_, xr_, bt_, kt_):
            bt16, kt16 = bt_.astype(BF16), kt_.astype(BF16)
            z_all = jnp.concatenate([bt16, bt16, kt16, kt16], axis=0)
            x_all = jnp.concatenate([xa_, xr_.astype(BF16)], axis=0)
            return _dot_nt(x_all, z_all)

        gmat = each(gram, xa, xr32, bt32, kt32)
        m_ab = each(lambda g_: jnp.where(strict, g_[:2 * L, :2 * L], 0.0), gmat)
        m_ak = each(lambda g_: jnp.where(strict, g_[:2 * L, 2 * L:], 0.0).astype(BF16), gmat)
        m_rb = each(lambda g_: jnp.where(incl, g_[2 * L:, :2 * L], 0.0).astype(BF16), gmat)
        m_rk = each(lambda g_: jnp.where(incl, g_[2 * L:, 2 * L:], 0.0).astype(BF16), gmat)

        p = m_ab
        t_inv = each(lambda m_: eye + m_, m_ab)
        for _ in range(5):
            p = each(lambda p_: _dot(p_.astype(BF16), p_.astype(BF16)), p)
            t_inv = each(lambda t_, p_: t_ + _dot(t_.astype(BF16), p_.astype(BF16)), t_inv, p)
        t16 = each(lambda t_: t_.astype(BF16), t_inv)

        v_st = each(lambda v_: stack_heads(v_).astype(BF16), v)
        mv = each(_dot, m_ak, v_st)
        tav = each(lambda t_, xa_, mv_: _dot(t_, jnp.concatenate([xa_, mv_.astype(BF16)], axis=1)),
                   t16, xa, mv)
        rby = each(lambda m_, tav_: _dot(m_, tav_.astype(BF16)), m_rb, tav)
        y0 = each(lambda rby_, m_, v_: rby_[:, LANES:] + _dot(m_, v_), rby, m_rk, v_st)
        svt = each(lambda kt_, dl_, v_: _dot(stack_heads(kt_ * dl_).T.astype(BF16), v_), kt32, d_last, v_st)
        for n, idx in enumerate(idxs):
            xp_ref[idx] = jnp.concatenate([tav[n][:, :LANES], xr32[n] + rby[n][:, :LANES]], axis=0).astype(BF16)
            vp_ref[idx] = tav[n][:, LANES:]
            y0_ref[idx] = y0[n]
            bdt_ref[idx] = stack_heads(bt32[n] * d_last[n]).T.astype(BF16)
            svt_ref[idx] = svt[n]
            dl_ref[idx] = jnp.broadcast_to(d_last[n], (LANES, LANES)).T
            bon_ref[rows[n], lanes[n]] = _dot_exact_rhs(r[n] * k[n] * rk_ref[:, lanes[n]], ones_bd, 2) * v[n]
        return carry

    def advance(ci, carry):
        rows = pl.ds(pl.multiple_of(ci * L, L), L)
        groups = list(range(n_groups))
        lanes = [slice(gi * LANES, (gi + 1) * LANES) for gi in groups]
        idxs = [ci * n_groups + gi for gi in groups]
        each = lambda f, *cols: [f(*a) for a in zip(*cols)]
        s_kv = each(lambda gi: state_ref[gi], groups)
        xs = each(lambda i_, s_: _dot(xp_ref[i_], s_.astype(BF16)), idxs, s_kv)
        u16 = each(lambda x_, i_: (x_[:2 * L] + vp_ref[i_]).astype(BF16), xs, idxs)
        upd = each(lambda i_, u_: _dot(bdt_ref[i_], u_), idxs, u16)
        for gi in groups:
            state_ref[gi] = s_kv[gi] * dl_ref[idxs[gi]] + svt_ref[idxs[gi]] + upd[gi]

        inv_n = 1.0 / HEAD_DIM
        y = each(lambda x_, i_: unstack(x_[2 * L:] + y0_ref[i_]), xs, idxs)
        yc = each(lambda y_: y_ - _dot_exact_rhs(y_, ones_bd, 2) * inv_n, y)
        var = each(lambda yc_: _dot_exact_rhs(yc_ * yc_, ones_bd, 2) * inv_n, yc)
        for gi in groups:
            ls = lanes[gi]
            yn = yc[gi] * lax.rsqrt(var[gi] + GN_EPS) * lnw_ref[:, ls] + lnb_ref[:, ls]
            o_ref[0, rows, ls] = ((yn + bon_ref[rows, ls]) * g_ref[0, rows, ls].astype(F32)).astype(o_ref.dtype)
        return carry

    lax.fori_loop(0, n_chunks // prep_chunks, prepare, 0)
    lax.fori_loop(0, n_chunks, advance, 0)


def _rwkv_recurrence(r, lw, k, v, kkn, bb, g, r_k, lnx_w, lnx_b, tri, ones_pair, ts, lane_w):
    b, s, c = r.shape
    tok = pl.BlockSpec((1, ts, lane_w), lambda i, j, t: (i, t, j))
    vec = pl.BlockSpec((1, lane_w), lambda i, j, t: (0, j))
    small = lambda arr: pl.BlockSpec(arr.shape, lambda i, j, t: (0,) * arr.ndim)
    n_groups = lane_w // LANES
    ncg = (ts // CHUNK) * n_groups
    return pl.pallas_call(
        _rwkv_rec_kernel,
        grid=(b, c // lane_w, s // ts),
        in_specs=[tok] * 7 + [vec, vec, vec, small(tri), small(ones_pair)],
        out_specs=tok,
        out_shape=jax.ShapeDtypeStruct((b, s, c), BF16),
        scratch_shapes=[pltpu.VMEM((n_groups, LANES, LANES), F32),
                        pltpu.VMEM((ncg, 4 * CHUNK, LANES), BF16),
                        pltpu.VMEM((ncg, 2 * CHUNK, LANES), F32),
                        pltpu.VMEM((ncg, 2 * CHUNK, LANES), F32),
                        pltpu.VMEM((ncg, LANES, 2 * CHUNK), BF16),
                        pltpu.VMEM((ncg, LANES, LANES), F32),
                        pltpu.VMEM((ncg, LANES, LANES), F32),
                        pltpu.VMEM((ts, lane_w), F32)],
        compiler_params=_cparams("parallel", "parallel", "arbitrary"),
        name="rwkv_recurrence",
    )(r, lw, k, v, kkn, bb, g, r_k, lnx_w, lnx_b, tri, ones_pair)


SB_QTILE = 512
SB_SUBTILE = 128
SB_STAGE_LAG = 3
SB_SKIP_BELOW = -104.0
SB_PAIRS = 4


def _sb_attn_kernel(q_ref, k_ref, v_ref, cs_ref, o_ref, qs_ref, acc_ref, run_ref, tmax_ref):
    s = q_ref.shape[1]
    tb = SB_BLOCK
    tq = min(SB_QTILE, s)
    rs = SB_SUBTILE
    sub_per_block = tb // rs
    bpt = tq // tb
    kpb = 2 if bpt % 2 == 0 else 1
    nb, nt = s // tb, s // tq
    shift = bpt.bit_length() - 1
    lane = lax.broadcasted_iota(jnp.int32, (s, LANES), 1)
    head0 = lane < HEAD_DIM
    n_heads = qs_ref.shape[0]
    for pp in range(n_heads // 2):
        q = q_ref[0, :, pp * LANES:(pp + 1) * LANES] * (HEAD_DIM ** -0.5)
        qs_ref[2 * pp] = jnp.where(head0, q, jnp.zeros_like(q))
        qs_ref[2 * pp + 1] = jnp.where(head0, jnp.zeros_like(q), q)
    acc_ref[...] = jnp.zeros_like(acc_ref)
    run_ref[...] = jnp.zeros_like(run_ref)
    for qb in range(nb):
        tmax_ref[qb] = 0.0
    cs = cs_ref[...]
    cs2 = jnp.concatenate([cs, cs], axis=0)

    def tile_step(ti, blocks, pieces):
        r0 = ti * tq
        subs = [(kbi, h, pl.ds(pl.multiple_of(r0 + i * rs, rs), rs), r0 + i * rs, diag, i // sub_per_block)
                for kbi, i, diag in pieces for h in range(n_heads)]
        n = len(subs)
        lag = SB_STAGE_LAG
        z, lb, parts, sums, att, before = ({} for _ in range(6))
        top = {}
        for step in range(n + 2 + 2 * lag):
            if step < n:
                kbi, h, rows = subs[step][:3]
                z[step] = _dot_nt(qs_ref[h, rows, :], blocks[kbi][1][h // 2])
            i = step - 1
            if 0 <= i < n:
                zz = z.pop(i)
                sp = jnp.log(1.0 + jnp.exp(-jnp.abs(zz)))
                lb[i] = jnp.minimum(zz, 0.0) - sp
                log_fail = lb[i] - zz
                if subs[i][4]:
                    t_idx = subs[i][3] + lax.broadcasted_iota(jnp.int32, (rs, tb), 0)
                    s_idx = blocks[subs[i][0]][0] + lax.broadcasted_iota(jnp.int32, (rs, tb), 1)
                    before[i] = s_idx < t_idx
                    log_fail = jnp.where(before[i], log_fail, 0.0)
                parts[i] = _split2(log_fail)
            i = step - 1 - lag
            if 0 <= i < n:
                hi, lo = parts.pop(i)
                sums[i] = _dot(jnp.concatenate([hi, lo], axis=1), cs2)
            i = step - 2 - lag
            if 0 <= i < n:
                kbi, h, rows, _, diag, _ = subs[i]
                sm = sums.pop(i)
                run = run_ref[h, rows, :]
                a = jnp.exp(lb.pop(i) + run + sm[:, :tb])
                if diag:
                    a = jnp.where(before.pop(i), a, 0.0)
                att[i] = a.astype(BF16)
                run = run + sm[:, tb:]
                run_ref[h, rows, :] = run
                if kbi == len(blocks) - 1:
                    qb = subs[i][5]
                    top[qb] = run if qb not in top else jnp.maximum(top[qb], run)
            i = step - 2 - 2 * lag
            if 0 <= i < n:
                kbi, h, rows = subs[i][:3]
                acc_ref[h, rows, :] += _dot(att.pop(i), blocks[kbi][2][h // 2])
        for qb, t in top.items():
            tmax_ref[ti * bpt + qb] = jnp.max(t)

    def below_pieces(n_q):
        return [(kbi, i, False) for kbi in range(kpb) for i in range(n_q * sub_per_block)]

    def diagonal_pieces(u, q_last):
        jl = kpb * u + kpb - 1
        out = []
        for kbi in range(kpb):
            kl = jl - kbi
            for qb in range(kl, q_last + 1):
                for i in range(qb * sub_per_block, (qb + 1) * sub_per_block):
                    out.append((kbi, i, qb == kl))
        return out

    def last_alive(ti, first):
        n = jnp.int32(first)
        for qb in range(first, bpt):
            n = jnp.where(tmax_ref[ti * bpt + qb] >= SB_SKIP_BELOW, qb + 1, n)
        return n

    def key_blocks(jj, carry):
        j = nb - 1 - kpb * jj
        blocks = []
        for d in range(kpb):
            k0 = pl.multiple_of((j - d) * tb, tb)
            lanes = [slice(pp * LANES, (pp + 1) * LANES) for pp in range(n_heads // 2)]
            blocks.append((k0, [k_ref[0, pl.ds(k0, tb), ls] for ls in lanes],
                           [v_ref[0, pl.ds(k0, tb), ls] for ls in lanes]))
        t0 = lax.shift_right_logical(j, shift)
        case = lax.shift_right_logical(j - t0 * bpt, kpb.bit_length() - 1)
        for u in range(bpt // kpb):
            jl = kpb * u + kpb - 1
            n_q = last_alive(t0, jl + 1)
            for q_last in range(jl, bpt):
                @pl.when((case == u) & (n_q == q_last + 1))
                def _():
                    tile_step(t0, blocks, diagonal_pieces(u, q_last))

        def below(ti, c):
            n_q = last_alive(ti, 0)
            for m in range(1, bpt + 1):
                @pl.when(n_q == m)
                def _():
                    tile_step(ti, blocks, below_pieces(m))
            return c

        lax.fori_loop(t0 + 1, nt, below, 0)
        return carry

    lax.fori_loop(0, nb // kpb, key_blocks, 0)
    for pp in range(n_heads // 2):
        o_ref[0, :, pp * LANES:(pp + 1) * LANES] = jnp.where(head0, acc_ref[2 * pp], acc_ref[2 * pp + 1]).astype(o_ref.dtype)


def _sb_attention(qkv, cs):
    b, s, c3 = qkv.shape
    c = c3 // 3
    w = LANES * SB_PAIRS
    n_blk = c // w
    n_heads = 2 * SB_PAIRS
    return pl.pallas_call(
        _sb_attn_kernel,
        grid=(b, n_blk),
        in_specs=[
            pl.BlockSpec((1, s, w), lambda i, p: (i, 0, p)),
            pl.BlockSpec((1, s, w), lambda i, p: (i, 0, n_blk + p)),
            pl.BlockSpec((1, s, w), lambda i, p: (i, 0, 2 * n_blk + p)),
            pl.BlockSpec(cs.shape, lambda i, p: (0, 0)),
        ],
        out_specs=pl.BlockSpec((1, s, w), lambda i, p: (i, 0, p)),
        out_shape=jax.ShapeDtypeStruct((b, s, c), BF16),
        scratch_shapes=[pltpu.VMEM((n_heads, s, LANES), BF16), pltpu.VMEM((n_heads, s, LANES), F32),
                        pltpu.VMEM((n_heads, s, SB_BLOCK), F32),
                        pltpu.SMEM((s // SB_BLOCK,), F32)],
        compiler_params=_cparams("parallel", "parallel"),
        name="sb_attention",
    )(qkv, qkv, qkv, cs)


def _cross_kernel(y_ref, wy_ref, h_ref, gn_ref, wq_ref, kv_ref, wo_ref, o_ref):
    h = h_ref[0] + _dot(y_ref[0], wy_ref[...])
    c = h.shape[-1]
    xhd = c // X_HEADS
    q = _dot(_rms(h, gn_ref[...]).astype(BF16), wq_ref[...]).astype(BF16)
    outs = []
    for hd in range(X_HEADS):
        qh = q[:, hd * xhd:(hd + 1) * xhd]
        kh = kv_ref[0, :, hd * xhd:(hd + 1) * xhd]
        vh = kv_ref[0, :, c + hd * xhd:c + (hd + 1) * xhd]
        sc = _dot_nt(qh, kh) * (xhd ** -0.5)
        sc = sc - jnp.max(sc, axis=-1, keepdims=True)
        e = jnp.exp(sc)
        p = e / jnp.sum(e, axis=-1, keepdims=True)
        outs.append(_dot(p.astype(BF16), vh).astype(BF16))
    o = jnp.concatenate(outs, axis=-1)
    o_ref[0] = h + _dot(o, wo_ref[...])


def _cross_block(y, wy, h, gn, wq, memkv, wo, tq):
    b, s, c = h.shape
    tok = pl.BlockSpec((1, tq, c), lambda i, t: (i, t, 0))
    full = lambda arr: pl.BlockSpec(arr.shape, lambda i, t: (0,) * arr.ndim)
    return pl.pallas_call(
        _cross_kernel,
        grid=(b, s // tq),
        in_specs=[tok, full(wy), tok, full(gn), full(wq),
                  pl.BlockSpec((1,) + memkv.shape[1:], lambda i, t: (i, 0, 0)), full(wo)],
        out_specs=tok,
        out_shape=jax.ShapeDtypeStruct((b, s, c), F32),
        compiler_params=_cparams("parallel", "parallel"),
        name="cross_attention",
    )(y, wy, h, gn, wq, memkv, wo)


def _lane_min_index(mask, lane):
    return jnp.min(jnp.where(mask, lane, float(ROUTER_LANES)), axis=-1, keepdims=True)


MOE_ROUTE_TILE = 512
MOE_ROW_TILE = 256


def _dot_3pass(a, b):
    ah, al = _split2(a)
    bh, bl = _split2(b)
    return _dot(ah, bl) + _dot(al, bh) + _dot(ah, bh)


def _moe_route_kernel(h_ref, gn_ref, wg_ref, bg_ref, tri_ref, dest_ref, pg_ref, cnt_ref, base_ref, *, n_tokens):
    @pl.when(pl.program_id(0) == 0)
    def _():
        base_ref[...] = jnp.zeros_like(base_ref)

    hn = _rms(h_ref[...], gn_ref[...])
    gl = _dot_3pass(hn, wg_ref[...]) + bg_ref[...]
    lane = lax.broadcasted_iota(jnp.int32, gl.shape, 1).astype(F32)
    gmax = jnp.max(gl, axis=-1, keepdims=True)
    group = _lane_min_index(gl == gmax, lane)
    p_group = 1.0 / jnp.sum(jnp.exp(gl - gmax), axis=-1, keepdims=True)
    onehot = (lane == group).astype(F32)
    earlier = _dot(tri_ref[...], onehot.astype(BF16))
    base = base_ref[0:1, :]
    rank = jnp.sum(onehot * (earlier + base), axis=-1, keepdims=True)
    dest = group * float(n_tokens) + rank
    tm = dest.shape[0]
    dest_row = jnp.broadcast_to(dest, (tm, LANES)).T[0:1, :]
    dest_ref[0] = dest_row.astype(jnp.int32)
    pg_ref[...] = jnp.broadcast_to(p_group, pg_ref.shape)
    new_base = base + earlier[tm - 1:tm, :] + onehot[tm - 1:tm, :]
    base_ref[0:1, :] = new_base
    cnt_ref[...] = jnp.broadcast_to(new_base, cnt_ref.shape)


def _moe_route(h, gn, wg128, bg128, tri, tm):
    t, c = h.shape
    full = lambda arr: pl.BlockSpec(arr.shape, lambda i: (0,) * arr.ndim)
    return pl.pallas_call(
        functools.partial(_moe_route_kernel, n_tokens=t),
        grid=(t // tm,),
        in_specs=[pl.BlockSpec((tm, c), lambda i: (i, 0)), full(gn), full(wg128), full(bg128), full(tri)],
        out_specs=[pl.BlockSpec((1, 1, tm), lambda i: (i, 0, 0)),
                   pl.BlockSpec((tm, LANES), lambda i: (i, 0)),
                   pl.BlockSpec((8, LANES), lambda i: (0, 0))],
        out_shape=[jax.ShapeDtypeStruct((t // tm, 1, tm), jnp.int32),
                   jax.ShapeDtypeStruct((t, LANES), F32),
                   jax.ShapeDtypeStruct((8, LANES), F32)],
        scratch_shapes=[pltpu.VMEM((8, LANES), F32)],
        compiler_params=_cparams("arbitrary"),
        name="moe_route",
    )(h, gn, wg128, bg128, tri)


def _moe_scatter_kernel(zt_ref, dest_ref, h_ref, gn_ref, xs_ref, buf_ref, sem_ref, zsem_ref):
    i = pl.program_id(0)
    n = pl.num_programs(0)
    tm = h_ref.shape[0]
    slot = lax.rem(i, 2)

    def row_copy(s, r, d):
        return pltpu.make_async_copy(buf_ref.at[s, pl.ds(r, 1), :], xs_ref.at[pl.ds(d, 1), :], sem_ref.at[s])

    def wait_all(s):
        pltpu.make_async_copy(buf_ref.at[s], xs_ref.at[pl.ds(0, tm), :], sem_ref.at[s]).wait()

    @pl.when(i == 0)
    def _():
        buf_ref[1] = jnp.zeros(buf_ref.shape[1:], buf_ref.dtype)
        fills = [pltpu.make_async_copy(buf_ref.at[1], xs_ref.at[pl.ds(pl.multiple_of(zt_ref[k] * tm, tm), tm), :],
                                       zsem_ref.at[0]) for k in range(zt_ref.shape[0])]
        for f in fills:
            f.start()
            f.wait()

    buf_ref[slot] = _rms(h_ref[...], gn_ref[...])

    for r in range(tm):
        row_copy(slot, r, dest_ref[0, 0, r]).start(priority=r % 2)

    @pl.when(i > 0)
    def _():
        wait_all(1 - slot)

    @pl.when(i == n - 1)
    def _():
        wait_all(slot)


def _moe_scatter(zero_tiles, h, gn, dest3, n_rows, tm):
    t, c = h.shape
    grid_spec = pltpu.PrefetchScalarGridSpec(
        num_scalar_prefetch=1,
        grid=(t // tm,),
        in_specs=[pl.BlockSpec((1, 1, tm), lambda i, zt: (i, 0, 0), memory_space=pltpu.SMEM),
                  pl.BlockSpec((tm, c), lambda i, zt: (i, 0)),
                  pl.BlockSpec(gn.shape, lambda i, zt: (0, 0))],
        out_specs=pl.BlockSpec(memory_space=pl.ANY),
        scratch_shapes=[pltpu.VMEM((2, tm, c), F32), pltpu.SemaphoreType.DMA((2,)),
                        pltpu.SemaphoreType.DMA((1,))],
    )
    return pl.pallas_call(
        _moe_scatter_kernel,
        grid_spec=grid_spec,
        out_shape=jax.ShapeDtypeStruct((n_rows, c), F32),
        compiler_params=_cparams("arbitrary"),
        name="moe_scatter",
    )(zero_tiles, dest3, h, gn)


def _moe_group_ffn_kernel(tg_ref, x_ref, we_ref, be_ref, w1_ref, w3_ref, w2_ref, o_ref):
    del tg_ref
    x = x_ref[...]
    xh, xl = _split2(x)
    el = be_ref[0] + _dot(xh, we_ref[0, 1]) + _dot(xl, we_ref[0, 0]) + _dot(xh, we_ref[0, 0])
    lane = lax.broadcasted_iota(jnp.int32, el.shape, 1).astype(F32)
    top1 = jnp.max(el, axis=-1, keepdims=True)
    idx1 = _lane_min_index(el == top1, lane)
    el2 = jnp.where(lane == idx1, -jnp.inf, el)
    top2 = jnp.max(el2, axis=-1, keepdims=True)
    idx2 = _lane_min_index(el2 == top2, lane)
    e2 = jnp.exp(top2 - top1)
    s1 = 1.0 / (1.0 + e2)
    s2 = e2 / (1.0 + e2)
    hids = []
    for e in range(EXPERTS_PER_GROUP):
        ge = jnp.where(idx1 == float(e), s1, jnp.where(idx2 == float(e), s2, 0.0))
        h1 = _dot(xh, w1_ref[0, e])
        h3 = _dot(xh, w3_ref[0, e])
        hids.append(((h1 * jax.nn.sigmoid(h1)) * h3 * ge).astype(BF16))
    o_ref[...] = _dot(jnp.concatenate(hids, axis=1), w2_ref[0])


def _moe_group_ffn(tile_group, xs, we2, be, w1, w3, w2, tm):
    rows, c = xs.shape
    g, e, _, f = w1.shape
    grid_spec = pltpu.PrefetchScalarGridSpec(
        num_scalar_prefetch=1,
        grid=(rows // tm,),
        in_specs=[pl.BlockSpec((tm, c), lambda i, tg: (i, 0)),
                  pl.BlockSpec((1, 2, c, LANES), lambda i, tg: (tg[i], 0, 0, 0)),
                  pl.BlockSpec((1, 1, LANES), lambda i, tg: (tg[i], 0, 0)),
                  pl.BlockSpec((1, e, c, f), lambda i, tg: (tg[i], 0, 0, 0)),
                  pl.BlockSpec((1, e, c, f), lambda i, tg: (tg[i], 0, 0, 0)),
                  pl.BlockSpec((1, e * f, c), lambda i, tg: (tg[i], 0, 0))],
        out_specs=pl.BlockSpec((tm, c), lambda i, tg: (i, 0)),
    )
    return pl.pallas_call(
        _moe_group_ffn_kernel,
        grid_spec=grid_spec,
        out_shape=jax.ShapeDtypeStruct((rows, c), F32),
        compiler_params=_cparams("arbitrary"),
        name="moe_group_ffn",
    )(tile_group, xs, we2, be, w1, w3, w2)


def _moe_gather_kernel(dest_ref, nxt_ref, h_ref, pg_ref, gf_ref, ys_ref, o_ref, buf_ref, sem_ref, *, final_norm):
    i = pl.program_id(0)
    n = pl.num_programs(0)
    tm = h_ref.shape[0]
    slot = lax.rem(i, 2)

    def row_copy(s, r, d):
        return pltpu.make_async_copy(ys_ref.at[pl.ds(d, 1), :], buf_ref.at[s, pl.ds(r, 1), :], sem_ref.at[s])

    def start_all(s, idx_ref):
        for r in range(tm):
            row_copy(s, r, idx_ref[0, 0, r]).start(priority=r % 2)

    @pl.when(i == 0)
    def _():
        start_all(0, dest_ref)

    @pl.when(i + 1 < n)
    def _():
        start_all(1 - slot, nxt_ref)

    pltpu.make_async_copy(ys_ref.at[pl.ds(0, tm), :], buf_ref.at[slot], sem_ref.at[slot]).wait()

    y = buf_ref[slot]
    pg = pg_ref[...]
    c = y.shape[1]
    out = h_ref[...] + y * jnp.concatenate([pg] * (c // LANES), axis=1)
    if final_norm:
        out = _rms(out, gf_ref[...])
    o_ref[...] = out


def _moe_gather(h, pg, dest3, ys, g_final, final_norm, tm):
    t, c = h.shape
    n = t // tm
    return pl.pallas_call(
        functools.partial(_moe_gather_kernel, final_norm=final_norm),
        grid=(n,),
        in_specs=[pl.BlockSpec((1, 1, tm), lambda i: (i, 0, 0), memory_space=pltpu.SMEM),
                  pl.BlockSpec((1, 1, tm), lambda i: (jnp.minimum(i + 1, n - 1), 0, 0), memory_space=pltpu.SMEM),
                  pl.BlockSpec((tm, c), lambda i: (i, 0)),
                  pl.BlockSpec((tm, LANES), lambda i: (i, 0)),
                  pl.BlockSpec(g_final.shape, lambda i: (0, 0)),
                  pl.BlockSpec(memory_space=pl.ANY)],
        out_specs=pl.BlockSpec((tm, c), lambda i: (i, 0)),
        out_shape=jax.ShapeDtypeStruct((t, c), F32),
        scratch_shapes=[pltpu.VMEM((2, tm, c), F32), pltpu.SemaphoreType.DMA((2,))],
        compiler_params=_cparams("arbitrary"),
        name="moe_gather",
    )(dest3, dest3, h, pg, g_final, ys)


def _moe_layer(h, gn, wg, bg, we, be, w1, w3, w2, g_final, final_norm):
    t, c = h.shape
    g, e = N_GROUPS, EXPERTS_PER_GROUP
    f = w1.shape[-1]
    tm_r = MOE_ROUTE_TILE if t % MOE_ROUTE_TILE == 0 else t
    tm = MOE_ROW_TILE if t % MOE_ROW_TILE == 0 else t
    neg = -1e30
    wg128 = jnp.zeros((c, LANES), F32).at[:, :g].set(wg)
    bg128 = jnp.full((1, LANES), neg, F32).at[0, :g].set(bg)
    jj = jnp.arange(tm_r)
    tri = (jj[:, None] > jj[None, :]).astype(BF16)
    dest_local, pg, counts = _moe_route(h, gn, wg128, bg128, tri, tm_r)

    cnt = counts[0, :g].astype(jnp.int32)
    tiles = (cnt + tm - 1) // tm
    tile_end = jnp.cumsum(tiles)
    offs = (tile_end - tiles) * tm
    n_tiles = t // tm + g
    dl = dest_local.reshape(t)
    grp = dl // t
    dest = (dl - grp * t + offs[grp]).reshape(t // tm, 1, tm)
    tile_group = jnp.minimum(jnp.sum(jnp.arange(n_tiles)[:, None] >= tile_end[None, :], axis=1), g - 1).astype(jnp.int32)

    zero_tiles = jnp.concatenate([jnp.maximum(tile_end - 1, 0),
                                  jnp.minimum(tile_end[g - 1] + jnp.arange(g), n_tiles - 1)]).astype(jnp.int32)
    xs = _moe_scatter(zero_tiles, h, gn, dest, n_tiles * tm, tm)
    we_g = jnp.zeros((g, c, LANES), F32).at[:, :, :e].set(we.reshape(c, g, e).transpose(1, 0, 2))
    we2 = jnp.stack(_split2(we_g), axis=1)
    be_g = jnp.full((g, 1, LANES), neg, F32).at[:, 0, :e].set(be.reshape(g, e))
    ys = _moe_group_ffn(tile_group, xs, we2, be_g, w1.astype(BF16), w3.astype(BF16),
                        w2.astype(BF16).reshape(g, e * f, c), tm)
    return _moe_gather(h, pg, dest, ys, g_final, final_norm, tm)


def _block_ones(n, blk):
    i = jnp.arange(n)
    return ((i[:, None] // blk) == (i[None, :] // blk)).astype(BF16)


def _tile(n, pref):
    return pref if n % pref == 0 else n


def kernel(x, mem, norm_mix, norm_cross, norm_ffn, norm_mem, norm_final, rw_mix, rw_wr, rw_wk, rw_wv, rw_w0, rw_w1, rw_w2, rw_a0, rw_a1, rw_a2, rw_g1, rw_g2, rw_kk, rw_ka, rw_rk, rw_lnx_w, rw_lnx_b, rw_wo, sb_wqkv, sb_wo, xa_wq, xa_wkv, xa_wo, moe_wg, moe_bg, moe_we, moe_be, moe_w1, moe_w3, moe_w2):
    b, s, c = x.shape
    t = b * s
    depth = norm_mix.shape[0]
    row = lambda vec: vec.reshape(1, -1).astype(F32)
    bf = lambda w: w.astype(BF16)

    ones_quad = _block_ones(MXU_DIM, HEAD_DIM)
    ones_pair = _block_ones(LANES, HEAD_DIM)
    tri_incl = jnp.tile((jnp.arange(CHUNK)[:, None] >= jnp.arange(CHUNK)[None, :]).astype(BF16), (1, 3))
    jj = jnp.arange(SB_BLOCK)
    sb_cs = jnp.concatenate([(jj[:, None] > jj[None, :]).astype(BF16),
                             jnp.ones((SB_BLOCK, SB_BLOCK), BF16)], axis=1)

    memkv = _norm_matmul(mem.reshape(b * N_MEM, c), row(norm_mem), bf(xa_wkv), BF16,
                         _tile(b * N_MEM, 512), _tile(2 * c, 1024)).reshape(b, N_MEM, 2 * c)

    h = x
    for i in range(depth):
        j = i // 2
        if i % 2 == 0:
            r, lw, k, v, kkn, bb, g = _rwkv_proj(
                h, row(norm_mix[i]), rw_mix[j].astype(F32), bf(rw_wr[j]), bf(rw_wk[j]), bf(rw_wv[j]),
                bf(rw_w1[j]), bf(rw_w2[j]), bf(rw_a1[j]), bf(rw_a2[j]), bf(rw_g1[j]), bf(rw_g2[j]),
                row(rw_w0[j]), row(rw_a0[j]), row(rw_kk[j]), row(rw_ka[j]), ones_quad, _tile(s, 512))
            y = _rwkv_recurrence(r, lw, k, v, kkn, bb, g, row(rw_rk[j]), row(rw_lnx_w[j]),
                                 row(rw_lnx_b[j]), tri_incl, ones_pair, _tile(s, 256), 8 * LANES)
            w_mix_out = bf(rw_wo[j])
        else:
            qkv = _norm_matmul(h.reshape(t, c), row(norm_mix[i]), bf(sb_wqkv[j]), BF16,
                               _tile(t, 1024), _tile(3 * c, 1024)).reshape(b, s, 3 * c)
            y = _sb_attention(qkv, sb_cs)
            w_mix_out = bf(sb_wo[j])
        h = _cross_block(y, w_mix_out, h, row(norm_cross[i]), bf(xa_wq[i]), memkv, bf(xa_wo[i]), _tile(s, 512))

        h = _moe_layer(h.reshape(t, c), row(norm_ffn[i]), moe_wg[i], moe_bg[i], moe_we[i], moe_be[i],
                       moe_w1[i], moe_w3[i], moe_w2[i], row(norm_final), i == depth - 1).reshape(b, s, c)
    return h
```

```python
import functools

import jax
import jax.numpy as jnp
from jax import lax
from jax.experimental import pallas as pl
from jax.experimental.pallas import tpu as pltpu

F32 = jnp.float32
BF16 = jnp.bfloat16

HEAD_DIM = 64
N_MEM = 256
X_HEADS = 4
N_GROUPS = 4
EXPERTS_PER_GROUP = 8
N_EXPERTS = N_GROUPS * EXPERTS_PER_GROUP
D_EXPERT = 256
GN_EPS = 64e-5
NORM_EPS = 1e-6
SB_BLOCK = 128
CHUNK = 64

LANES = 128
MXU_DIM = 256
VMEM_LIMIT = 48 * 1024 * 1024

ROUTER_LANES = LANES


def _cparams(*sem):
    return pltpu.CompilerParams(dimension_semantics=sem, vmem_limit_bytes=VMEM_LIMIT)


def _dot(a, b):
    return jnp.dot(a, b, preferred_element_type=F32)


def _dot_nt(a, b):
    return lax.dot_general(a, b, (((1,), (1,)), ((), ())), preferred_element_type=F32)


def _split2(x):
    hi = x.astype(BF16)
    lo = (x - hi.astype(F32)).astype(BF16)
    return hi, lo


def _split3(x):
    hi = x.astype(BF16)
    r1 = x - hi.astype(F32)
    mid = r1.astype(BF16)
    lo = (r1 - mid.astype(F32)).astype(BF16)
    return hi, mid, lo


def _dot_exact_rhs(x, m_bf16, parts):
    pieces = _split3(x) if parts == 3 else _split2(x)
    if parts * x.shape[1] <= MXU_DIM:
        return _dot(jnp.concatenate(pieces, axis=1), jnp.concatenate([m_bf16] * parts, axis=0))
    acc = _dot(pieces[0], m_bf16)
    for p in pieces[1:]:
        acc = acc + _dot(p, m_bf16)
    return acc


def _dot_exact_lhs3(m3_bf16, x):
    return _dot(m3_bf16, jnp.concatenate(_split3(x), axis=0))


def _rms(x, g):
    ms = jnp.mean(x * x, axis=-1, keepdims=True)
    return x * lax.rsqrt(ms + NORM_EPS) * g


def _head_sum(x, ones_bd):
    c = x.shape[-1]
    outs = []
    for j in range(c // MXU_DIM):
        outs.append(_dot_exact_rhs(x[:, j * MXU_DIM:(j + 1) * MXU_DIM], ones_bd, 2))
    return jnp.concatenate(outs, axis=-1) if len(outs) > 1 else outs[0]


def _norm_mm_kernel(x_ref, g_ref, w_ref, o_ref, xn_ref):
    @pl.when(pl.program_id(1) == 0)
    def _():
        xn_ref[...] = _rms(x_ref[...], g_ref[...]).astype(BF16)

    o_ref[...] = _dot(xn_ref[...], w_ref[...]).astype(o_ref.dtype)


def _norm_matmul(x, g, w, out_dtype, tm, tn):
    m, c = x.shape
    n = w.shape[1]
    return pl.pallas_call(
        _norm_mm_kernel,
        grid=(m // tm, n // tn),
        in_specs=[
            pl.BlockSpec((tm, c), lambda i, j: (i, 0)),
            pl.BlockSpec((1, c), lambda i, j: (0, 0)),
            pl.BlockSpec((c, tn), lambda i, j: (0, j)),
        ],
        out_specs=pl.BlockSpec((tm, tn), lambda i, j: (i, j)),
        out_shape=jax.ShapeDtypeStruct((m, n), out_dtype),
        scratch_shapes=[pltpu.VMEM((tm, c), BF16)],
        compiler_params=_cparams("parallel", "arbitrary"),
        name="norm_matmul",
    )(x, g, w)


PROJ_SUBTILE = 256


def _softplus(x):
    return jnp.maximum(x, 0.0) + jnp.log(1.0 + jnp.exp(-jnp.abs(x)))


def _rwkv_proj_kernel(h_ref, gn_ref, mix_ref, wr_ref, wk_ref, wv_ref, w1_ref, w2_ref, a1_ref, a2_ref,
                      g1_ref, g2_ref, w0_ref, a0_ref, kk_ref, ka_ref, ones_ref,
                      r_out, lw_out, k_out, v_out, kkn_out, b_out, g_out, carry_ref):
    ts = h_ref.shape[1]
    hn = _rms(h_ref[0], gn_ref[...])

    @pl.when(pl.program_id(1) == 0)
    def _():
        carry_ref[...] = jnp.zeros_like(carry_ref)

    prev_last = carry_ref[0:1, :]
    row = lax.broadcasted_iota(jnp.int32, hn.shape, 0)
    hp = jnp.where(row == 0, prev_last, pltpu.roll(hn, 1, 0))
    carry_ref[0:1, :] = hn[ts - 1:ts, :]
    dx = hp - hn

    def project(rows):
        hn_s, dx_s = hn[rows], dx[rows]
        mixed = lambda i: (hn_s + dx_s * mix_ref[i:i + 1, :]).astype(BF16)
        r = _dot(mixed(0), wr_ref[...])
        lora_w = _dot(jnp.tanh(_dot(mixed(1), w1_ref[...])).astype(BF16), w2_ref[...])
        k = _dot(mixed(2), wk_ref[...])
        v = _dot(mixed(3), wv_ref[...])
        lora_a = _dot(_dot(mixed(4), a1_ref[...]).astype(BF16), a2_ref[...])
        g = _dot(jax.nn.sigmoid(_dot(mixed(5), g1_ref[...])).astype(BF16), g2_ref[...])
        return r, lora_w, k, v, lora_a, g

    def finish(rows, r, lora_w, k, v, lora_a, g):
        w_log = -_softplus(-(w0_ref[...] + lora_w)) - 0.5
        a = jax.nn.sigmoid(a0_ref[...] + lora_a)
        kk = k * kk_ref[...]
        nrm = jnp.sqrt(_head_sum(kk * kk, ones_ref[...]))
        kkn = kk / jnp.maximum(nrm, 1e-12)
        r_out[0, rows] = r.astype(r_out.dtype)
        lw_out[0, rows] = -jnp.exp(w_log)
        k_out[0, rows] = (k * (1.0 + (a - 1.0) * ka_ref[...])).astype(k_out.dtype)
        v_out[0, rows] = v.astype(v_out.dtype)
        kkn_out[0, rows] = kkn.astype(kkn_out.dtype)
        b_out[0, rows] = (kkn * a).astype(b_out.dtype)
        g_out[0, rows] = g.astype(g_out.dtype)

    sub = min(PROJ_SUBTILE, ts)
    pending = None
    for i in range(ts // sub + 1):
        rows = slice(i * sub, (i + 1) * sub)
        cur = (rows,) + project(rows) if i < ts // sub else None
        if pending is not None:
            finish(*pending)
        pending = cur


def _rwkv_proj(h, gn, mix, wr, wk, wv, w1, w2, a1, a2, g1, g2, w0, a0, k_k, k_a, ones_bd, ts):
    b, s, c = h.shape
    full = lambda arr: pl.BlockSpec(arr.shape, lambda i, j: (0,) * arr.ndim)
    tok = pl.BlockSpec((1, ts, c), lambda i, j: (i, j, 0))
    params = (gn, mix, wr, wk, wv, w1, w2, a1, a2, g1, g2, w0, a0, k_k, k_a, ones_bd)
    return pl.pallas_call(
        _rwkv_proj_kernel,
        grid=(b, s // ts),
        in_specs=[tok] + [full(p) for p in params],
        out_specs=[tok] * 7,
        out_shape=[jax.ShapeDtypeStruct((b, s, c), F32 if n == 1 else BF16) for n in range(7)],
        scratch_shapes=[pltpu.VMEM((8, c), F32)],
        compiler_params=_cparams("parallel", "arbitrary"),
        name="rwkv_proj",
    )(h, *params)


def _rwkv_rec_kernel(r_ref, lw_ref, k_ref, v_ref, kk_ref, b_ref, g_ref, rk_ref, lnw_ref, lnb_ref,
                     tri_ref, ones_ref, o_ref,
                     state_ref, xp_ref, vp_ref, y0_ref, bdt_ref, svt_ref, dl_ref, bon_ref):
    ts, lw_lanes = r_ref.shape[1], r_ref.shape[2]
    n_chunks = ts // CHUNK
    n_groups = lw_lanes // LANES
    L = CHUNK
    prep_chunks = 2 if n_chunks % 2 == 0 else 1

    @pl.when(pl.program_id(2) == 0)
    def _():
        state_ref[...] = jnp.zeros_like(state_ref)

    lane = lax.broadcasted_iota(jnp.int32, (L, LANES), 1)
    head0 = lane < HEAD_DIM
    rr = lax.broadcasted_iota(jnp.int32, (2 * L, 2 * L), 0)
    cc = lax.broadcasted_iota(jnp.int32, (2 * L, 2 * L), 1)
    same_head = (rr // L) == (cc // L)
    strict = same_head & ((cc % L) < (rr % L))
    incl = same_head & ((cc % L) <= (rr % L))
    eye = (rr == cc).astype(F32)
    tri_incl = tri_ref[...]
    ones_bd = ones_ref[...]

    def stack_heads(x):
        return jnp.concatenate([jnp.where(head0, x, 0.0), jnp.where(head0, 0.0, x)], axis=0)

    def unstack(x):
        return x[:L] + x[L:]

    def prepare(ci, tick):
        chains = [(ci * prep_chunks + cc, gi) for cc in range(prep_chunks) for gi in range(n_groups)]
        rows = [pl.ds(pl.multiple_of(c * L, L), L) for c, _ in chains]
        lanes = [slice(gi * LANES, (gi + 1) * LANES) for _, gi in chains]
        idxs = [c * n_groups + gi for c, gi in chains]
        each = lambda f, *cols: [f(*a) for a in zip(*cols)]
        load = lambda ref: each(lambda rw, ls: ref[0, rw, ls].astype(F32), rows, lanes)
        r, lw, k, v, kk, bb = (load(ref) for ref in (r_ref, lw_ref, k_ref, v_ref, kk_ref, b_ref))

        cum = each(lambda x: _dot_exact_lhs3(tri_incl, x), lw)
        d_inv = each(lambda c_: jnp.exp(-c_), cum)
        d_last = each(lambda c_: jnp.exp(c_[L - 1:L, :]), cum)
        xa = each(lambda c_, l_, kk_: stack_heads(-(jnp.exp(c_ - l_) * kk_)).astype(BF16), cum, lw, kk)
        xr32 = each(lambda c_, r_: stack_heads(jnp.exp(c_) * r_), cum, r)
        bt32 = each(lambda b_, d_: b_ * d_, bb, d_inv)
        kt32 = each(lambda k_, d_: k_ * d_, k, d_inv)

        def gram(xa_, xr_, bt_, kt_):
            bt16, kt16 = bt_.astype(BF16), kt_.astype(BF16)
            z_all = jnp.concatenate([bt16, bt16, kt16, kt16], axis=0)
            x_all = jnp.concatenate([xa_, xr_.astype(BF16)], axis=0)
            return _dot_nt(x_all, z_all)

        gmat = each(gram, xa, xr32, bt32, kt32)
        tick()
        m_ab = each(lambda g_: jnp.where(strict, g_[:2 * L, :2 * L], 0.0), gmat)
        m_ak = each(lambda g_: jnp.where(strict, g_[:2 * L, 2 * L:], 0.0).astype(BF16), gmat)
        m_rb = each(lambda g_: jnp.where(incl, g_[2 * L:, :2 * L], 0.0).astype(BF16), gmat)
        m_rk = each(lambda g_: jnp.where(incl, g_[2 * L:, 2 * L:], 0.0).astype(BF16), gmat)

        p = m_ab
        t_inv = each(lambda m_: eye + m_, m_ab)
        for _ in range(5):
            p = each(lambda p_: _dot(p_.astype(BF16), p_.astype(BF16)), p)
            t_inv = each(lambda t_, p_: t_ + _dot(t_.astype(BF16), p_.astype(BF16)), t_inv, p)
            tick()
        t16 = each(lambda t_: t_.astype(BF16), t_inv)

        v_st = each(lambda v_: stack_heads(v_).astype(BF16), v)
        mv = each(_dot, m_ak, v_st)
        tick()
        tav = each(lambda t_, xa_, mv_: _dot(t_, jnp.concatenate([xa_, mv_.astype(BF16)], axis=1)),
                   t16, xa, mv)
        tick()
        rby = each(lambda m_, tav_: _dot(m_, tav_.astype(BF16)), m_rb, tav)
        tick()
        y0 = each(lambda rby_, m_, v_: rby_[:, LANES:] + _dot(m_, v_), rby, m_rk, v_st)
        svt = each(lambda kt_, dl_, v_: _dot(stack_heads(kt_ * dl_).T.astype(BF16), v_), kt32, d_last, v_st)
        for n, idx in enumerate(idxs):
            xp_ref[idx] = jnp.concatenate([tav[n][:, :LANES], xr32[n] + rby[n][:, :LANES]], axis=0).astype(BF16)
            vp_ref[idx] = tav[n][:, LANES:]
            y0_ref[idx] = y0[n]
            bdt_ref[idx] = stack_heads(bt32[n] * d_last[n]).T.astype(BF16)
            svt_ref[idx] = svt[n]
            dl_ref[idx] = jnp.broadcast_to(d_last[n], (LANES, LANES)).T
            bon_ref[rows[n], lanes[n]] = _dot_exact_rhs(r[n] * k[n] * rk_ref[:, lanes[n]], ones_bd, 2) * v[n]

    def advance_steps(ci):
        rows = pl.ds(pl.multiple_of(ci * L, L), L)
        groups = list(range(n_groups))
        lanes = [slice(gi * LANES, (gi + 1) * LANES) for gi in groups]
        idxs = [ci * n_groups + gi for gi in groups]
        each = lambda f, *cols: [f(*a) for a in zip(*cols)]
        s_kv = each(lambda gi: state_ref[gi], groups)
        xs = each(lambda i_, s_: _dot(xp_ref[i_], s_.astype(BF16)), idxs, s_kv)
        yield
        u16 = each(lambda x_, i_: (x_[:2 * L] + vp_ref[i_]).astype(BF16), xs, idxs)
        upd = each(lambda i_, u_: _dot(bdt_ref[i_], u_), idxs, u16)
        for gi in groups:
            state_ref[gi] = s_kv[gi] * dl_ref[idxs[gi]] + svt_ref[idxs[gi]] + upd[gi]
        yield

        inv_n = 1.0 / HEAD_DIM
        y = each(lambda x_, i_: unstack(x_[2 * L:] + y0_ref[i_]), xs, idxs)
        yc = each(lambda y_: y_ - _dot_exact_rhs(y_, ones_bd, 2) * inv_n, y)
        yield
        var = each(lambda yc_: _dot_exact_rhs(yc_ * yc_, ones_bd, 2) * inv_n, yc)
        for gi in groups:
            ls = lanes[gi]
            yn = yc[gi] * lax.rsqrt(var[gi] + GN_EPS) * lnw_ref[:, ls] + lnb_ref[:, ls]
            o_ref[0, rows, ls] = ((yn + bon_ref[rows, ls]) * g_ref[0, rows, ls].astype(F32)).astype(o_ref.dtype)
        yield

    n_prep = n_chunks // prep_chunks

    def advance_pair(pi):
        for cc in range(prep_chunks):
            yield from advance_steps(pi * prep_chunks + cc)

    def prepare_with(pi, side):
        prepare(pi, (lambda: next(side, None)) if side is not None else (lambda: None))
        if side is not None:
            for _ in side:
                pass

    prepare_with(jnp.int32(0), None)

    def overlapped(pi, carry):
        prepare_with(pi, advance_pair(pi - 1))
        return carry

    lax.fori_loop(1, n_prep, overlapped, 0)
    for _ in advance_pair(jnp.int32(n_prep - 1)):
        pass


def _rwkv_recurrence(r, lw, k, v, kkn, bb, g, r_k, lnx_w, lnx_b, tri, ones_pair, ts, lane_w):
    b, s, c = r.shape
    tok = pl.BlockSpec((1, ts, lane_w), lambda i, j, t: (i, t, j))
    vec = pl.BlockSpec((1, lane_w), lambda i, j, t: (0, j))
    small = lambda arr: pl.BlockSpec(arr.shape, lambda i, j, t: (0,) * arr.ndim)
    n_groups = lane_w // LANES
    ncg = (ts // CHUNK) * n_groups
    return pl.pallas_call(
        _rwkv_rec_kernel,
        grid=(b, c // lane_w, s // ts),
        in_specs=[tok] * 7 + [vec, vec, vec, small(tri), small(ones_pair)],
        out_specs=tok,
        out_shape=jax.ShapeDtypeStruct((b, s, c), BF16),
        scratch_shapes=[pltpu.VMEM((n_groups, LANES, LANES), F32),
                        pltpu.VMEM((ncg, 4 * CHUNK, LANES), BF16),
                        pltpu.VMEM((ncg, 2 * CHUNK, LANES), F32),
                        pltpu.VMEM((ncg, 2 * CHUNK, LANES), F32),
                        pltpu.VMEM((ncg, LANES, 2 * CHUNK), BF16),
                        pltpu.VMEM((ncg, LANES, LANES), F32),
                        pltpu.VMEM((ncg, LANES, LANES), F32),
                        pltpu.VMEM((ts, lane_w), F32)],
        compiler_params=_cparams("parallel", "parallel", "arbitrary"),
        name="rwkv_recurrence",
    )(r, lw, k, v, kkn, bb, g, r_k, lnx_w, lnx_b, tri, ones_pair)


SB_QTILE = 512
SB_SUBTILE = 128
SB_STAGE_LAG = 3
SB_SKIP_BELOW = -104.0
SB_PAIRS = 4


def _sb_attn_kernel(q_ref, k_ref, v_ref, cs_ref, o_ref, qs_ref, acc_ref, run_ref, tmax_ref):
    s = q_ref.shape[1]
    tb = SB_BLOCK
    tq = min(SB_QTILE, s)
    rs = SB_SUBTILE
    sub_per_block = tb // rs
    bpt = tq // tb
    kpb = 2 if bpt % 2 == 0 else 1
    nb, nt = s // tb, s // tq
    shift = bpt.bit_length() - 1
    lane = lax.broadcasted_iota(jnp.int32, (s, LANES), 1)
    head0 = lane < HEAD_DIM
    n_heads = qs_ref.shape[0]
    for pp in range(n_heads // 2):
        q = q_ref[0, :, pp * LANES:(pp + 1) * LANES] * (HEAD_DIM ** -0.5)
        qs_ref[2 * pp] = jnp.where(head0, q, jnp.zeros_like(q))
        qs_ref[2 * pp + 1] = jnp.where(head0, jnp.zeros_like(q), q)
    acc_ref[...] = jnp.zeros_like(acc_ref)
    run_ref[...] = jnp.zeros_like(run_ref)
    for qb in range(nb):
        tmax_ref[qb] = 0.0
    cs = cs_ref[...]
    cs2 = jnp.concatenate([cs, cs], axis=0)

    def tile_step(ti, blocks, pieces):
        r0 = ti * tq
        subs = [(kbi, h, pl.ds(pl.multiple_of(r0 + i * rs, rs), rs), r0 + i * rs, diag, i // sub_per_block)
                for kbi, i, diag in pieces for h in range(n_heads)]
        n = len(subs)
        lag = SB_STAGE_LAG
        z, lb, parts, sums, att, before = ({} for _ in range(6))
        top = {}
        for step in range(n + 2 + 2 * lag):
            if step < n:
                kbi, h, rows = subs[step][:3]
                z[step] = _dot_nt(qs_ref[h, rows, :], blocks[kbi][1][h // 2])
            i = step - 1
            if 0 <= i < n:
                zz = z.pop(i)
                sp = jnp.log(1.0 + jnp.exp(-jnp.abs(zz)))
                lb[i] = jnp.minimum(zz, 0.0) - sp
                log_fail = lb[i] - zz
                if subs[i][4]:
                    t_idx = subs[i][3] + lax.broadcasted_iota(jnp.int32, (rs, tb), 0)
                    s_idx = blocks[subs[i][0]][0] + lax.broadcasted_iota(jnp.int32, (rs, tb), 1)
                    before[i] = s_idx < t_idx
                    log_fail = jnp.where(before[i], log_fail, 0.0)
                parts[i] = _split2(log_fail)
            i = step - 1 - lag
            if 0 <= i < n:
                hi, lo = parts.pop(i)
                sums[i] = _dot(jnp.concatenate([hi, lo], axis=1), cs2)
            i = step - 2 - lag
            if 0 <= i < n:
                kbi, h, rows, _, diag, _ = subs[i]
                sm = sums.pop(i)
                run = run_ref[h, rows, :]
                a = jnp.exp(lb.pop(i) + run + sm[:, :tb])
                if diag:
                    a = jnp.where(before.pop(i), a, 0.0)
                att[i] = a.astype(BF16)
                run = run + sm[:, tb:]
                run_ref[h, rows, :] = run
                if kbi == len(blocks) - 1:
                    qb = subs[i][5]
                    top[qb] = run if qb not in top else jnp.maximum(top[qb], run)
            i = step - 2 - 2 * lag
            if 0 <= i < n:
                kbi, h, rows = subs[i][:3]
                acc_ref[h, rows, :] += _dot(att.pop(i), blocks[kbi][2][h // 2])
        for qb, t in top.items():
            tmax_ref[ti * bpt + qb] = jnp.max(t)

    def below_pieces(n_q):
        return [(kbi, i, False) for kbi in range(kpb) for i in range(n_q * sub_per_block)]

    def diagonal_pieces(u, q_last):
        jl = kpb * u + kpb - 1
        out = []
        for kbi in range(kpb):
            kl = jl - kbi
            for qb in range(kl, q_last + 1):
                for i in range(qb * sub_per_block, (qb + 1) * sub_per_block):
                    out.append((kbi, i, qb == kl))
        return out

    def last_alive(ti, first):
        n = jnp.int32(first)
        for qb in range(first, bpt):
            n = jnp.where(tmax_ref[ti * bpt + qb] >= SB_SKIP_BELOW, qb + 1, n)
        return n

    def key_blocks(jj, carry):
        j = nb - 1 - kpb * jj
        blocks = []
        for d in range(kpb):
            k0 = pl.multiple_of((j - d) * tb, tb)
            lanes = [slice(pp * LANES, (pp + 1) * LANES) for pp in range(n_heads // 2)]
            blocks.append((k0, [k_ref[0, pl.ds(k0, tb), ls] for ls in lanes],
                           [v_ref[0, pl.ds(k0, tb), ls] for ls in lanes]))
        t0 = lax.shift_right_logical(j, shift)
        case = lax.shift_right_logical(j - t0 * bpt, kpb.bit_length() - 1)
        for u in range(bpt // kpb):
            jl = kpb * u + kpb - 1
            n_q = last_alive(t0, jl + 1)
            for q_last in range(jl, bpt):
                @pl.when((case == u) & (n_q == q_last + 1))
                def _():
                    tile_step(t0, blocks, diagonal_pieces(u, q_last))

        def below(ti, c):
            n_q = last_alive(ti, 0)
            for m in range(1, bpt + 1):
                @pl.when(n_q == m)
                def _():
                    tile_step(ti, blocks, below_pieces(m))
            return c

        lax.fori_loop(t0 + 1, nt, below, 0)
        return carry

    lax.fori_loop(0, nb // kpb, key_blocks, 0)
    for pp in range(n_heads // 2):
        o_ref[0, :, pp * LANES:(pp + 1) * LANES] = jnp.where(head0, acc_ref[2 * pp], acc_ref[2 * pp + 1]).astype(o_ref.dtype)


def _sb_attention(qkv, cs):
    b, s, c3 = qkv.shape
    c = c3 // 3
    w = LANES * SB_PAIRS
    n_blk = c // w
    n_heads = 2 * SB_PAIRS
    return pl.pallas_call(
        _sb_attn_kernel,
        grid=(b, n_blk),
        in_specs=[
            pl.BlockSpec((1, s, w), lambda i, p: (i, 0, p)),
            pl.BlockSpec((1, s, w), lambda i, p: (i, 0, n_blk + p)),
            pl.BlockSpec((1, s, w), lambda i, p: (i, 0, 2 * n_blk + p)),
            pl.BlockSpec(cs.shape, lambda i, p: (0, 0)),
        ],
        out_specs=pl.BlockSpec((1, s, w), lambda i, p: (i, 0, p)),
        out_shape=jax.ShapeDtypeStruct((b, s, c), BF16),
        scratch_shapes=[pltpu.VMEM((n_heads, s, LANES), BF16), pltpu.VMEM((n_heads, s, LANES), F32),
                        pltpu.VMEM((n_heads, s, SB_BLOCK), F32),
                        pltpu.SMEM((s // SB_BLOCK,), F32)],
        compiler_params=_cparams("parallel", "parallel"),
        name="sb_attention",
    )(qkv, qkv, qkv, cs)


def _cross_kernel(y_ref, wy_ref, h_ref, gn_ref, wq_ref, kv_ref, wo_ref, o_ref):
    h = h_ref[0] + _dot(y_ref[0], wy_ref[...])
    c = h.shape[-1]
    xhd = c // X_HEADS
    q = _dot(_rms(h, gn_ref[...]).astype(BF16), wq_ref[...]).astype(BF16)
    outs = []
    for hd in range(X_HEADS):
        qh = q[:, hd * xhd:(hd + 1) * xhd]
        kh = kv_ref[0, :, hd * xhd:(hd + 1) * xhd]
        vh = kv_ref[0, :, c + hd * xhd:c + (hd + 1) * xhd]
        sc = _dot_nt(qh, kh) * (xhd ** -0.5)
        sc = sc - jnp.max(sc, axis=-1, keepdims=True)
        e = jnp.exp(sc)
        p = e / jnp.sum(e, axis=-1, keepdims=True)
        outs.append(_dot(p.astype(BF16), vh).astype(BF16))
    o = jnp.concatenate(outs, axis=-1)
    o_ref[0] = h + _dot(o, wo_ref[...])


def _cross_block(y, wy, h, gn, wq, memkv, wo, tq):
    b, s, c = h.shape
    tok = pl.BlockSpec((1, tq, c), lambda i, t: (i, t, 0))
    full = lambda arr: pl.BlockSpec(arr.shape, lambda i, t: (0,) * arr.ndim)
    return pl.pallas_call(
        _cross_kernel,
        grid=(b, s // tq),
        in_specs=[tok, full(wy), tok, full(gn), full(wq),
                  pl.BlockSpec((1,) + memkv.shape[1:], lambda i, t: (i, 0, 0)), full(wo)],
        out_specs=tok,
        out_shape=jax.ShapeDtypeStruct((b, s, c), F32),
        compiler_params=_cparams("parallel", "parallel"),
        name="cross_attention",
    )(y, wy, h, gn, wq, memkv, wo)


def _lane_min_index(mask, lane):
    return jnp.min(jnp.where(mask, lane, float(ROUTER_LANES)), axis=-1, keepdims=True)


MOE_ROUTE_TILE = 512
MOE_ROW_TILE = 256


def _dot_3pass(a, b):
    ah, al = _split2(a)
    bh, bl = _split2(b)
    return _dot(ah, bl) + _dot(al, bh) + _dot(ah, bh)


def _moe_route_kernel(h_ref, gn_ref, wg_ref, bg_ref, tri_ref, dest_ref, pg_ref, cnt_ref, base_ref, *, n_tokens):
    @pl.when(pl.program_id(0) == 0)
    def _():
        base_ref[...] = jnp.zeros_like(base_ref)

    hn = _rms(h_ref[...], gn_ref[...])
    gl = _dot_3pass(hn, wg_ref[...]) + bg_ref[...]
    lane = lax.broadcasted_iota(jnp.int32, gl.shape, 1).astype(F32)
    gmax = jnp.max(gl, axis=-1, keepdims=True)
    group = _lane_min_index(gl == gmax, lane)
    p_group = 1.0 / jnp.sum(jnp.exp(gl - gmax), axis=-1, keepdims=True)
    onehot = (lane == group).astype(F32)
    earlier = _dot(tri_ref[...], onehot.astype(BF16))
    base = base_ref[0:1, :]
    rank = jnp.sum(onehot * (earlier + base), axis=-1, keepdims=True)
    dest = group * float(n_tokens) + rank
    tm = dest.shape[0]
    dest_row = jnp.broadcast_to(dest, (tm, LANES)).T[0:1, :]
    dest_ref[0] = dest_row.astype(jnp.int32)
    pg_ref[...] = jnp.broadcast_to(p_group, pg_ref.shape)
    new_base = base + earlier[tm - 1:tm, :] + onehot[tm - 1:tm, :]
    base_ref[0:1, :] = new_base
    cnt_ref[...] = jnp.broadcast_to(new_base, cnt_ref.shape)


def _moe_route(h, gn, wg128, bg128, tri, tm):
    t, c = h.shape
    full = lambda arr: pl.BlockSpec(arr.shape, lambda i: (0,) * arr.ndim)
    return pl.pallas_call(
        functools.partial(_moe_route_kernel, n_tokens=t),
        grid=(t // tm,),
        in_specs=[pl.BlockSpec((tm, c), lambda i: (i, 0)), full(gn), full(wg128), full(bg128), full(tri)],
        out_specs=[pl.BlockSpec((1, 1, tm), lambda i: (i, 0, 0)),
                   pl.BlockSpec((tm, LANES), lambda i: (i, 0)),
                   pl.BlockSpec((8, LANES), lambda i: (0, 0))],
        out_shape=[jax.ShapeDtypeStruct((t // tm, 1, tm), jnp.int32),
                   jax.ShapeDtypeStruct((t, LANES), F32),
                   jax.ShapeDtypeStruct((8, LANES), F32)],
        scratch_shapes=[pltpu.VMEM((8, LANES), F32)],
        compiler_params=_cparams("arbitrary"),
        name="moe_route",
    )(h, gn, wg128, bg128, tri)


def _moe_scatter_kernel(zt_ref, dest_ref, h_ref, gn_ref, xs_ref, buf_ref, sem_ref, zsem_ref):
    i = pl.program_id(0)
    n = pl.num_programs(0)
    tm = h_ref.shape[0]
    slot = lax.rem(i, 2)

    def row_copy(s, r, d):
        return pltpu.make_async_copy(buf_ref.at[s, pl.ds(r, 1), :], xs_ref.at[pl.ds(d, 1), :], sem_ref.at[s])

    def wait_all(s):
        pltpu.make_async_copy(buf_ref.at[s], xs_ref.at[pl.ds(0, tm), :], sem_ref.at[s]).wait()

    @pl.when(i == 0)
    def _():
        buf_ref[1] = jnp.zeros(buf_ref.shape[1:], buf_ref.dtype)
        fills = [pltpu.make_async_copy(buf_ref.at[1], xs_ref.at[pl.ds(pl.multiple_of(zt_ref[k] * tm, tm), tm), :],
                                       zsem_ref.at[0]) for k in range(zt_ref.shape[0])]
        for f in fills:
            f.start()
            f.wait()

    buf_ref[slot] = _rms(h_ref[...], gn_ref[...])

    for r in range(tm):
        row_copy(slot, r, dest_ref[0, 0, r]).start(priority=r % 2)

    @pl.when(i > 0)
    def _():
        wait_all(1 - slot)

    @pl.when(i == n - 1)
    def _():
        wait_all(slot)


def _moe_scatter(zero_tiles, h, gn, dest3, n_rows, tm):
    t, c = h.shape
    grid_spec = pltpu.PrefetchScalarGridSpec(
        num_scalar_prefetch=1,
        grid=(t // tm,),
        in_specs=[pl.BlockSpec((1, 1, tm), lambda i, zt: (i, 0, 0), memory_space=pltpu.SMEM),
                  pl.BlockSpec((tm, c), lambda i, zt: (i, 0)),
                  pl.BlockSpec(gn.shape, lambda i, zt: (0, 0))],
        out_specs=pl.BlockSpec(memory_space=pl.ANY),
        scratch_shapes=[pltpu.VMEM((2, tm, c), F32), pltpu.SemaphoreType.DMA((2,)),
                        pltpu.SemaphoreType.DMA((1,))],
    )
    return pl.pallas_call(
        _moe_scatter_kernel,
        grid_spec=grid_spec,
        out_shape=jax.ShapeDtypeStruct((n_rows, c), F32),
        compiler_params=_cparams("arbitrary"),
        name="moe_scatter",
    )(zero_tiles, dest3, h, gn)


def _moe_group_ffn_kernel(tg_ref, x_ref, we_ref, be_ref, w1_ref, w3_ref, w2_ref, o_ref):
    del tg_ref
    x = x_ref[...]
    xh, xl = _split2(x)
    el = be_ref[0] + _dot(xh, we_ref[0, 1]) + _dot(xl, we_ref[0, 0]) + _dot(xh, we_ref[0, 0])
    lane = lax.broadcasted_iota(jnp.int32, el.shape, 1).astype(F32)
    top1 = jnp.max(el, axis=-1, keepdims=True)
    idx1 = _lane_min_index(el == top1, lane)
    el2 = jnp.where(lane == idx1, -jnp.inf, el)
    top2 = jnp.max(el2, axis=-1, keepdims=True)
    idx2 = _lane_min_index(el2 == top2, lane)
    e2 = jnp.exp(top2 - top1)
    s1 = 1.0 / (1.0 + e2)
    s2 = e2 / (1.0 + e2)
    hids = []
    for e in range(EXPERTS_PER_GROUP):
        ge = jnp.where(idx1 == float(e), s1, jnp.where(idx2 == float(e), s2, 0.0))
        h1 = _dot(xh, w1_ref[0, e])
        h3 = _dot(xh, w3_ref[0, e])
        hids.append(((h1 * jax.nn.sigmoid(h1)) * h3 * ge).astype(BF16))
    o_ref[...] = _dot(jnp.concatenate(hids, axis=1), w2_ref[0])


def _moe_group_ffn(tile_group, xs, we2, be, w1, w3, w2, tm):
    rows, c = xs.shape
    g, e, _, f = w1.shape
    grid_spec = pltpu.PrefetchScalarGridSpec(
        num_scalar_prefetch=1,
        grid=(rows // tm,),
        in_specs=[pl.BlockSpec((tm, c), lambda i, tg: (i, 0)),
                  pl.BlockSpec((1, 2, c, LANES), lambda i, tg: (tg[i], 0, 0, 0)),
                  pl.BlockSpec((1, 1, LANES), lambda i, tg: (tg[i], 0, 0)),
                  pl.BlockSpec((1, e, c, f), lambda i, tg: (tg[i], 0, 0, 0)),
                  pl.BlockSpec((1, e, c, f), lambda i, tg: (tg[i], 0, 0, 0)),
                  pl.BlockSpec((1, e * f, c), lambda i, tg: (tg[i], 0, 0))],
        out_specs=pl.BlockSpec((tm, c), lambda i, tg: (i, 0)),
    )
    return pl.pallas_call(
        _moe_group_ffn_kernel,
        grid_spec=grid_spec,
        out_shape=jax.ShapeDtypeStruct((rows, c), F32),
        compiler_params=_cparams("arbitrary"),
        name="moe_group_ffn",
    )(tile_group, xs, we2, be, w1, w3, w2)


def _moe_gather_kernel(dest_ref, nxt_ref, h_ref, pg_ref, gf_ref, ys_ref, o_ref, buf_ref, sem_ref, *, final_norm):
    i = pl.program_id(0)
    n = pl.num_programs(0)
    tm = h_ref.shape[0]
    slot = lax.rem(i, 2)

    def row_copy(s, r, d):
        return pltpu.make_async_copy(ys_ref.at[pl.ds(d, 1), :], buf_ref.at[s, pl.ds(r, 1), :], sem_ref.at[s])

    def start_all(s, idx_ref):
        for r in range(tm):
            row_copy(s, r, idx_ref[0, 0, r]).start(priority=r % 2)

    @pl.when(i == 0)
    def _():
        start_all(0, dest_ref)

    @pl.when(i + 1 < n)
    def _():
        start_all(1 - slot, nxt_ref)

    pltpu.make_async_copy(ys_ref.at[pl.ds(0, tm), :], buf_ref.at[slot], sem_ref.at[slot]).wait()

    y = buf_ref[slot]
    pg = pg_ref[...]
    c = y.shape[1]
    out = h_ref[...] + y * jnp.concatenate([pg] * (c // LANES), axis=1)
    if final_norm:
        out = _rms(out, gf_ref[...])
    o_ref[...] = out


def _moe_gather(h, pg, dest3, ys, g_final, final_norm, tm):
    t, c = h.shape
    n = t // tm
    return pl.pallas_call(
        functools.partial(_moe_gather_kernel, final_norm=final_norm),
        grid=(n,),
        in_specs=[pl.BlockSpec((1, 1, tm), lambda i: (i, 0, 0), memory_space=pltpu.SMEM),
                  pl.BlockSpec((1, 1, tm), lambda i: (jnp.minimum(i + 1, n - 1), 0, 0), memory_space=pltpu.SMEM),
                  pl.BlockSpec((tm, c), lambda i: (i, 0)),
                  pl.BlockSpec((tm, LANES), lambda i: (i, 0)),
                  pl.BlockSpec(g_final.shape, lambda i: (0, 0)),
                  pl.BlockSpec(memory_space=pl.ANY)],
        out_specs=pl.BlockSpec((tm, c), lambda i: (i, 0)),
        out_shape=jax.ShapeDtypeStruct((t, c), F32),
        scratch_shapes=[pltpu.VMEM((2, tm, c), F32), pltpu.SemaphoreType.DMA((2,))],
        compiler_params=_cparams("arbitrary"),
        name="moe_gather",
    )(dest3, dest3, h, pg, g_final, ys)


def _moe_layer(h, gn, wg, bg, we, be, w1, w3, w2, g_final, final_norm):
    t, c = h.shape
    g, e = N_GROUPS, EXPERTS_PER_GROUP
    f = w1.shape[-1]
    tm_r = MOE_ROUTE_TILE if t % MOE_ROUTE_TILE == 0 else t
    tm = MOE_ROW_TILE if t % MOE_ROW_TILE == 0 else t
    neg = -1e30
    wg128 = jnp.zeros((c, LANES), F32).at[:, :g].set(wg)
    bg128 = jnp.full((1, LANES), neg, F32).at[0, :g].set(bg)
    jj = jnp.arange(tm_r)
    tri = (jj[:, None] > jj[None, :]).astype(BF16)
    dest_local, pg, counts = _moe_route(h, gn, wg128, bg128, tri, tm_r)

    cnt = counts[0, :g].astype(jnp.int32)
    tiles = (cnt + tm - 1) // tm
    tile_end = jnp.cumsum(tiles)
    offs = (tile_end - tiles) * tm
    n_tiles = t // tm + g
    dl = dest_local.reshape(t)
    grp = dl // t
    dest = (dl - grp * t + offs[grp]).reshape(t // tm, 1, tm)
    tile_group = jnp.minimum(jnp.sum(jnp.arange(n_tiles)[:, None] >= tile_end[None, :], axis=1), g - 1).astype(jnp.int32)

    zero_tiles = jnp.concatenate([jnp.maximum(tile_end - 1, 0),
                                  jnp.minimum(tile_end[g - 1] + jnp.arange(g), n_tiles - 1)]).astype(jnp.int32)
    xs = _moe_scatter(zero_tiles, h, gn, dest, n_tiles * tm, tm)
    we_g = jnp.zeros((g, c, LANES), F32).at[:, :, :e].set(we.reshape(c, g, e).transpose(1, 0, 2))
    we2 = jnp.stack(_split2(we_g), axis=1)
    be_g = jnp.full((g, 1, LANES), neg, F32).at[:, 0, :e].set(be.reshape(g, e))
    ys = _moe_group_ffn(tile_group, xs, we2, be_g, w1.astype(BF16), w3.astype(BF16),
                        w2.astype(BF16).reshape(g, e * f, c), tm)
    return _moe_gather(h, pg, dest, ys, g_final, final_norm, tm)


def _block_ones(n, blk):
    i = jnp.arange(n)
    return ((i[:, None] // blk) == (i[None, :] // blk)).astype(BF16)


def _tile(n, pref):
    return pref if n % pref == 0 else n


def kernel(x, mem, norm_mix, norm_cross, norm_ffn, norm_mem, norm_final, rw_mix, rw_wr, rw_wk, rw_wv, rw_w0, rw_w1, rw_w2, rw_a0, rw_a1, rw_a2, rw_g1, rw_g2, rw_kk, rw_ka, rw_rk, rw_lnx_w, rw_lnx_b, rw_wo, sb_wqkv, sb_wo, xa_wq, xa_wkv, xa_wo, moe_wg, moe_bg, moe_we, moe_be, moe_w1, moe_w3, moe_w2):
    b, s, c = x.shape
    t = b * s
    depth = norm_mix.shape[0]
    row = lambda vec: vec.reshape(1, -1).astype(F32)
    bf = lambda w: w.astype(BF16)

    ones_quad = _block_ones(MXU_DIM, HEAD_DIM)
    ones_pair = _block_ones(LANES, HEAD_DIM)
    tri_incl = jnp.tile((jnp.arange(CHUNK)[:, None] >= jnp.arange(CHUNK)[None, :]).astype(BF16), (1, 3))
    jj = jnp.arange(SB_BLOCK)
    sb_cs = jnp.concatenate([(jj[:, None] > jj[None, :]).astype(BF16),
                             jnp.ones((SB_BLOCK, SB_BLOCK), BF16)], axis=1)

    memkv = _norm_matmul(mem.reshape(b * N_MEM, c), row(norm_mem), bf(xa_wkv), BF16,
                         _tile(b * N_MEM, 512), _tile(2 * c, 1024)).reshape(b, N_MEM, 2 * c)

    h = x
    for i in range(depth):
        j = i // 2
        if i % 2 == 0:
            r, lw, k, v, kkn, bb, g = _rwkv_proj(
                h, row(norm_mix[i]), rw_mix[j].astype(F32), bf(rw_wr[j]), bf(rw_wk[j]), bf(rw_wv[j]),
                bf(rw_w1[j]), bf(rw_w2[j]), bf(rw_a1[j]), bf(rw_a2[j]), bf(rw_g1[j]), bf(rw_g2[j]),
                row(rw_w0[j]), row(rw_a0[j]), row(rw_kk[j]), row(rw_ka[j]), ones_quad, _tile(s, 512))
            y = _rwkv_recurrence(r, lw, k, v, kkn, bb, g, row(rw_rk[j]), row(rw_lnx_w[j]),
                                 row(rw_lnx_b[j]), tri_incl, ones_pair, _tile(s, 256), 8 * LANES)
            w_mix_out = bf(rw_wo[j])
        else:
            qkv = _norm_matmul(h.reshape(t, c), row(norm_mix[i]), bf(sb_wqkv[j]), BF16,
                               _tile(t, 1024), 3 * c).reshape(b, s, 3 * c)
            y = _sb_attention(qkv, sb_cs)
            w_mix_out = bf(sb_wo[j])
        h = _cross_block(y, w_mix_out, h, row(norm_cross[i]), bf(xa_wq[i]), memkv, bf(xa_wo[i]), _tile(s, 512))

        h = _moe_layer(h.reshape(t, c), row(norm_ffn[i]), moe_wg[i], moe_bg[i], moe_we[i], moe_be[i],
                       moe_w1[i], moe_w3[i], moe_w2[i], row(norm_final), i == depth - 1).reshape(b, s, c)
    return h
```

```python
import functools

import jax
import jax.numpy as jnp
from jax import lax
from jax.experimental import pallas as pl
from jax.experimental.pallas import tpu as pltpu

F32 = jnp.float32
BF16 = jnp.bfloat16

HEAD_DIM = 64
N_MEM = 256
X_HEADS = 4
N_GROUPS = 4
EXPERTS_PER_GROUP = 8
N_EXPERTS = N_GROUPS * EXPERTS_PER_GROUP
D_EXPERT = 256
GN_EPS = 64e-5
NORM_EPS = 1e-6
SB_BLOCK = 128
CHUNK = 64

LANES = 128
MXU_DIM = 256
VMEM_LIMIT = 48 * 1024 * 1024

ROUTER_LANES = LANES


def _cparams(*sem):
    return pltpu.CompilerParams(dimension_semantics=sem, vmem_limit_bytes=VMEM_LIMIT)


def _dot(a, b):
    return jnp.dot(a, b, preferred_element_type=F32)


def _dot_nt(a, b):
    return lax.dot_general(a, b, (((1,), (1,)), ((), ())), preferred_element_type=F32)


def _split2(x):
    hi = x.astype(BF16)
    lo = (x - hi.astype(F32)).astype(BF16)
    return hi, lo


def _split3(x):
    hi = x.astype(BF16)
    r1 = x - hi.astype(F32)
    mid = r1.astype(BF16)
    lo = (r1 - mid.astype(F32)).astype(BF16)
    return hi, mid, lo


def _dot_exact_rhs(x, m_bf16, parts):
    pieces = _split3(x) if parts == 3 else _split2(x)
    if parts * x.shape[1] <= MXU_DIM:
        return _dot(jnp.concatenate(pieces, axis=1), jnp.concatenate([m_bf16] * parts, axis=0))
    acc = _dot(pieces[0], m_bf16)
    for p in pieces[1:]:
        acc = acc + _dot(p, m_bf16)
    return acc


def _dot_exact_lhs3(m3_bf16, x):
    return _dot(m3_bf16, jnp.concatenate(_split3(x), axis=0))


def _rms(x, g):
    ms = jnp.mean(x * x, axis=-1, keepdims=True)
    return x * lax.rsqrt(ms + NORM_EPS) * g


def _head_sum(x, ones_bd):
    c = x.shape[-1]
    outs = []
    for j in range(c // MXU_DIM):
        outs.append(_dot_exact_rhs(x[:, j * MXU_DIM:(j + 1) * MXU_DIM], ones_bd, 2))
    return jnp.concatenate(outs, axis=-1) if len(outs) > 1 else outs[0]


def _norm_mm_kernel(x_ref, g_ref, w_ref, o_ref, xn_ref):
    @pl.when(pl.program_id(1) == 0)
    def _():
        xn_ref[...] = _rms(x_ref[...], g_ref[...]).astype(BF16)

    o_ref[...] = _dot(xn_ref[...], w_ref[...]).astype(o_ref.dtype)


def _norm_matmul(x, g, w, out_dtype, tm, tn):
    m, c = x.shape
    n = w.shape[1]
    return pl.pallas_call(
        _norm_mm_kernel,
        grid=(m // tm, n // tn),
        in_specs=[
            pl.BlockSpec((tm, c), lambda i, j: (i, 0)),
            pl.BlockSpec((1, c), lambda i, j: (0, 0)),
            pl.BlockSpec((c, tn), lambda i, j: (0, j)),
        ],
        out_specs=pl.BlockSpec((tm, tn), lambda i, j: (i, j)),
        out_shape=jax.ShapeDtypeStruct((m, n), out_dtype),
        scratch_shapes=[pltpu.VMEM((tm, c), BF16)],
        compiler_params=_cparams("parallel", "arbitrary"),
        name="norm_matmul",
    )(x, g, w)


PROJ_SUBTILE = 256


def _softplus(x):
    return jnp.maximum(x, 0.0) + jnp.log(1.0 + jnp.exp(-jnp.abs(x)))


def _rwkv_proj_kernel(h_ref, gn_ref, mix_ref, wr_ref, wk_ref, wv_ref, w1_ref, w2_ref, a1_ref, a2_ref,
                      g1_ref, g2_ref, w0_ref, a0_ref, kk_ref, ka_ref, ones_ref,
                      r_out, lw_out, k_out, v_out, kkn_out, b_out, g_out, carry_ref):
    ts = h_ref.shape[1]
    hn = _rms(h_ref[0], gn_ref[...])

    @pl.when(pl.program_id(1) == 0)
    def _():
        carry_ref[...] = jnp.zeros_like(carry_ref)

    prev_last = carry_ref[0:1, :]
    row = lax.broadcasted_iota(jnp.int32, hn.shape, 0)
    hp = jnp.where(row == 0, prev_last, pltpu.roll(hn, 1, 0))
    carry_ref[0:1, :] = hn[ts - 1:ts, :]
    dx = hp - hn

    def project(rows):
        hn_s, dx_s = hn[rows], dx[rows]
        mixed = lambda i: (hn_s + dx_s * mix_ref[i:i + 1, :]).astype(BF16)
        r = _dot(mixed(0), wr_ref[...])
        lora_w = _dot(jnp.tanh(_dot(mixed(1), w1_ref[...])).astype(BF16), w2_ref[...])
        k = _dot(mixed(2), wk_ref[...])
        v = _dot(mixed(3), wv_ref[...])
        lora_a = _dot(_dot(mixed(4), a1_ref[...]).astype(BF16), a2_ref[...])
        g = _dot(jax.nn.sigmoid(_dot(mixed(5), g1_ref[...])).astype(BF16), g2_ref[...])
        return r, lora_w, k, v, lora_a, g

    def finish(rows, r, lora_w, k, v, lora_a, g):
        w_log = -_softplus(-(w0_ref[...] + lora_w)) - 0.5
        a = jax.nn.sigmoid(a0_ref[...] + lora_a)
        kk = k * kk_ref[...]
        nrm = jnp.sqrt(_head_sum(kk * kk, ones_ref[...]))
        kkn = kk / jnp.maximum(nrm, 1e-12)
        r_out[0, rows] = r.astype(r_out.dtype)
        lw_out[0, rows] = -jnp.exp(w_log)
        k_out[0, rows] = (k * (1.0 + (a - 1.0) * ka_ref[...])).astype(k_out.dtype)
        v_out[0, rows] = v.astype(v_out.dtype)
        kkn_out[0, rows] = kkn.astype(kkn_out.dtype)
        b_out[0, rows] = (kkn * a).astype(b_out.dtype)
        g_out[0, rows] = g.astype(g_out.dtype)

    sub = min(PROJ_SUBTILE, ts)
    pending = None
    for i in range(ts // sub + 1):
        rows = slice(i * sub, (i + 1) * sub)
        cur = (rows,) + project(rows) if i < ts // sub else None
        if pending is not None:
            finish(*pending)
        pending = cur


def _rwkv_proj(h, gn, mix, wr, wk, wv, w1, w2, a1, a2, g1, g2, w0, a0, k_k, k_a, ones_bd, ts):
    b, s, c = h.shape
    full = lambda arr: pl.BlockSpec(arr.shape, lambda i, j: (0,) * arr.ndim)
    tok = pl.BlockSpec((1, ts, c), lambda i, j: (i, j, 0))
    params = (gn, mix, wr, wk, wv, w1, w2, a1, a2, g1, g2, w0, a0, k_k, k_a, ones_bd)
    return pl.pallas_call(
        _rwkv_proj_kernel,
        grid=(b, s // ts),
        in_specs=[tok] + [full(p) for p in params],
        out_specs=[tok] * 7,
        out_shape=[jax.ShapeDtypeStruct((b, s, c), F32 if n == 1 else BF16) for n in range(7)],
        scratch_shapes=[pltpu.VMEM((8, c), F32)],
        compiler_params=_cparams("parallel", "arbitrary"),
        name="rwkv_proj",
    )(h, *params)


def _rwkv_rec_kernel(r_ref, lw_ref, k_ref, v_ref, kk_ref, b_ref, g_ref, rk_ref, lnw_ref, lnb_ref,
                     tri_ref, ones_ref, o_ref,
                     state_ref, xp_ref, vp_ref, y0_ref, bdt_ref, vst_ref, dl_ref, bon_ref):
    ts, lw_lanes = r_ref.shape[1], r_ref.shape[2]
    n_chunks = ts // CHUNK
    n_groups = lw_lanes // LANES
    L = CHUNK
    prep_chunks = 2 if n_chunks % 2 == 0 else 1

    @pl.when(pl.program_id(2) == 0)
    def _():
        state_ref[...] = jnp.zeros_like(state_ref)

    lane = lax.broadcasted_iota(jnp.int32, (L, LANES), 1)
    head0 = lane < HEAD_DIM
    rr = lax.broadcasted_iota(jnp.int32, (2 * L, 2 * L), 0)
    cc = lax.broadcasted_iota(jnp.int32, (2 * L, 2 * L), 1)
    same_head = (rr // L) == (cc // L)
    strict = same_head & ((cc % L) < (rr % L))
    incl = same_head & ((cc % L) <= (rr % L))
    eye = (rr == cc).astype(F32)
    first_head = rr < L
    tri_incl = tri_ref[...]
    ones_bd = ones_ref[...]

    def stack_heads(x):
        return jnp.concatenate([jnp.where(head0, x, 0.0), jnp.where(head0, 0.0, x)], axis=0)

    def unstack(x):
        return x[:L] + x[L:]

    def prepare(ci, tick):
        chains = [(ci * prep_chunks + cc, gi) for cc in range(prep_chunks) for gi in range(n_groups)]
        rows = [pl.ds(pl.multiple_of(c * L, L), L) for c, _ in chains]
        lanes = [slice(gi * LANES, (gi + 1) * LANES) for _, gi in chains]
        idxs = [c * n_groups + gi for c, gi in chains]
        each = lambda f, *cols: [f(*a) for a in zip(*cols)]
        load = lambda ref: each(lambda rw, ls: ref[0, rw, ls].astype(F32), rows, lanes)
        r, lw, k, v, kk, bb = (load(ref) for ref in (r_ref, lw_ref, k_ref, v_ref, kk_ref, b_ref))

        cum = each(lambda x: _dot_exact_lhs3(tri_incl, x), lw)
        d_inv = each(lambda c_: jnp.exp(-c_), cum)
        d_last = each(lambda c_: jnp.exp(c_[L - 1:L, :]), cum)
        xa = each(lambda c_, l_, kk_: stack_heads(-(jnp.exp(c_ - l_) * kk_)).astype(BF16), cum, lw, kk)
        xr32 = each(lambda c_, r_: stack_heads(jnp.exp(c_) * r_), cum, r)
        bt32 = each(lambda b_, d_: b_ * d_, bb, d_inv)
        kt32 = each(lambda k_, d_: k_ * d_, k, d_inv)

        def gram(xa_, xr_, bt_, kt_):
            z2 = jnp.concatenate([bt_.astype(BF16), kt_.astype(BF16)], axis=0)
            x_all = jnp.concatenate([xa_, xr_.astype(BF16)], axis=0)
            return _dot_nt(x_all, z2)

        gmat = each(gram, xa, xr32, bt32, kt32)
        tick()
        swap = each(lambda g_: pltpu.roll(g_, HEAD_DIM, 1), gmat)
        m_ab = each(lambda g_, w_: jnp.where(strict, jnp.where(first_head, g_[:2 * L], w_[:2 * L]), 0.0), gmat, swap)
        m_ak = each(lambda g_, w_: jnp.where(strict, jnp.where(first_head, w_[:2 * L], g_[:2 * L]), 0.0).astype(BF16),
                    gmat, swap)
        m_rb = each(lambda g_, w_: jnp.where(incl, jnp.where(first_head, g_[2 * L:], w_[2 * L:]), 0.0).astype(BF16),
                    gmat, swap)
        m_rk = each(lambda g_, w_: jnp.where(incl, jnp.where(first_head, w_[2 * L:], g_[2 * L:]), 0.0).astype(BF16),
                    gmat, swap)

        p = m_ab
        t_inv = each(lambda m_: eye + m_, m_ab)
        for _ in range(5):
            p = each(lambda p_: _dot(p_.astype(BF16), p_.astype(BF16)), p)
            t_inv = each(lambda t_, p_: t_ + _dot(t_.astype(BF16), p_.astype(BF16)), t_inv, p)
            tick()
        t16 = each(lambda t_: t_.astype(BF16), t_inv)

        v_st = each(lambda v_: stack_heads(v_).astype(BF16), v)
        mv = each(_dot, m_ak, v_st)
        tick()
        tav = each(lambda t_, xa_, mv_: _dot(t_, jnp.concatenate([xa_, mv_.astype(BF16)], axis=1)),
                   t16, xa, mv)
        tick()
        rby = each(lambda m_, tav_: _dot(m_, tav_.astype(BF16)), m_rb, tav)
        tick()
        y0 = each(lambda rby_, m_, v_: rby_[:, LANES:] + _dot(m_, v_), rby, m_rk, v_st)
        for n, idx in enumerate(idxs):
            xp_ref[idx] = jnp.concatenate([tav[n][:, :LANES], xr32[n] + rby[n][:, :LANES]], axis=0).astype(BF16)
            vp_ref[idx] = tav[n][:, LANES:]
            y0_ref[idx] = y0[n]
            bdt_ref[idx] = jnp.concatenate([stack_heads(bt32[n] * d_last[n]), stack_heads(kt32[n] * d_last[n])],
                                           axis=0).T.astype(BF16)
            vst_ref[idx] = v_st[n]
            dl_ref[idx] = jnp.broadcast_to(d_last[n], (LANES, LANES)).T
            bon_ref[rows[n], lanes[n]] = _dot_exact_rhs(r[n] * k[n] * rk_ref[:, lanes[n]], ones_bd, 2) * v[n]

    def advance_steps(ci):
        rows = pl.ds(pl.multiple_of(ci * L, L), L)
        groups = list(range(n_groups))
        lanes = [slice(gi * LANES, (gi + 1) * LANES) for gi in groups]
        idxs = [ci * n_groups + gi for gi in groups]
        each = lambda f, *cols: [f(*a) for a in zip(*cols)]
        s_kv = each(lambda gi: state_ref[gi], groups)
        xs = each(lambda i_, s_: _dot(xp_ref[i_], s_.astype(BF16)), idxs, s_kv)
        yield
        u16 = each(lambda x_, i_: (x_[:2 * L] + vp_ref[i_]).astype(BF16), xs, idxs)
        upd = each(lambda i_, u_: _dot(bdt_ref[i_], jnp.concatenate([u_, vst_ref[i_]], axis=0)), idxs, u16)
        for gi in groups:
            state_ref[gi] = s_kv[gi] * dl_ref[idxs[gi]] + upd[gi]
        yield

        inv_n = 1.0 / HEAD_DIM
        y = each(lambda x_, i_: unstack(x_[2 * L:] + y0_ref[i_]), xs, idxs)
        yc = each(lambda y_: y_ - _dot_exact_rhs(y_, ones_bd, 2) * inv_n, y)
        yield
        var = each(lambda yc_: _dot_exact_rhs(yc_ * yc_, ones_bd, 2) * inv_n, yc)
        for gi in groups:
            ls = lanes[gi]
            yn = yc[gi] * lax.rsqrt(var[gi] + GN_EPS) * lnw_ref[:, ls] + lnb_ref[:, ls]
            o_ref[0, rows, ls] = ((yn + bon_ref[rows, ls]) * g_ref[0, rows, ls].astype(F32)).astype(o_ref.dtype)
        yield

    n_prep = n_chunks // prep_chunks

    def advance_pair(pi):
        for cc in range(prep_chunks):
            yield from advance_steps(pi * prep_chunks + cc)

    def prepare_with(pi, side):
        prepare(pi, (lambda: next(side, None)) if side is not None else (lambda: None))
        if side is not None:
            for _ in side:
                pass

    prepare_with(jnp.int32(0), None)

    def overlapped(pi, carry):
        prepare_with(pi, advance_pair(pi - 1))
        return carry

    lax.fori_loop(1, n_prep, overlapped, 0)
    for _ in advance_pair(jnp.int32(n_prep - 1)):
        pass


def _rwkv_recurrence(r, lw, k, v, kkn, bb, g, r_k, lnx_w, lnx_b, tri, ones_pair, ts, lane_w):
    b, s, c = r.shape
    tok = pl.BlockSpec((1, ts, lane_w), lambda i, j, t: (i, t, j))
    vec = pl.BlockSpec((1, lane_w), lambda i, j, t: (0, j))
    small = lambda arr: pl.BlockSpec(arr.shape, lambda i, j, t: (0,) * arr.ndim)
    n_groups = lane_w // LANES
    ncg = (ts // CHUNK) * n_groups
    return pl.pallas_call(
        _rwkv_rec_kernel,
        grid=(b, c // lane_w, s // ts),
        in_specs=[tok] * 7 + [vec, vec, vec, small(tri), small(ones_pair)],
        out_specs=tok,
        out_shape=jax.ShapeDtypeStruct((b, s, c), BF16),
        scratch_shapes=[pltpu.VMEM((n_groups, LANES, LANES), F32),
                        pltpu.VMEM((ncg, 4 * CHUNK, LANES), BF16),
                        pltpu.VMEM((ncg, 2 * CHUNK, LANES), F32),
                        pltpu.VMEM((ncg, 2 * CHUNK, LANES), F32),
                        pltpu.VMEM((ncg, LANES, 4 * CHUNK), BF16),
                        pltpu.VMEM((ncg, 2 * CHUNK, LANES), BF16),
                        pltpu.VMEM((ncg, LANES, LANES), F32),
                        pltpu.VMEM((ts, lane_w), F32)],
        compiler_params=_cparams("parallel", "parallel", "arbitrary"),
        name="rwkv_recurrence",
    )(r, lw, k, v, kkn, bb, g, r_k, lnx_w, lnx_b, tri, ones_pair)


SB_QTILE = 512
SB_SUBTILE = 128
SB_STAGE_LAG = 3
SB_SKIP_BELOW = -104.0
SB_PAIRS = 4


def _sb_attn_kernel(q_ref, k_ref, v_ref, cs_ref, o_ref, qs_ref, acc_ref, run_ref, tmax_ref):
    s = q_ref.shape[1]
    tb = SB_BLOCK
    tq = min(SB_QTILE, s)
    rs = SB_SUBTILE
    sub_per_block = tb // rs
    bpt = tq // tb
    kpb = 2 if bpt % 2 == 0 else 1
    nb, nt = s // tb, s // tq
    shift = bpt.bit_length() - 1
    lane = lax.broadcasted_iota(jnp.int32, (s, LANES), 1)
    head0 = lane < HEAD_DIM
    n_heads = qs_ref.shape[0]
    for pp in range(n_heads // 2):
        q = q_ref[0, :, pp * LANES:(pp + 1) * LANES] * (HEAD_DIM ** -0.5)
        qs_ref[2 * pp] = jnp.where(head0, q, jnp.zeros_like(q))
        qs_ref[2 * pp + 1] = jnp.where(head0, jnp.zeros_like(q), q)
    for qb in range(nb):
        tmax_ref[qb] = 0.0
    cs = cs_ref[...]
    cs2 = jnp.concatenate([cs, cs], axis=0)

    def tile_step(ti, blocks, pieces):
        r0 = ti * tq
        subs = [(kbi, h, pl.ds(pl.multiple_of(r0 + i * rs, rs), rs), r0 + i * rs, diag, i // sub_per_block)
                for kbi, i, diag in pieces for h in range(n_heads)]
        n = len(subs)
        lag = SB_STAGE_LAG
        z, lb, parts, sums, att, before = ({} for _ in range(6))
        top = {}
        for step in range(n + 2 + 2 * lag):
            if step < n:
                kbi, h, rows = subs[step][:3]
                z[step] = _dot_nt(qs_ref[h, rows, :], blocks[kbi][1][h // 2])
            i = step - 1
            if 0 <= i < n:
                zz = z.pop(i)
                sp = jnp.log(1.0 + jnp.exp(-jnp.abs(zz)))
                lb[i] = jnp.minimum(zz, 0.0) - sp
                log_fail = lb[i] - zz
                if subs[i][4]:
                    t_idx = subs[i][3] + lax.broadcasted_iota(jnp.int32, (rs, tb), 0)
                    s_idx = blocks[subs[i][0]][0] + lax.broadcasted_iota(jnp.int32, (rs, tb), 1)
                    before[i] = s_idx < t_idx
                    log_fail = jnp.where(before[i], log_fail, 0.0)
                parts[i] = _split2(log_fail)
            i = step - 1 - lag
            if 0 <= i < n:
                hi, lo = parts.pop(i)
                sums[i] = _dot(jnp.concatenate([hi, lo], axis=1), cs2)
            i = step - 2 - lag
            if 0 <= i < n:
                kbi, h, rows, _, diag, _ = subs[i]
                sm = sums.pop(i)
                if diag:
                    a = jnp.where(before.pop(i), jnp.exp(lb.pop(i) + sm[:, :tb]), 0.0)
                    run = sm[:, tb:]
                else:
                    run = run_ref[h, rows, :]
                    a = jnp.exp(lb.pop(i) + run + sm[:, :tb])
                    run = run + sm[:, tb:]
                att[i] = a.astype(BF16)
                run_ref[h, rows, :] = run
                if kbi == len(blocks) - 1:
                    qb = subs[i][5]
                    top[qb] = run if qb not in top else jnp.maximum(top[qb], run)
            i = step - 2 - 2 * lag
            if 0 <= i < n:
                kbi, h, rows = subs[i][:3]
                pv = _dot(att.pop(i), blocks[kbi][2][h // 2])
                if subs[i][4]:
                    acc_ref[h, rows, :] = pv
                else:
                    acc_ref[h, rows, :] += pv
        for qb, t in top.items():
            tmax_ref[ti * bpt + qb] = jnp.max(t)

    def below_pieces(n_q):
        return [(kbi, i, False) for kbi in range(kpb) for i in range(n_q * sub_per_block)]

    def diagonal_pieces(u, q_last):
        jl = kpb * u + kpb - 1
        out = []
        for kbi in range(kpb):
            kl = jl - kbi
            for qb in range(kl, q_last + 1):
                for i in range(qb * sub_per_block, (qb + 1) * sub_per_block):
                    out.append((kbi, i, qb == kl))
        return out

    def last_alive(ti, first):
        n = jnp.int32(first)
        for qb in range(first, bpt):
            n = jnp.where(tmax_ref[ti * bpt + qb] >= SB_SKIP_BELOW, qb + 1, n)
        return n

    def key_blocks(jj, carry):
        j = nb - 1 - kpb * jj
        blocks = []
        for d in range(kpb):
            k0 = pl.multiple_of((j - d) * tb, tb)
            lanes = [slice(pp * LANES, (pp + 1) * LANES) for pp in range(n_heads // 2)]
            blocks.append((k0, [k_ref[0, pl.ds(k0, tb), ls] for ls in lanes],
                           [v_ref[0, pl.ds(k0, tb), ls] for ls in lanes]))
        t0 = lax.shift_right_logical(j, shift)
        case = lax.shift_right_logical(j - t0 * bpt, kpb.bit_length() - 1)
        for u in range(bpt // kpb):
            jl = kpb * u + kpb - 1
            n_q = last_alive(t0, jl + 1)
            for q_last in range(jl, bpt):
                @pl.when((case == u) & (n_q == q_last + 1))
                def _():
                    tile_step(t0, blocks, diagonal_pieces(u, q_last))

        def below(ti, c):
            n_q = last_alive(ti, 0)
            for m in range(1, bpt + 1):
                @pl.when(n_q == m)
                def _():
                    tile_step(ti, blocks, below_pieces(m))
            return c

        lax.fori_loop(t0 + 1, nt, below, 0)
        return carry

    lax.fori_loop(0, nb // kpb, key_blocks, 0)
    for pp in range(n_heads // 2):
        o_ref[0, :, pp * LANES:(pp + 1) * LANES] = jnp.where(head0, acc_ref[2 * pp], acc_ref[2 * pp + 1]).astype(o_ref.dtype)


def _sb_attention(qkv, cs):
    b, s, c3 = qkv.shape
    c = c3 // 3
    w = LANES * SB_PAIRS
    n_blk = c // w
    n_heads = 2 * SB_PAIRS
    return pl.pallas_call(
        _sb_attn_kernel,
        grid=(b, n_blk),
        in_specs=[
            pl.BlockSpec((1, s, w), lambda i, p: (i, 0, p)),
            pl.BlockSpec((1, s, w), lambda i, p: (i, 0, n_blk + p)),
            pl.BlockSpec((1, s, w), lambda i, p: (i, 0, 2 * n_blk + p)),
            pl.BlockSpec(cs.shape, lambda i, p: (0, 0)),
        ],
        out_specs=pl.BlockSpec((1, s, w), lambda i, p: (i, 0, p)),
        out_shape=jax.ShapeDtypeStruct((b, s, c), BF16),
        scratch_shapes=[pltpu.VMEM((n_heads, s, LANES), BF16), pltpu.VMEM((n_heads, s, LANES), F32),
                        pltpu.VMEM((n_heads, s, SB_BLOCK), F32),
                        pltpu.SMEM((s // SB_BLOCK,), F32)],
        compiler_params=_cparams("parallel", "parallel"),
        name="sb_attention",
    )(qkv, qkv, qkv, cs)


def _cross_kernel(y_ref, wy_ref, h_ref, gn_ref, wq_ref, kv_ref, wo_ref, o_ref):
    h = h_ref[0] + _dot(y_ref[0], wy_ref[...])
    c = h.shape[-1]
    xhd = c // X_HEADS
    q = _dot(_rms(h, gn_ref[...]).astype(BF16), wq_ref[...]).astype(BF16)
    outs = []
    for hd in range(X_HEADS):
        qh = q[:, hd * xhd:(hd + 1) * xhd]
        kh = kv_ref[0, :, hd * xhd:(hd + 1) * xhd]
        vh = kv_ref[0, :, c + hd * xhd:c + (hd + 1) * xhd]
        sc = _dot_nt(qh, kh) * (xhd ** -0.5)
        sc = sc - jnp.max(sc, axis=-1, keepdims=True)
        e = jnp.exp(sc)
        p = e / jnp.sum(e, axis=-1, keepdims=True)
        outs.append(_dot(p.astype(BF16), vh).astype(BF16))
    o = jnp.concatenate(outs, axis=-1)
    o_ref[0] = h + _dot(o, wo_ref[...])


def _cross_block(y, wy, h, gn, wq, memkv, wo, tq):
    b, s, c = h.shape
    tok = pl.BlockSpec((1, tq, c), lambda i, t: (i, t, 0))
    full = lambda arr: pl.BlockSpec(arr.shape, lambda i, t: (0,) * arr.ndim)
    return pl.pallas_call(
        _cross_kernel,
        grid=(b, s // tq),
        in_specs=[tok, full(wy), tok, full(gn), full(wq),
                  pl.BlockSpec((1,) + memkv.shape[1:], lambda i, t: (i, 0, 0)), full(wo)],
        out_specs=tok,
        out_shape=jax.ShapeDtypeStruct((b, s, c), F32),
        compiler_params=_cparams("parallel", "parallel"),
        name="cross_attention",
    )(y, wy, h, gn, wq, memkv, wo)


def _lane_min_index(mask, lane):
    return jnp.min(jnp.where(mask, lane, float(ROUTER_LANES)), axis=-1, keepdims=True)


MOE_ROUTE_TILE = 512
MOE_ROW_TILE = 256


def _dot_3pass(a, b):
    ah, al = _split2(a)
    bh, bl = _split2(b)
    return _dot(ah, bl) + _dot(al, bh) + _dot(ah, bh)


def _moe_route_kernel(h_ref, gn_ref, wg_ref, bg_ref, tri_ref, dest_ref, pg_ref, cnt_ref, base_ref, *, n_tokens):
    @pl.when(pl.program_id(0) == 0)
    def _():
        base_ref[...] = jnp.zeros_like(base_ref)

    hn = _rms(h_ref[...], gn_ref[...])
    gl = _dot_3pass(hn, wg_ref[...]) + bg_ref[...]
    lane = lax.broadcasted_iota(jnp.int32, gl.shape, 1).astype(F32)
    gmax = jnp.max(gl, axis=-1, keepdims=True)
    group = _lane_min_index(gl == gmax, lane)
    p_group = 1.0 / jnp.sum(jnp.exp(gl - gmax), axis=-1, keepdims=True)
    onehot = (lane == group).astype(F32)
    earlier = _dot(tri_ref[...], onehot.astype(BF16))
    base = base_ref[0:1, :]
    rank = jnp.sum(onehot * (earlier + base), axis=-1, keepdims=True)
    dest = group * float(n_tokens) + rank
    tm = dest.shape[0]
    dest_row = jnp.broadcast_to(dest, (tm, LANES)).T[0:1, :]
    dest_ref[0] = dest_row.astype(jnp.int32)
    pg_ref[...] = jnp.broadcast_to(p_group, pg_ref.shape)
    new_base = base + earlier[tm - 1:tm, :] + onehot[tm - 1:tm, :]
    base_ref[0:1, :] = new_base
    cnt_ref[...] = jnp.broadcast_to(new_base, cnt_ref.shape)


def _moe_route(h, gn, wg128, bg128, tri, tm):
    t, c = h.shape
    full = lambda arr: pl.BlockSpec(arr.shape, lambda i: (0,) * arr.ndim)
    return pl.pallas_call(
        functools.partial(_moe_route_kernel, n_tokens=t),
        grid=(t // tm,),
        in_specs=[pl.BlockSpec((tm, c), lambda i: (i, 0)), full(gn), full(wg128), full(bg128), full(tri)],
        out_specs=[pl.BlockSpec((1, 1, tm), lambda i: (i, 0, 0)),
                   pl.BlockSpec((tm, LANES), lambda i: (i, 0)),
                   pl.BlockSpec((8, LANES), lambda i: (0, 0))],
        out_shape=[jax.ShapeDtypeStruct((t // tm, 1, tm), jnp.int32),
                   jax.ShapeDtypeStruct((t, LANES), F32),
                   jax.ShapeDtypeStruct((8, LANES), F32)],
        scratch_shapes=[pltpu.VMEM((8, LANES), F32)],
        compiler_params=_cparams("arbitrary"),
        name="moe_route",
    )(h, gn, wg128, bg128, tri)


def _moe_scatter_kernel(zt_ref, dest_ref, h_ref, gn_ref, xs_ref, buf_ref, sem_ref, zsem_ref):
    i = pl.program_id(0)
    n = pl.num_programs(0)
    tm = h_ref.shape[0]
    slot = lax.rem(i, 2)

    def row_copy(s, r, d):
        return pltpu.make_async_copy(buf_ref.at[s, pl.ds(r, 1), :], xs_ref.at[pl.ds(d, 1), :], sem_ref.at[s])

    def wait_all(s):
        pltpu.make_async_copy(buf_ref.at[s], xs_ref.at[pl.ds(0, tm), :], sem_ref.at[s]).wait()

    @pl.when(i == 0)
    def _():
        buf_ref[1] = jnp.zeros(buf_ref.shape[1:], buf_ref.dtype)
        fills = [pltpu.make_async_copy(buf_ref.at[1], xs_ref.at[pl.ds(pl.multiple_of(zt_ref[k] * tm, tm), tm), :],
                                       zsem_ref.at[0]) for k in range(zt_ref.shape[0])]
        for f in fills:
            f.start()
            f.wait()

    buf_ref[slot] = _rms(h_ref[...], gn_ref[...])

    for r in range(tm):
        row_copy(slot, r, dest_ref[0, 0, r]).start(priority=r % 2)

    @pl.when(i > 0)
    def _():
        wait_all(1 - slot)

    @pl.when(i == n - 1)
    def _():
        wait_all(slot)


def _moe_scatter(zero_tiles, h, gn, dest3, n_rows, tm):
    t, c = h.shape
    grid_spec = pltpu.PrefetchScalarGridSpec(
        num_scalar_prefetch=1,
        grid=(t // tm,),
        in_specs=[pl.BlockSpec((1, 1, tm), lambda i, zt: (i, 0, 0), memory_space=pltpu.SMEM),
                  pl.BlockSpec((tm, c), lambda i, zt: (i, 0)),
                  pl.BlockSpec(gn.shape, lambda i, zt: (0, 0))],
        out_specs=pl.BlockSpec(memory_space=pl.ANY),
        scratch_shapes=[pltpu.VMEM((2, tm, c), F32), pltpu.SemaphoreType.DMA((2,)),
                        pltpu.SemaphoreType.DMA((1,))],
    )
    return pl.pallas_call(
        _moe_scatter_kernel,
        grid_spec=grid_spec,
        out_shape=jax.ShapeDtypeStruct((n_rows, c), F32),
        compiler_params=_cparams("arbitrary"),
        name="moe_scatter",
    )(zero_tiles, dest3, h, gn)


def _moe_group_ffn_kernel(tg_ref, x_ref, we_ref, be_ref, w1_ref, w3_ref, w2_ref, o_ref):
    del tg_ref
    x = x_ref[...]
    xh, xl = _split2(x)
    el = be_ref[0] + _dot(xh, we_ref[0, 1]) + _dot(xl, we_ref[0, 0]) + _dot(xh, we_ref[0, 0])
    lane = lax.broadcasted_iota(jnp.int32, el.shape, 1).astype(F32)
    top1 = jnp.max(el, axis=-1, keepdims=True)
    idx1 = _lane_min_index(el == top1, lane)
    el2 = jnp.where(lane == idx1, -jnp.inf, el)
    top2 = jnp.max(el2, axis=-1, keepdims=True)
    idx2 = _lane_min_index(el2 == top2, lane)
    e2 = jnp.exp(top2 - top1)
    s1 = 1.0 / (1.0 + e2)
    s2 = e2 / (1.0 + e2)
    hids = []
    for e in range(EXPERTS_PER_GROUP):
        ge = jnp.where(idx1 == float(e), s1, jnp.where(idx2 == float(e), s2, 0.0))
        h1 = _dot(xh, w1_ref[0, e])
        h3 = _dot(xh, w3_ref[0, e])
        hids.append(((h1 * jax.nn.sigmoid(h1)) * h3 * ge).astype(BF16))
    o_ref[...] = _dot(jnp.concatenate(hids, axis=1), w2_ref[0])


def _moe_group_ffn(tile_group, xs, we2, be, w1, w3, w2, tm):
    rows, c = xs.shape
    g, e, _, f = w1.shape
    grid_spec = pltpu.PrefetchScalarGridSpec(
        num_scalar_prefetch=1,
        grid=(rows // tm,),
        in_specs=[pl.BlockSpec((tm, c), lambda i, tg: (i, 0)),
                  pl.BlockSpec((1, 2, c, LANES), lambda i, tg: (tg[i], 0, 0, 0)),
                  pl.BlockSpec((1, 1, LANES), lambda i, tg: (tg[i], 0, 0)),
                  pl.BlockSpec((1, e, c, f), lambda i, tg: (tg[i], 0, 0, 0)),
                  pl.BlockSpec((1, e, c, f), lambda i, tg: (tg[i], 0, 0, 0)),
                  pl.BlockSpec((1, e * f, c), lambda i, tg: (tg[i], 0, 0))],
        out_specs=pl.BlockSpec((tm, c), lambda i, tg: (i, 0)),
    )
    return pl.pallas_call(
        _moe_group_ffn_kernel,
        grid_spec=grid_spec,
        out_shape=jax.ShapeDtypeStruct((rows, c), F32),
        compiler_params=_cparams("arbitrary"),
        name="moe_group_ffn",
    )(tile_group, xs, we2, be, w1, w3, w2)


def _moe_gather_kernel(dest_ref, nxt_ref, h_ref, pg_ref, gf_ref, ys_ref, o_ref, buf_ref, sem_ref, *, final_norm):
    i = pl.program_id(0)
    n = pl.num_programs(0)
    tm = h_ref.shape[0]
    slot = lax.rem(i, 2)

    def row_copy(s, r, d):
        return pltpu.make_async_copy(ys_ref.at[pl.ds(d, 1), :], buf_ref.at[s, pl.ds(r, 1), :], sem_ref.at[s])

    def start_all(s, idx_ref):
        for r in range(tm):
            row_copy(s, r, idx_ref[0, 0, r]).start(priority=r % 2)

    @pl.when(i == 0)
    def _():
        start_all(0, dest_ref)

    @pl.when(i + 1 < n)
    def _():
        start_all(1 - slot, nxt_ref)

    pltpu.make_async_copy(ys_ref.at[pl.ds(0, tm), :], buf_ref.at[slot], sem_ref.at[slot]).wait()

    y = buf_ref[slot]
    pg = pg_ref[...]
    c = y.shape[1]
    out = h_ref[...] + y * jnp.concatenate([pg] * (c // LANES), axis=1)
    if final_norm:
        out = _rms(out, gf_ref[...])
    o_ref[...] = out


def _moe_gather(h, pg, dest3, ys, g_final, final_norm, tm):
    t, c = h.shape
    n = t // tm
    return pl.pallas_call(
        functools.partial(_moe_gather_kernel, final_norm=final_norm),
        grid=(n,),
        in_specs=[pl.BlockSpec((1, 1, tm), lambda i: (i, 0, 0), memory_space=pltpu.SMEM),
                  pl.BlockSpec((1, 1, tm), lambda i: (jnp.minimum(i + 1, n - 1), 0, 0), memory_space=pltpu.SMEM),
                  pl.BlockSpec((tm, c), lambda i: (i, 0)),
                  pl.BlockSpec((tm, LANES), lambda i: (i, 0)),
                  pl.BlockSpec(g_final.shape, lambda i: (0, 0)),
                  pl.BlockSpec(memory_space=pl.ANY)],
        out_specs=pl.BlockSpec((tm, c), lambda i: (i, 0)),
        out_shape=jax.ShapeDtypeStruct((t, c), F32),
        scratch_shapes=[pltpu.VMEM((2, tm, c), F32), pltpu.SemaphoreType.DMA((2,))],
        compiler_params=_cparams("arbitrary"),
        name="moe_gather",
    )(dest3, dest3, h, pg, g_final, ys)


def _moe_layer(h, gn, wg, bg, we, be, w1, w3, w2, g_final, final_norm):
    t, c = h.shape
    g, e = N_GROUPS, EXPERTS_PER_GROUP
    f = w1.shape[-1]
    tm_r = MOE_ROUTE_TILE if t % MOE_ROUTE_TILE == 0 else t
    tm = MOE_ROW_TILE if t % MOE_ROW_TILE == 0 else t
    neg = -1e30
    wg128 = jnp.zeros((c, LANES), F32).at[:, :g].set(wg)
    bg128 = jnp.full((1, LANES), neg, F32).at[0, :g].set(bg)
    jj = jnp.arange(tm_r)
    tri = (jj[:, None] > jj[None, :]).astype(BF16)
    dest_local, pg, counts = _moe_route(h, gn, wg128, bg128, tri, tm_r)

    cnt = counts[0, :g].astype(jnp.int32)
    tiles = (cnt + tm - 1) // tm
    tile_end = jnp.cumsum(tiles)
    offs = (tile_end - tiles) * tm
    n_tiles = t // tm + g
    dl = dest_local.reshape(t)
    grp = dl // t
    dest = (dl - grp * t + offs[grp]).reshape(t // tm, 1, tm)
    tile_group = jnp.minimum(jnp.sum(jnp.arange(n_tiles)[:, None] >= tile_end[None, :], axis=1), g - 1).astype(jnp.int32)

    zero_tiles = jnp.concatenate([jnp.maximum(tile_end - 1, 0),
                                  jnp.minimum(tile_end[g - 1] + jnp.arange(g), n_tiles - 1)]).astype(jnp.int32)
    xs = _moe_scatter(zero_tiles, h, gn, dest, n_tiles * tm, tm)
    we_g = jnp.zeros((g, c, LANES), F32).at[:, :, :e].set(we.reshape(c, g, e).transpose(1, 0, 2))
    we2 = jnp.stack(_split2(we_g), axis=1)
    be_g = jnp.full((g, 1, LANES), neg, F32).at[:, 0, :e].set(be.reshape(g, e))
    ys = _moe_group_ffn(tile_group, xs, we2, be_g, w1.astype(BF16), w3.astype(BF16),
                        w2.astype(BF16).reshape(g, e * f, c), tm)
    return _moe_gather(h, pg, dest, ys, g_final, final_norm, tm)


def _block_ones(n, blk):
    i = jnp.arange(n)
    return ((i[:, None] // blk) == (i[None, :] // blk)).astype(BF16)


def _tile(n, pref):
    return pref if n % pref == 0 else n


def kernel(x, mem, norm_mix, norm_cross, norm_ffn, norm_mem, norm_final, rw_mix, rw_wr, rw_wk, rw_wv, rw_w0, rw_w1, rw_w2, rw_a0, rw_a1, rw_a2, rw_g1, rw_g2, rw_kk, rw_ka, rw_rk, rw_lnx_w, rw_lnx_b, rw_wo, sb_wqkv, sb_wo, xa_wq, xa_wkv, xa_wo, moe_wg, moe_bg, moe_we, moe_be, moe_w1, moe_w3, moe_w2):
    b, s, c = x.shape
    t = b * s
    depth = norm_mix.shape[0]
    row = lambda vec: vec.reshape(1, -1).astype(F32)
    bf = lambda w: w.astype(BF16)

    ones_quad = _block_ones(MXU_DIM, HEAD_DIM)
    ones_pair = _block_ones(LANES, HEAD_DIM)
    tri_incl = jnp.tile((jnp.arange(CHUNK)[:, None] >= jnp.arange(CHUNK)[None, :]).astype(BF16), (1, 3))
    jj = jnp.arange(SB_BLOCK)
    sb_cs = jnp.concatenate([(jj[:, None] > jj[None, :]).astype(BF16),
                             jnp.ones((SB_BLOCK, SB_BLOCK), BF16)], axis=1)

    memkv = _norm_matmul(mem.reshape(b * N_MEM, c), row(norm_mem), bf(xa_wkv), BF16,
                         _tile(b * N_MEM, 512), _tile(2 * c, 1024)).reshape(b, N_MEM, 2 * c)

    h = x
    for i in range(depth):
        j = i // 2
        if i % 2 == 0:
            r, lw, k, v, kkn, bb, g = _rwkv_proj(
                h, row(norm_mix[i]), rw_mix[j].astype(F32), bf(rw_wr[j]), bf(rw_wk[j]), bf(rw_wv[j]),
                bf(rw_w1[j]), bf(rw_w2[j]), bf(rw_a1[j]), bf(rw_a2[j]), bf(rw_g1[j]), bf(rw_g2[j]),
                row(rw_w0[j]), row(rw_a0[j]), row(rw_kk[j]), row(rw_ka[j]), ones_quad, _tile(s, 512))
            y = _rwkv_recurrence(r, lw, k, v, kkn, bb, g, row(rw_rk[j]), row(rw_lnx_w[j]),
                                 row(rw_lnx_b[j]), tri_incl, ones_pair, _tile(s, 256), 8 * LANES)
            w_mix_out = bf(rw_wo[j])
        else:
            qkv = _norm_matmul(h.reshape(t, c), row(norm_mix[i]), bf(sb_wqkv[j]), BF16,
                               _tile(t, 1024), 3 * c).reshape(b, s, 3 * c)
            y = _sb_attention(qkv, sb_cs)
            w_mix_out = bf(sb_wo[j])
        h = _cross_block(y, w_mix_out, h, row(norm_cross[i]), bf(xa_wq[i]), memkv, bf(xa_wo[i]), _tile(s, 1024))

        h = _moe_layer(h.reshape(t, c), row(norm_ffn[i]), moe_wg[i], moe_bg[i], moe_we[i], moe_be[i],
                       moe_w1[i], moe_w3[i], moe_w2[i], row(norm_final), i == depth - 1).reshape(b, s, c)
    return h
```

```python
import functools

import jax
import jax.numpy as jnp
from jax import lax
from jax.experimental import pallas as pl
from jax.experimental.pallas import tpu as pltpu

F32 = jnp.float32
BF16 = jnp.bfloat16

HEAD_DIM = 64
N_MEM = 256
X_HEADS = 4
N_GROUPS = 4
EXPERTS_PER_GROUP = 8
N_EXPERTS = N_GROUPS * EXPERTS_PER_GROUP
D_EXPERT = 256
GN_EPS = 64e-5
NORM_EPS = 1e-6
SB_BLOCK = 128
CHUNK = 64

LANES = 128
MXU_DIM = 256
VMEM_LIMIT = 48 * 1024 * 1024

ROUTER_LANES = LANES


def _cparams(*sem):
    return pltpu.CompilerParams(dimension_semantics=sem, vmem_limit_bytes=VMEM_LIMIT)


def _dot(a, b):
    return jnp.dot(a, b, preferred_element_type=F32)


def _dot_nt(a, b):
    return lax.dot_general(a, b, (((1,), (1,)), ((), ())), preferred_element_type=F32)


def _split2(x):
    hi = x.astype(BF16)
    lo = (x - hi.astype(F32)).astype(BF16)
    return hi, lo


def _split3(x):
    hi = x.astype(BF16)
    r1 = x - hi.astype(F32)
    mid = r1.astype(BF16)
    lo = (r1 - mid.astype(F32)).astype(BF16)
    return hi, mid, lo


def _dot_exact_rhs(x, m_bf16, parts):
    pieces = _split3(x) if parts == 3 else _split2(x)
    if parts * x.shape[1] <= MXU_DIM:
        return _dot(jnp.concatenate(pieces, axis=1), jnp.concatenate([m_bf16] * parts, axis=0))
    acc = _dot(pieces[0], m_bf16)
    for p in pieces[1:]:
        acc = acc + _dot(p, m_bf16)
    return acc


def _dot_exact_lhs3(m3_bf16, x):
    return _dot(m3_bf16, jnp.concatenate(_split3(x), axis=0))


def _rms(x, g):
    ms = jnp.mean(x * x, axis=-1, keepdims=True)
    return x * lax.rsqrt(ms + NORM_EPS) * g


def _head_sum(x, ones_bd):
    c = x.shape[-1]
    outs = []
    for j in range(c // MXU_DIM):
        outs.append(_dot_exact_rhs(x[:, j * MXU_DIM:(j + 1) * MXU_DIM], ones_bd, 2))
    return jnp.concatenate(outs, axis=-1) if len(outs) > 1 else outs[0]


def _norm_mm_kernel(x_ref, g_ref, w_ref, o_ref, xn_ref):
    @pl.when(pl.program_id(1) == 0)
    def _():
        xn_ref[...] = _rms(x_ref[...], g_ref[...]).astype(BF16)

    o_ref[...] = _dot(xn_ref[...], w_ref[...]).astype(o_ref.dtype)


def _norm_matmul(x, g, w, out_dtype, tm, tn):
    m, c = x.shape
    n = w.shape[1]
    return pl.pallas_call(
        _norm_mm_kernel,
        grid=(m // tm, n // tn),
        in_specs=[
            pl.BlockSpec((tm, c), lambda i, j: (i, 0)),
            pl.BlockSpec((1, c), lambda i, j: (0, 0)),
            pl.BlockSpec((c, tn), lambda i, j: (0, j)),
        ],
        out_specs=pl.BlockSpec((tm, tn), lambda i, j: (i, j)),
        out_shape=jax.ShapeDtypeStruct((m, n), out_dtype),
        scratch_shapes=[pltpu.VMEM((tm, c), BF16)],
        compiler_params=_cparams("parallel", "arbitrary"),
        name="norm_matmul",
    )(x, g, w)


PROJ_SUBTILE = 512


def _softplus(x):
    return jnp.maximum(x, 0.0) + jnp.log(1.0 + jnp.exp(-jnp.abs(x)))


def _rwkv_proj_kernel(h_ref, gn_ref, mix_ref, wr_ref, wk_ref, wv_ref, w1_ref, w2_ref, a1_ref, a2_ref,
                      g1_ref, g2_ref, w0_ref, a0_ref, kk_ref, ka_ref, ones_ref,
                      r_out, lw_out, k_out, v_out, kkn_out, b_out, g_out, carry_ref):
    ts = h_ref.shape[1]
    hn = _rms(h_ref[0], gn_ref[...])

    @pl.when(pl.program_id(1) == 0)
    def _():
        carry_ref[...] = jnp.zeros_like(carry_ref)

    prev_last = carry_ref[0:1, :]
    row = lax.broadcasted_iota(jnp.int32, hn.shape, 0)
    hp = jnp.where(row == 0, prev_last, pltpu.roll(hn, 1, 0))
    carry_ref[0:1, :] = hn[ts - 1:ts, :]
    dx = hp - hn

    def project(rows):
        hn_s, dx_s = hn[rows], dx[rows]
        mixed = lambda i: (hn_s + dx_s * mix_ref[i:i + 1, :]).astype(BF16)
        r = _dot(mixed(0), wr_ref[...])
        lora_w = _dot(jnp.tanh(_dot(mixed(1), w1_ref[...])).astype(BF16), w2_ref[...])
        k = _dot(mixed(2), wk_ref[...])
        v = _dot(mixed(3), wv_ref[...])
        lora_a = _dot(_dot(mixed(4), a1_ref[...]).astype(BF16), a2_ref[...])
        g = _dot(jax.nn.sigmoid(_dot(mixed(5), g1_ref[...])).astype(BF16), g2_ref[...])
        return r, lora_w, k, v, lora_a, g

    def finish(rows, r, lora_w, k, v, lora_a, g):
        w_log = -_softplus(-(w0_ref[...] + lora_w)) - 0.5
        a = jax.nn.sigmoid(a0_ref[...] + lora_a)
        kk = k * kk_ref[...]
        nrm = jnp.sqrt(_head_sum(kk * kk, ones_ref[...]))
        kkn = kk / jnp.maximum(nrm, 1e-12)
        r_out[0, rows] = r.astype(r_out.dtype)
        lw_out[0, rows] = -jnp.exp(w_log)
        k_out[0, rows] = (k * (1.0 + (a - 1.0) * ka_ref[...])).astype(k_out.dtype)
        v_out[0, rows] = v.astype(v_out.dtype)
        kkn_out[0, rows] = kkn.astype(kkn_out.dtype)
        b_out[0, rows] = (kkn * a).astype(b_out.dtype)
        g_out[0, rows] = g.astype(g_out.dtype)

    sub = min(PROJ_SUBTILE, ts)
    pending = None
    for i in range(ts // sub + 1):
        rows = slice(i * sub, (i + 1) * sub)
        cur = (rows,) + project(rows) if i < ts // sub else None
        if pending is not None:
            finish(*pending)
        pending = cur


def _rwkv_proj(h, gn, mix, wr, wk, wv, w1, w2, a1, a2, g1, g2, w0, a0, k_k, k_a, ones_bd, ts):
    b, s, c = h.shape
    full = lambda arr: pl.BlockSpec(arr.shape, lambda i, j: (0,) * arr.ndim)
    tok = pl.BlockSpec((1, ts, c), lambda i, j: (i, j, 0))
    params = (gn, mix, wr, wk, wv, w1, w2, a1, a2, g1, g2, w0, a0, k_k, k_a, ones_bd)
    return pl.pallas_call(
        _rwkv_proj_kernel,
        grid=(b, s // ts),
        in_specs=[tok] + [full(p) for p in params],
        out_specs=[tok] * 7,
        out_shape=[jax.ShapeDtypeStruct((b, s, c), F32 if n == 1 else BF16) for n in range(7)],
        scratch_shapes=[pltpu.VMEM((8, c), F32)],
        compiler_params=_cparams("parallel", "arbitrary"),
        name="rwkv_proj",
    )(h, *params)


def _rwkv_rec_kernel(r_ref, lw_ref, k_ref, v_ref, kk_ref, b_ref, g_ref, rk_ref, lnw_ref, lnb_ref,
                     tri_ref, ones_ref, o_ref,
                     state_ref, xp_ref, vp_ref, y0_ref, bdt_ref, vst_ref, dl_ref, bon_ref):
    ts, lw_lanes = r_ref.shape[1], r_ref.shape[2]
    n_chunks = ts // CHUNK
    n_groups = lw_lanes // LANES
    L = CHUNK
    prep_chunks = 2 if n_chunks % 2 == 0 else 1

    @pl.when(pl.program_id(2) == 0)
    def _():
        state_ref[...] = jnp.zeros_like(state_ref)

    lane = lax.broadcasted_iota(jnp.int32, (L, LANES), 1)
    head0 = lane < HEAD_DIM
    rr = lax.broadcasted_iota(jnp.int32, (2 * L, 2 * L), 0)
    cc = lax.broadcasted_iota(jnp.int32, (2 * L, 2 * L), 1)
    same_head = (rr // L) == (cc // L)
    strict = same_head & ((cc % L) < (rr % L))
    incl = same_head & ((cc % L) <= (rr % L))
    eye = (rr == cc).astype(F32)
    first_head = rr < L
    tri_incl = tri_ref[...]
    ones_bd = ones_ref[...]

    def stack_heads(x):
        return jnp.concatenate([jnp.where(head0, x, 0.0), jnp.where(head0, 0.0, x)], axis=0)

    def unstack(x):
        return x[:L] + x[L:]

    def prepare(ci, tick):
        chains = [(ci * prep_chunks + cc, gi) for cc in range(prep_chunks) for gi in range(n_groups)]
        rows = [pl.ds(pl.multiple_of(c * L, L), L) for c, _ in chains]
        lanes = [slice(gi * LANES, (gi + 1) * LANES) for _, gi in chains]
        idxs = [c * n_groups + gi for c, gi in chains]
        each = lambda f, *cols: [f(*a) for a in zip(*cols)]
        load = lambda ref: each(lambda rw, ls: ref[0, rw, ls].astype(F32), rows, lanes)
        r, lw, k, v, kk, bb = (load(ref) for ref in (r_ref, lw_ref, k_ref, v_ref, kk_ref, b_ref))

        cum = each(lambda x: _dot_exact_lhs3(tri_incl, x), lw)
        d_inv = each(lambda c_: jnp.exp(-c_), cum)
        d_last = each(lambda c_: jnp.exp(c_[L - 1:L, :]), cum)
        xa = each(lambda c_, l_, kk_: stack_heads(-(jnp.exp(c_ - l_) * kk_)).astype(BF16), cum, lw, kk)
        xr32 = each(lambda c_, r_: stack_heads(jnp.exp(c_) * r_), cum, r)
        bt32 = each(lambda b_, d_: b_ * d_, bb, d_inv)
        kt32 = each(lambda k_, d_: k_ * d_, k, d_inv)

        def gram(xa_, xr_, bt_, kt_):
            z2 = jnp.concatenate([bt_.astype(BF16), kt_.astype(BF16)], axis=0)
            x_all = jnp.concatenate([xa_, xr_.astype(BF16)], axis=0)
            return _dot_nt(x_all, z2)

        gmat = each(gram, xa, xr32, bt32, kt32)
        tick()
        swap = each(lambda g_: pltpu.roll(g_, HEAD_DIM, 1), gmat)
        m_ab = each(lambda g_, w_: jnp.where(strict, jnp.where(first_head, g_[:2 * L], w_[:2 * L]), 0.0), gmat, swap)
        m_ak = each(lambda g_, w_: jnp.where(strict, jnp.where(first_head, w_[:2 * L], g_[:2 * L]), 0.0).astype(BF16),
                    gmat, swap)
        m_rb = each(lambda g_, w_: jnp.where(incl, jnp.where(first_head, g_[2 * L:], w_[2 * L:]), 0.0).astype(BF16),
                    gmat, swap)
        m_rk = each(lambda g_, w_: jnp.where(incl, jnp.where(first_head, w_[2 * L:], g_[2 * L:]), 0.0).astype(BF16),
                    gmat, swap)

        p = m_ab
        t_inv = each(lambda m_: eye + m_, m_ab)
        for _ in range(5):
            p = each(lambda p_: _dot(p_.astype(BF16), p_.astype(BF16)), p)
            t_inv = each(lambda t_, p_: t_ + _dot(t_.astype(BF16), p_.astype(BF16)), t_inv, p)
            tick()
        t16 = each(lambda t_: t_.astype(BF16), t_inv)

        v_st = each(lambda v_: stack_heads(v_).astype(BF16), v)
        mv = each(_dot, m_ak, v_st)
        tick()
        tav = each(lambda t_, xa_, mv_: _dot(t_, jnp.concatenate([xa_, mv_.astype(BF16)], axis=1)),
                   t16, xa, mv)
        tick()
        rby = each(lambda m_, tav_: _dot(m_, tav_.astype(BF16)), m_rb, tav)
        tick()
        y0 = each(lambda rby_, m_, v_: rby_[:, LANES:] + _dot(m_, v_), rby, m_rk, v_st)
        for n, idx in enumerate(idxs):
            xp_ref[idx] = jnp.concatenate([tav[n][:, :LANES], xr32[n] + rby[n][:, :LANES]], axis=0).astype(BF16)
            vp_ref[idx] = tav[n][:, LANES:]
            y0_ref[idx] = y0[n]
            bdt_ref[idx] = jnp.concatenate([stack_heads(bt32[n] * d_last[n]), stack_heads(kt32[n] * d_last[n])],
                                           axis=0).T.astype(BF16)
            vst_ref[idx] = v_st[n]
            dl_ref[idx] = jnp.broadcast_to(d_last[n], (LANES, LANES)).T
            bon_ref[rows[n], lanes[n]] = _dot_exact_rhs(r[n] * k[n] * rk_ref[:, lanes[n]], ones_bd, 2) * v[n]

    def advance_steps(ci):
        rows = pl.ds(pl.multiple_of(ci * L, L), L)
        groups = list(range(n_groups))
        lanes = [slice(gi * LANES, (gi + 1) * LANES) for gi in groups]
        idxs = [ci * n_groups + gi for gi in groups]
        each = lambda f, *cols: [f(*a) for a in zip(*cols)]
        s_kv = each(lambda gi: state_ref[gi], groups)
        xs = each(lambda i_, s_: _dot(xp_ref[i_], s_.astype(BF16)), idxs, s_kv)
        yield
        u16 = each(lambda x_, i_: (x_[:2 * L] + vp_ref[i_]).astype(BF16), xs, idxs)
        upd = each(lambda i_, u_: _dot(bdt_ref[i_], jnp.concatenate([u_, vst_ref[i_]], axis=0)), idxs, u16)
        for gi in groups:
            state_ref[gi] = s_kv[gi] * dl_ref[idxs[gi]] + upd[gi]
        yield

        inv_n = 1.0 / HEAD_DIM
        y = each(lambda x_, i_: unstack(x_[2 * L:] + y0_ref[i_]), xs, idxs)
        yc = each(lambda y_: y_ - _dot_exact_rhs(y_, ones_bd, 2) * inv_n, y)
        yield
        var = each(lambda yc_: _dot_exact_rhs(yc_ * yc_, ones_bd, 2) * inv_n, yc)
        for gi in groups:
            ls = lanes[gi]
            yn = yc[gi] * lax.rsqrt(var[gi] + GN_EPS) * lnw_ref[:, ls] + lnb_ref[:, ls]
            o_ref[0, rows, ls] = ((yn + bon_ref[rows, ls]) * g_ref[0, rows, ls].astype(F32)).astype(o_ref.dtype)
        yield

    n_prep = n_chunks // prep_chunks

    def advance_pair(pi):
        for cc in range(prep_chunks):
            yield from advance_steps(pi * prep_chunks + cc)

    def prepare_with(pi, side):
        prepare(pi, (lambda: next(side, None)) if side is not None else (lambda: None))
        if side is not None:
            for _ in side:
                pass

    prepare_with(jnp.int32(0), None)

    def overlapped(pi, carry):
        prepare_with(pi, advance_pair(pi - 1))
        return carry

    lax.fori_loop(1, n_prep, overlapped, 0)
    for _ in advance_pair(jnp.int32(n_prep - 1)):
        pass


def _rwkv_recurrence(r, lw, k, v, kkn, bb, g, r_k, lnx_w, lnx_b, tri, ones_pair, ts, lane_w):
    b, s, c = r.shape
    tok = pl.BlockSpec((1, ts, lane_w), lambda i, j, t: (i, t, j))
    vec = pl.BlockSpec((1, lane_w), lambda i, j, t: (0, j))
    small = lambda arr: pl.BlockSpec(arr.shape, lambda i, j, t: (0,) * arr.ndim)
    n_groups = lane_w // LANES
    ncg = (ts // CHUNK) * n_groups
    return pl.pallas_call(
        _rwkv_rec_kernel,
        grid=(b, c // lane_w, s // ts),
        in_specs=[tok] * 7 + [vec, vec, vec, small(tri), small(ones_pair)],
        out_specs=tok,
        out_shape=jax.ShapeDtypeStruct((b, s, c), BF16),
        scratch_shapes=[pltpu.VMEM((n_groups, LANES, LANES), F32),
                        pltpu.VMEM((ncg, 4 * CHUNK, LANES), BF16),
                        pltpu.VMEM((ncg, 2 * CHUNK, LANES), F32),
                        pltpu.VMEM((ncg, 2 * CHUNK, LANES), F32),
                        pltpu.VMEM((ncg, LANES, 4 * CHUNK), BF16),
                        pltpu.VMEM((ncg, 2 * CHUNK, LANES), BF16),
                        pltpu.VMEM((ncg, LANES, LANES), F32),
                        pltpu.VMEM((ts, lane_w), F32)],
        compiler_params=_cparams("parallel", "parallel", "arbitrary"),
        name="rwkv_recurrence",
    )(r, lw, k, v, kkn, bb, g, r_k, lnx_w, lnx_b, tri, ones_pair)


SB_QTILE = 512
SB_SUBTILE = 128
SB_STAGE_LAG = 3
SB_SKIP_BELOW = -104.0
SB_PAIRS = 4


def _sb_attn_kernel(q_ref, k_ref, v_ref, cs_ref, o_ref, qs_ref, acc_ref, run_ref, tmax_ref):
    s = q_ref.shape[1]
    tb = SB_BLOCK
    tq = min(SB_QTILE, s)
    rs = SB_SUBTILE
    sub_per_block = tb // rs
    bpt = tq // tb
    kpb = 2 if bpt % 2 == 0 else 1
    nb, nt = s // tb, s // tq
    shift = bpt.bit_length() - 1
    lane = lax.broadcasted_iota(jnp.int32, (s, LANES), 1)
    head0 = lane < HEAD_DIM
    n_heads = qs_ref.shape[0]
    for pp in range(n_heads // 2):
        q = q_ref[0, :, pp * LANES:(pp + 1) * LANES] * (HEAD_DIM ** -0.5)
        qs_ref[2 * pp] = jnp.where(head0, q, jnp.zeros_like(q))
        qs_ref[2 * pp + 1] = jnp.where(head0, jnp.zeros_like(q), q)
    for qb in range(nb):
        tmax_ref[qb] = 0.0
    cs = cs_ref[...]
    cs2 = jnp.concatenate([cs, cs], axis=0)

    def tile_step(ti, blocks, pieces):
        r0 = ti * tq
        subs = [(kbi, h, pl.ds(pl.multiple_of(r0 + i * rs, rs), rs), r0 + i * rs, diag, i // sub_per_block)
                for kbi, i, diag in pieces for h in range(n_heads)]
        n = len(subs)
        lag = SB_STAGE_LAG
        z, lb, parts, sums, att, before = ({} for _ in range(6))
        top = {}
        for step in range(n + 2 + 2 * lag):
            if step < n:
                kbi, h, rows = subs[step][:3]
                z[step] = _dot_nt(qs_ref[h, rows, :], blocks[kbi][1][h // 2])
            i = step - 1
            if 0 <= i < n:
                zz = z.pop(i)
                sp = jnp.log(1.0 + jnp.exp(-jnp.abs(zz)))
                lb[i] = jnp.minimum(zz, 0.0) - sp
                log_fail = lb[i] - zz
                if subs[i][4]:
                    t_idx = subs[i][3] + lax.broadcasted_iota(jnp.int32, (rs, tb), 0)
                    s_idx = blocks[subs[i][0]][0] + lax.broadcasted_iota(jnp.int32, (rs, tb), 1)
                    before[i] = s_idx < t_idx
                    log_fail = jnp.where(before[i], log_fail, 0.0)
                parts[i] = _split2(log_fail)
            i = step - 1 - lag
            if 0 <= i < n:
                hi, lo = parts.pop(i)
                sums[i] = _dot(jnp.concatenate([hi, lo], axis=1), cs2)
            i = step - 2 - lag
            if 0 <= i < n:
                kbi, h, rows, _, diag, _ = subs[i]
                sm = sums.pop(i)
                if diag:
                    a = jnp.where(before.pop(i), jnp.exp(lb.pop(i) + sm[:, :tb]), 0.0)
                    run = sm[:, tb:]
                else:
                    run = run_ref[h, rows, :]
                    a = jnp.exp(lb.pop(i) + run + sm[:, :tb])
                    run = run + sm[:, tb:]
                att[i] = a.astype(BF16)
                run_ref[h, rows, :] = run
                if kbi == len(blocks) - 1:
                    qb = subs[i][5]
                    top[qb] = run if qb not in top else jnp.maximum(top[qb], run)
            i = step - 2 - 2 * lag
            if 0 <= i < n:
                kbi, h, rows = subs[i][:3]
                pv = _dot(att.pop(i), blocks[kbi][2][h // 2])
                if subs[i][4]:
                    acc_ref[h, rows, :] = pv
                else:
                    acc_ref[h, rows, :] += pv
        for qb, t in top.items():
            tmax_ref[ti * bpt + qb] = jnp.max(t)

    def below_pieces(n_q):
        return [(kbi, i, False) for kbi in range(kpb) for i in range(n_q * sub_per_block)]

    def diagonal_pieces(u, q_last):
        jl = kpb * u + kpb - 1
        out = []
        for kbi in range(kpb):
            kl = jl - kbi
            for qb in range(kl, q_last + 1):
                for i in range(qb * sub_per_block, (qb + 1) * sub_per_block):
                    out.append((kbi, i, qb == kl))
        return out

    def last_alive(ti, first):
        n = jnp.int32(first)
        for qb in range(first, bpt):
            n = jnp.where(tmax_ref[ti * bpt + qb] >= SB_SKIP_BELOW, qb + 1, n)
        return n

    def key_blocks(jj, carry):
        j = nb - 1 - kpb * jj
        blocks = []
        for d in range(kpb):
            k0 = pl.multiple_of((j - d) * tb, tb)
            lanes = [slice(pp * LANES, (pp + 1) * LANES) for pp in range(n_heads // 2)]
            blocks.append((k0, [k_ref[0, pl.ds(k0, tb), ls] for ls in lanes],
                           [v_ref[0, pl.ds(k0, tb), ls] for ls in lanes]))
        t0 = lax.shift_right_logical(j, shift)
        case = lax.shift_right_logical(j - t0 * bpt, kpb.bit_length() - 1)
        for u in range(bpt // kpb):
            jl = kpb * u + kpb - 1
            n_q = last_alive(t0, jl + 1)
            for q_last in range(jl, bpt):
                @pl.when((case == u) & (n_q == q_last + 1))
                def _():
                    tile_step(t0, blocks, diagonal_pieces(u, q_last))

        def below(ti, c):
            n_q = last_alive(ti, 0)
            for m in range(1, bpt + 1):
                @pl.when(n_q == m)
                def _():
                    tile_step(ti, blocks, below_pieces(m))
            return c

        lax.fori_loop(t0 + 1, nt, below, 0)
        return carry

    lax.fori_loop(0, nb // kpb, key_blocks, 0)
    for pp in range(n_heads // 2):
        o_ref[0, :, pp * LANES:(pp + 1) * LANES] = jnp.where(head0, acc_ref[2 * pp], acc_ref[2 * pp + 1]).astype(o_ref.dtype)


def _sb_attention(qkv, cs):
    b, s, c3 = qkv.shape
    c = c3 // 3
    w = LANES * SB_PAIRS
    n_blk = c // w
    n_heads = 2 * SB_PAIRS
    return pl.pallas_call(
        _sb_attn_kernel,
        grid=(b, n_blk),
        in_specs=[
            pl.BlockSpec((1, s, w), lambda i, p: (i, 0, p)),
            pl.BlockSpec((1, s, w), lambda i, p: (i, 0, n_blk + p)),
            pl.BlockSpec((1, s, w), lambda i, p: (i, 0, 2 * n_blk + p)),
            pl.BlockSpec(cs.shape, lambda i, p: (0, 0)),
        ],
        out_specs=pl.BlockSpec((1, s, w), lambda i, p: (i, 0, p)),
        out_shape=jax.ShapeDtypeStruct((b, s, c), BF16),
        scratch_shapes=[pltpu.VMEM((n_heads, s, LANES), BF16), pltpu.VMEM((n_heads, s, LANES), F32),
                        pltpu.VMEM((n_heads, s, SB_BLOCK), F32),
                        pltpu.SMEM((s // SB_BLOCK,), F32)],
        compiler_params=_cparams("parallel", "parallel"),
        name="sb_attention",
    )(qkv, qkv, qkv, cs)


def _cross_kernel(y_ref, wy_ref, h_ref, gn_ref, wq_ref, kv_ref, wo_ref, o_ref):
    h = h_ref[0] + _dot(y_ref[0], wy_ref[...])
    c = h.shape[-1]
    xhd = c // X_HEADS
    q = _dot(_rms(h, gn_ref[...]).astype(BF16), wq_ref[...]).astype(BF16)
    outs = []
    for hd in range(X_HEADS):
        qh = q[:, hd * xhd:(hd + 1) * xhd]
        kh = kv_ref[0, :, hd * xhd:(hd + 1) * xhd]
        vh = kv_ref[0, :, c + hd * xhd:c + (hd + 1) * xhd]
        sc = _dot_nt(qh, kh) * (xhd ** -0.5)
        sc = sc - jnp.max(sc, axis=-1, keepdims=True)
        e = jnp.exp(sc)
        p = e / jnp.sum(e, axis=-1, keepdims=True)
        outs.append(_dot(p.astype(BF16), vh).astype(BF16))
    o = jnp.concatenate(outs, axis=-1)
    o_ref[0] = h + _dot(o, wo_ref[...])


def _cross_block(y, wy, h, gn, wq, memkv, wo, tq):
    b, s, c = h.shape
    tok = pl.BlockSpec((1, tq, c), lambda i, t: (i, t, 0))
    full = lambda arr: pl.BlockSpec(arr.shape, lambda i, t: (0,) * arr.ndim)
    return pl.pallas_call(
        _cross_kernel,
        grid=(b, s // tq),
        in_specs=[tok, full(wy), tok, full(gn), full(wq),
                  pl.BlockSpec((1,) + memkv.shape[1:], lambda i, t: (i, 0, 0)), full(wo)],
        out_specs=tok,
        out_shape=jax.ShapeDtypeStruct((b, s, c), F32),
        compiler_params=_cparams("parallel", "parallel"),
        name="cross_attention",
    )(y, wy, h, gn, wq, memkv, wo)


def _lane_min_index(mask, lane):
    return jnp.min(jnp.where(mask, lane, float(ROUTER_LANES)), axis=-1, keepdims=True)


MOE_ROUTE_TILE = 512
MOE_ROW_TILE = 256


def _dot_3pass(a, b):
    ah, al = _split2(a)
    bh, bl = _split2(b)
    return _dot(ah, bl) + _dot(al, bh) + _dot(ah, bh)


def _moe_route_kernel(h_ref, gn_ref, wg_ref, bg_ref, tri_ref, dest_ref, pg_ref, cnt_ref, base_ref, *, n_tokens):
    @pl.when(pl.program_id(0) == 0)
    def _():
        base_ref[...] = jnp.zeros_like(base_ref)

    hn = _rms(h_ref[...], gn_ref[...])
    gl = _dot_3pass(hn, wg_ref[...]) + bg_ref[...]
    lane = lax.broadcasted_iota(jnp.int32, gl.shape, 1).astype(F32)
    gmax = jnp.max(gl, axis=-1, keepdims=True)
    group = _lane_min_index(gl == gmax, lane)
    p_group = 1.0 / jnp.sum(jnp.exp(gl - gmax), axis=-1, keepdims=True)
    onehot = (lane == group).astype(F32)
    earlier = _dot(tri_ref[...], onehot.astype(BF16))
    base = base_ref[0:1, :]
    rank = jnp.sum(onehot * (earlier + base), axis=-1, keepdims=True)
    dest = group * float(n_tokens) + rank
    tm = dest.shape[0]
    dest_row = jnp.broadcast_to(dest, (tm, LANES)).T[0:1, :]
    dest_ref[0] = dest_row.astype(jnp.int32)
    pg_ref[...] = jnp.broadcast_to(p_group, pg_ref.shape)
    new_base = base + earlier[tm - 1:tm, :] + onehot[tm - 1:tm, :]
    base_ref[0:1, :] = new_base
    cnt_ref[...] = jnp.broadcast_to(new_base, cnt_ref.shape)


def _moe_route(h, gn, wg128, bg128, tri, tm):
    t, c = h.shape
    full = lambda arr: pl.BlockSpec(arr.shape, lambda i: (0,) * arr.ndim)
    return pl.pallas_call(
        functools.partial(_moe_route_kernel, n_tokens=t),
        grid=(t // tm,),
        in_specs=[pl.BlockSpec((tm, c), lambda i: (i, 0)), full(gn), full(wg128), full(bg128), full(tri)],
        out_specs=[pl.BlockSpec((1, 1, tm), lambda i: (i, 0, 0)),
                   pl.BlockSpec((tm, LANES), lambda i: (i, 0)),
                   pl.BlockSpec((8, LANES), lambda i: (0, 0))],
        out_shape=[jax.ShapeDtypeStruct((t // tm, 1, tm), jnp.int32),
                   jax.ShapeDtypeStruct((t, LANES), F32),
                   jax.ShapeDtypeStruct((8, LANES), F32)],
        scratch_shapes=[pltpu.VMEM((8, LANES), F32)],
        compiler_params=_cparams("arbitrary"),
        name="moe_route",
    )(h, gn, wg128, bg128, tri)


def _moe_scatter_kernel(zt_ref, dest_ref, h_ref, gn_ref, xs_ref, buf_ref, sem_ref, zsem_ref):
    i = pl.program_id(0)
    n = pl.num_programs(0)
    tm = h_ref.shape[0]
    slot = lax.rem(i, 2)

    def row_copy(s, r, d):
        return pltpu.make_async_copy(buf_ref.at[s, pl.ds(r, 1), :], xs_ref.at[pl.ds(d, 1), :], sem_ref.at[s])

    def wait_all(s):
        pltpu.make_async_copy(buf_ref.at[s], xs_ref.at[pl.ds(0, tm), :], sem_ref.at[s]).wait()

    @pl.when(i == 0)
    def _():
        buf_ref[1] = jnp.zeros(buf_ref.shape[1:], buf_ref.dtype)
        fills = [pltpu.make_async_copy(buf_ref.at[1], xs_ref.at[pl.ds(pl.multiple_of(zt_ref[k] * tm, tm), tm), :],
                                       zsem_ref.at[0]) for k in range(zt_ref.shape[0])]
        for f in fills:
            f.start()
            f.wait()

    buf_ref[slot] = _rms(h_ref[...], gn_ref[...])

    for r in range(tm):
        row_copy(slot, r, dest_ref[0, 0, r]).start(priority=r % 2)

    @pl.when(i > 0)
    def _():
        wait_all(1 - slot)

    @pl.when(i == n - 1)
    def _():
        wait_all(slot)


def _moe_scatter(zero_tiles, h, gn, dest3, n_rows, tm):
    t, c = h.shape
    grid_spec = pltpu.PrefetchScalarGridSpec(
        num_scalar_prefetch=1,
        grid=(t // tm,),
        in_specs=[pl.BlockSpec((1, 1, tm), lambda i, zt: (i, 0, 0), memory_space=pltpu.SMEM),
                  pl.BlockSpec((tm, c), lambda i, zt: (i, 0)),
                  pl.BlockSpec(gn.shape, lambda i, zt: (0, 0))],
        out_specs=pl.BlockSpec(memory_space=pl.ANY),
        scratch_shapes=[pltpu.VMEM((2, tm, c), F32), pltpu.SemaphoreType.DMA((2,)),
                        pltpu.SemaphoreType.DMA((1,))],
    )
    return pl.pallas_call(
        _moe_scatter_kernel,
        grid_spec=grid_spec,
        out_shape=jax.ShapeDtypeStruct((n_rows, c), F32),
        compiler_params=_cparams("arbitrary"),
        name="moe_scatter",
    )(zero_tiles, dest3, h, gn)


def _moe_group_ffn_kernel(tg_ref, x_ref, we_ref, be_ref, w1_ref, w3_ref, w2_ref, o_ref):
    del tg_ref
    x = x_ref[...]
    xh, xl = _split2(x)
    el = be_ref[0] + _dot(xh, we_ref[0, 1]) + _dot(xl, we_ref[0, 0]) + _dot(xh, we_ref[0, 0])
    lane = lax.broadcasted_iota(jnp.int32, el.shape, 1).astype(F32)
    top1 = jnp.max(el, axis=-1, keepdims=True)
    idx1 = _lane_min_index(el == top1, lane)
    el2 = jnp.where(lane == idx1, -jnp.inf, el)
    top2 = jnp.max(el2, axis=-1, keepdims=True)
    idx2 = _lane_min_index(el2 == top2, lane)
    e2 = jnp.exp(top2 - top1)
    s1 = 1.0 / (1.0 + e2)
    s2 = e2 / (1.0 + e2)
    hids = []
    for e in range(EXPERTS_PER_GROUP):
        ge = jnp.where(idx1 == float(e), s1, jnp.where(idx2 == float(e), s2, 0.0))
        h1 = _dot(xh, w1_ref[0, e])
        h3 = _dot(xh, w3_ref[0, e])
        hids.append(((h1 * jax.nn.sigmoid(h1)) * h3 * ge).astype(BF16))
    o_ref[...] = _dot(jnp.concatenate(hids, axis=1), w2_ref[0])


def _moe_group_ffn(tile_group, xs, we2, be, w1, w3, w2, tm):
    rows, c = xs.shape
    g, e, _, f = w1.shape
    grid_spec = pltpu.PrefetchScalarGridSpec(
        num_scalar_prefetch=1,
        grid=(rows // tm,),
        in_specs=[pl.BlockSpec((tm, c), lambda i, tg: (i, 0)),
                  pl.BlockSpec((1, 2, c, LANES), lambda i, tg: (tg[i], 0, 0, 0)),
                  pl.BlockSpec((1, 1, LANES), lambda i, tg: (tg[i], 0, 0)),
                  pl.BlockSpec((1, e, c, f), lambda i, tg: (tg[i], 0, 0, 0)),
                  pl.BlockSpec((1, e, c, f), lambda i, tg: (tg[i], 0, 0, 0)),
                  pl.BlockSpec((1, e * f, c), lambda i, tg: (tg[i], 0, 0))],
        out_specs=pl.BlockSpec((tm, c), lambda i, tg: (i, 0)),
    )
    return pl.pallas_call(
        _moe_group_ffn_kernel,
        grid_spec=grid_spec,
        out_shape=jax.ShapeDtypeStruct((rows, c), F32),
        compiler_params=_cparams("arbitrary"),
        name="moe_group_ffn",
    )(tile_group, xs, we2, be, w1, w3, w2)


def _moe_gather_kernel(dest_ref, nxt_ref, h_ref, pg_ref, gf_ref, ys_ref, o_ref, buf_ref, sem_ref, *, final_norm):
    i = pl.program_id(0)
    n = pl.num_programs(0)
    tm = h_ref.shape[0]
    slot = lax.rem(i, 2)

    def row_copy(s, r, d):
        return pltpu.make_async_copy(ys_ref.at[pl.ds(d, 1), :], buf_ref.at[s, pl.ds(r, 1), :], sem_ref.at[s])

    def start_all(s, idx_ref):
        for r in range(tm):
            row_copy(s, r, idx_ref[0, 0, r]).start(priority=r % 2)

    @pl.when(i == 0)
    def _():
        start_all(0, dest_ref)

    @pl.when(i + 1 < n)
    def _():
        start_all(1 - slot, nxt_ref)

    pltpu.make_async_copy(ys_ref.at[pl.ds(0, tm), :], buf_ref.at[slot], sem_ref.at[slot]).wait()

    y = buf_ref[slot]
    pg = pg_ref[...]
    c = y.shape[1]
    out = h_ref[...] + y * jnp.concatenate([pg] * (c // LANES), axis=1)
    if final_norm:
        out = _rms(out, gf_ref[...])
    o_ref[...] = out


def _moe_gather(h, pg, dest3, ys, g_final, final_norm, tm):
    t, c = h.shape
    n = t // tm
    return pl.pallas_call(
        functools.partial(_moe_gather_kernel, final_norm=final_norm),
        grid=(n,),
        in_specs=[pl.BlockSpec((1, 1, tm), lambda i: (i, 0, 0), memory_space=pltpu.SMEM),
                  pl.BlockSpec((1, 1, tm), lambda i: (jnp.minimum(i + 1, n - 1), 0, 0), memory_space=pltpu.SMEM),
                  pl.BlockSpec((tm, c), lambda i: (i, 0)),
                  pl.BlockSpec((tm, LANES), lambda i: (i, 0)),
                  pl.BlockSpec(g_final.shape, lambda i: (0, 0)),
                  pl.BlockSpec(memory_space=pl.ANY)],
        out_specs=pl.BlockSpec((tm, c), lambda i: (i, 0)),
        out_shape=jax.ShapeDtypeStruct((t, c), F32),
        scratch_shapes=[pltpu.VMEM((2, tm, c), F32), pltpu.SemaphoreType.DMA((2,))],
        compiler_params=_cparams("arbitrary"),
        name="moe_gather",
    )(dest3, dest3, h, pg, g_final, ys)


def _moe_layer(h, gn, wg, bg, we, be, w1, w3, w2, g_final, final_norm):
    t, c = h.shape
    g, e = N_GROUPS, EXPERTS_PER_GROUP
    f = w1.shape[-1]
    tm_r = MOE_ROUTE_TILE if t % MOE_ROUTE_TILE == 0 else t
    tm = MOE_ROW_TILE if t % MOE_ROW_TILE == 0 else t
    neg = -1e30
    wg128 = jnp.zeros((c, LANES), F32).at[:, :g].set(wg)
    bg128 = jnp.full((1, LANES), neg, F32).at[0, :g].set(bg)
    jj = jnp.arange(tm_r)
    tri = (jj[:, None] > jj[None, :]).astype(BF16)
    dest_local, pg, counts = _moe_route(h, gn, wg128, bg128, tri, tm_r)

    cnt = counts[0, :g].astype(jnp.int32)
    tiles = (cnt + tm - 1) // tm
    tile_end = jnp.cumsum(tiles)
    offs = (tile_end - tiles) * tm
    n_tiles = t // tm + g
    dl = dest_local.reshape(t)
    grp = dl // t
    dest = (dl - grp * t + offs[grp]).reshape(t // tm, 1, tm)
    tile_group = jnp.minimum(jnp.sum(jnp.arange(n_tiles)[:, None] >= tile_end[None, :], axis=1), g - 1).astype(jnp.int32)

    zero_tiles = jnp.concatenate([jnp.maximum(tile_end - 1, 0),
                                  jnp.minimum(tile_end[g - 1] + jnp.arange(g), n_tiles - 1)]).astype(jnp.int32)
    xs = _moe_scatter(zero_tiles, h, gn, dest, n_tiles * tm, tm)
    we_g = jnp.zeros((g, c, LANES), F32).at[:, :, :e].set(we.reshape(c, g, e).transpose(1, 0, 2))
    we2 = jnp.stack(_split2(we_g), axis=1)
    be_g = jnp.full((g, 1, LANES), neg, F32).at[:, 0, :e].set(be.reshape(g, e))
    ys = _moe_group_ffn(tile_group, xs, we2, be_g, w1.astype(BF16), w3.astype(BF16),
                        w2.astype(BF16).reshape(g, e * f, c), tm)
    return _moe_gather(h, pg, dest, ys, g_final, final_norm, tm)


def _block_ones(n, blk):
    i = jnp.arange(n)
    return ((i[:, None] // blk) == (i[None, :] // blk)).astype(BF16)


def _tile(n, pref):
    return pref if n % pref == 0 else n


def kernel(x, mem, norm_mix, norm_cross, norm_ffn, norm_mem, norm_final, rw_mix, rw_wr, rw_wk, rw_wv, rw_w0, rw_w1, rw_w2, rw_a0, rw_a1, rw_a2, rw_g1, rw_g2, rw_kk, rw_ka, rw_rk, rw_lnx_w, rw_lnx_b, rw_wo, sb_wqkv, sb_wo, xa_wq, xa_wkv, xa_wo, moe_wg, moe_bg, moe_we, moe_be, moe_w1, moe_w3, moe_w2):
    b, s, c = x.shape
    t = b * s
    depth = norm_mix.shape[0]
    row = lambda vec: vec.reshape(1, -1).astype(F32)
    bf = lambda w: w.astype(BF16)

    ones_quad = _block_ones(MXU_DIM, HEAD_DIM)
    ones_pair = _block_ones(LANES, HEAD_DIM)
    tri_incl = jnp.tile((jnp.arange(CHUNK)[:, None] >= jnp.arange(CHUNK)[None, :]).astype(BF16), (1, 3))
    jj = jnp.arange(SB_BLOCK)
    sb_cs = jnp.concatenate([(jj[:, None] > jj[None, :]).astype(BF16),
                             jnp.ones((SB_BLOCK, SB_BLOCK), BF16)], axis=1)

    memkv = _norm_matmul(mem.reshape(b * N_MEM, c), row(norm_mem), bf(xa_wkv), BF16,
                         _tile(b * N_MEM, 512), _tile(2 * c, 1024)).reshape(b, N_MEM, 2 * c)

    h = x
    for i in range(depth):
        j = i // 2
        if i % 2 == 0:
            r, lw, k, v, kkn, bb, g = _rwkv_proj(
                h, row(norm_mix[i]), rw_mix[j].astype(F32), bf(rw_wr[j]), bf(rw_wk[j]), bf(rw_wv[j]),
                bf(rw_w1[j]), bf(rw_w2[j]), bf(rw_a1[j]), bf(rw_a2[j]), bf(rw_g1[j]), bf(rw_g2[j]),
                row(rw_w0[j]), row(rw_a0[j]), row(rw_kk[j]), row(rw_ka[j]), ones_quad, _tile(s, 512))
            y = _rwkv_recurrence(r, lw, k, v, kkn, bb, g, row(rw_rk[j]), row(rw_lnx_w[j]),
                                 row(rw_lnx_b[j]), tri_incl, ones_pair, _tile(s, 256), 8 * LANES)
            w_mix_out = bf(rw_wo[j])
        else:
            qkv = _norm_matmul(h.reshape(t, c), row(norm_mix[i]), bf(sb_wqkv[j]), BF16,
                               _tile(t, 1024), 3 * c).reshape(b, s, 3 * c)
            y = _sb_attention(qkv, sb_cs)
            w_mix_out = bf(sb_wo[j])
        h = _cross_block(y, w_mix_out, h, row(norm_cross[i]), bf(xa_wq[i]), memkv, bf(xa_wo[i]), _tile(s, 1024))

        h = _moe_layer(h.reshape(t, c), row(norm_ffn[i]), moe_wg[i], moe_bg[i], moe_we[i], moe_be[i],
                       moe_w1[i], moe_w3[i], moe_w2[i], row(norm_final), i == depth - 1).reshape(b, s, c)
    return h
```

```python
import functools

import jax
import jax.numpy as jnp
from jax import lax
from jax.experimental import pallas as pl
from jax.experimental.pallas import tpu as pltpu

F32 = jnp.float32
BF16 = jnp.bfloat16

HEAD_DIM = 64
N_MEM = 256
X_HEADS = 4
N_GROUPS = 4
EXPERTS_PER_GROUP = 8
GN_EPS = 64e-5
NORM_EPS = 1e-6
SB_BLOCK = 128
CHUNK = 64

LANES = 128
MXU_DIM = 256
VMEM_LIMIT = 48 * 1024 * 1024

ROUTER_LANES = LANES

MEMKV_ROWS = 512
QKV_ROWS = 1024
PROJ_ROWS = 512
REC_ROWS = 256
REC_LANES = 8 * LANES
CROSS_ROWS = 1024


def _cparams(*sem):
    return pltpu.CompilerParams(dimension_semantics=sem, vmem_limit_bytes=VMEM_LIMIT)


def _dot(a, b):
    return jnp.dot(a, b, preferred_element_type=F32)


def _dot_nt(a, b):
    return lax.dot_general(a, b, (((1,), (1,)), ((), ())), preferred_element_type=F32)


def _split2(x):
    hi = x.astype(BF16)
    lo = (x - hi.astype(F32)).astype(BF16)
    return hi, lo


def _split3(x):
    hi = x.astype(BF16)
    r1 = x - hi.astype(F32)
    mid = r1.astype(BF16)
    lo = (r1 - mid.astype(F32)).astype(BF16)
    return hi, mid, lo


def _dot_exact_rhs(x, m_bf16, parts):
    pieces = _split3(x) if parts == 3 else _split2(x)
    if parts * x.shape[1] <= MXU_DIM:
        return _dot(jnp.concatenate(pieces, axis=1), jnp.concatenate([m_bf16] * parts, axis=0))
    acc = _dot(pieces[0], m_bf16)
    for p in pieces[1:]:
        acc = acc + _dot(p, m_bf16)
    return acc


def _dot_exact_lhs3(m3_bf16, x):
    return _dot(m3_bf16, jnp.concatenate(_split3(x), axis=0))


def _rms(x, g):
    ms = jnp.mean(x * x, axis=-1, keepdims=True)
    return x * lax.rsqrt(ms + NORM_EPS) * g


def _head_sum(x, ones_bd):
    c = x.shape[-1]
    outs = []
    for j in range(c // MXU_DIM):
        outs.append(_dot_exact_rhs(x[:, j * MXU_DIM:(j + 1) * MXU_DIM], ones_bd, 2))
    return jnp.concatenate(outs, axis=-1) if len(outs) > 1 else outs[0]


def _norm_mm_kernel(x_ref, g_ref, w_ref, o_ref, xn_ref):
    @pl.when(pl.program_id(1) == 0)
    def _():
        xn_ref[...] = _rms(x_ref[...], g_ref[...]).astype(BF16)

    o_ref[...] = _dot(xn_ref[...], w_ref[...]).astype(o_ref.dtype)


def _norm_matmul(x, g, w, out_dtype, tm, tn):
    m, c = x.shape
    n = w.shape[1]
    return pl.pallas_call(
        _norm_mm_kernel,
        grid=(m // tm, n // tn),
        in_specs=[
            pl.BlockSpec((tm, c), lambda i, j: (i, 0)),
            pl.BlockSpec((1, c), lambda i, j: (0, 0)),
            pl.BlockSpec((c, tn), lambda i, j: (0, j)),
        ],
        out_specs=pl.BlockSpec((tm, tn), lambda i, j: (i, j)),
        out_shape=jax.ShapeDtypeStruct((m, n), out_dtype),
        scratch_shapes=[pltpu.VMEM((tm, c), BF16)],
        compiler_params=_cparams("parallel", "arbitrary"),
        name="norm_matmul",
    )(x, g, w)


PROJ_SUBTILE = 512


def _softplus(x):
    return jnp.maximum(x, 0.0) + jnp.log(1.0 + jnp.exp(-jnp.abs(x)))


def _rwkv_proj_kernel(h_ref, gn_ref, mix_ref, wr_ref, wk_ref, wv_ref, w1_ref, w2_ref, a1_ref, a2_ref,
                      g1_ref, g2_ref, w0_ref, a0_ref, kk_ref, ka_ref, ones_ref,
                      r_out, lw_out, k_out, v_out, kkn_out, b_out, g_out, carry_ref):
    ts = h_ref.shape[1]
    hn = _rms(h_ref[0], gn_ref[...])

    @pl.when(pl.program_id(1) == 0)
    def _():
        carry_ref[...] = jnp.zeros_like(carry_ref)

    prev_last = carry_ref[0:1, :]
    row = lax.broadcasted_iota(jnp.int32, hn.shape, 0)
    hp = jnp.where(row == 0, prev_last, pltpu.roll(hn, 1, 0))
    carry_ref[0:1, :] = hn[ts - 1:ts, :]
    dx = hp - hn

    def project(rows):
        hn_s, dx_s = hn[rows], dx[rows]
        mixed = lambda i: (hn_s + dx_s * mix_ref[i:i + 1, :]).astype(BF16)
        r = _dot(mixed(0), wr_ref[...])
        lora_w = _dot(jnp.tanh(_dot(mixed(1), w1_ref[...])).astype(BF16), w2_ref[...])
        k = _dot(mixed(2), wk_ref[...])
        v = _dot(mixed(3), wv_ref[...])
        lora_a = _dot(_dot(mixed(4), a1_ref[...]).astype(BF16), a2_ref[...])
        g = _dot(jax.nn.sigmoid(_dot(mixed(5), g1_ref[...])).astype(BF16), g2_ref[...])
        return r, lora_w, k, v, lora_a, g

    def finish(rows, r, lora_w, k, v, lora_a, g):
        w_log = -_softplus(-(w0_ref[...] + lora_w)) - 0.5
        a = jax.nn.sigmoid(a0_ref[...] + lora_a)
        kk = k * kk_ref[...]
        nrm = jnp.sqrt(_head_sum(kk * kk, ones_ref[...]))
        kkn = kk / jnp.maximum(nrm, 1e-12)
        r_out[0, rows] = r.astype(r_out.dtype)
        lw_out[0, rows] = -jnp.exp(w_log)
        k_out[0, rows] = (k * (1.0 + (a - 1.0) * ka_ref[...])).astype(k_out.dtype)
        v_out[0, rows] = v.astype(v_out.dtype)
        kkn_out[0, rows] = kkn.astype(kkn_out.dtype)
        b_out[0, rows] = (kkn * a).astype(b_out.dtype)
        g_out[0, rows] = g.astype(g_out.dtype)

    sub = min(PROJ_SUBTILE, ts)
    pending = None
    for i in range(ts // sub + 1):
        rows = slice(i * sub, (i + 1) * sub)
        cur = (rows,) + project(rows) if i < ts // sub else None
        if pending is not None:
            finish(*pending)
        pending = cur


def _rwkv_proj(h, gn, mix, wr, wk, wv, w1, w2, a1, a2, g1, g2, w0, a0, k_k, k_a, ones_bd, ts):
    b, s, c = h.shape
    full = lambda arr: pl.BlockSpec(arr.shape, lambda i, j: (0,) * arr.ndim)
    tok = pl.BlockSpec((1, ts, c), lambda i, j: (i, j, 0))
    params = (gn, mix, wr, wk, wv, w1, w2, a1, a2, g1, g2, w0, a0, k_k, k_a, ones_bd)
    return pl.pallas_call(
        _rwkv_proj_kernel,
        grid=(b, s // ts),
        in_specs=[tok] + [full(p) for p in params],
        out_specs=[tok] * 7,
        out_shape=[jax.ShapeDtypeStruct((b, s, c), F32 if n == 1 else BF16) for n in range(7)],
        scratch_shapes=[pltpu.VMEM((8, c), F32)],
        compiler_params=_cparams("parallel", "arbitrary"),
        name="rwkv_proj",
    )(h, *params)


def _rwkv_rec_kernel(r_ref, lw_ref, k_ref, v_ref, kk_ref, b_ref, g_ref, rk_ref, lnw_ref, lnb_ref,
                     tri_ref, ones_ref, o_ref,
                     state_ref, xp_ref, vp_ref, y0_ref, bdt_ref, vst_ref, dl_ref, bon_ref):
    ts, lw_lanes = r_ref.shape[1], r_ref.shape[2]
    n_chunks = ts // CHUNK
    n_groups = lw_lanes // LANES
    L = CHUNK
    prep_chunks = 2 if n_chunks % 2 == 0 else 1

    @pl.when(pl.program_id(2) == 0)
    def _():
        state_ref[...] = jnp.zeros_like(state_ref)

    lane = lax.broadcasted_iota(jnp.int32, (L, LANES), 1)
    head0 = lane < HEAD_DIM
    rr = lax.broadcasted_iota(jnp.int32, (2 * L, 2 * L), 0)
    cc = lax.broadcasted_iota(jnp.int32, (2 * L, 2 * L), 1)
    same_head = (rr // L) == (cc // L)
    strict = same_head & ((cc % L) < (rr % L))
    incl = same_head & ((cc % L) <= (rr % L))
    eye = (rr == cc).astype(F32)
    first_head = rr < L
    tri_incl = tri_ref[...]
    ones_bd = ones_ref[...]

    def stack_heads(x):
        return jnp.concatenate([jnp.where(head0, x, 0.0), jnp.where(head0, 0.0, x)], axis=0)

    def unstack(x):
        return x[:L] + x[L:]

    def prepare(ci, tick):
        chains = [(ci * prep_chunks + cc, gi) for cc in range(prep_chunks) for gi in range(n_groups)]
        rows = [pl.ds(pl.multiple_of(c * L, L), L) for c, _ in chains]
        lanes = [slice(gi * LANES, (gi + 1) * LANES) for _, gi in chains]
        idxs = [c * n_groups + gi for c, gi in chains]
        each = lambda f, *cols: [f(*a) for a in zip(*cols)]
        load = lambda ref: each(lambda rw, ls: ref[0, rw, ls].astype(F32), rows, lanes)
        r, lw, k, v, kk, bb = (load(ref) for ref in (r_ref, lw_ref, k_ref, v_ref, kk_ref, b_ref))

        cum = each(lambda x: _dot_exact_lhs3(tri_incl, x), lw)
        d_inv = each(lambda c_: jnp.exp(-c_), cum)
        d_last = each(lambda c_: jnp.exp(c_[L - 1:L, :]), cum)
        xa = each(lambda c_, l_, kk_: stack_heads(-(jnp.exp(c_ - l_) * kk_)).astype(BF16), cum, lw, kk)
        xr32 = each(lambda c_, r_: stack_heads(jnp.exp(c_) * r_), cum, r)
        bt32 = each(lambda b_, d_: b_ * d_, bb, d_inv)
        kt32 = each(lambda k_, d_: k_ * d_, k, d_inv)

        def gram(xa_, xr_, bt_, kt_):
            z2 = jnp.concatenate([bt_.astype(BF16), kt_.astype(BF16)], axis=0)
            x_all = jnp.concatenate([xa_, xr_.astype(BF16)], axis=0)
            return _dot_nt(x_all, z2)

        gmat = each(gram, xa, xr32, bt32, kt32)
        tick()
        swap = each(lambda g_: pltpu.roll(g_, HEAD_DIM, 1), gmat)
        m_ab = each(lambda g_, w_: jnp.where(strict, jnp.where(first_head, g_[:2 * L], w_[:2 * L]), 0.0), gmat, swap)
        m_ak = each(lambda g_, w_: jnp.where(strict, jnp.where(first_head, w_[:2 * L], g_[:2 * L]), 0.0).astype(BF16),
                    gmat, swap)
        m_rb = each(lambda g_, w_: jnp.where(incl, jnp.where(first_head, g_[2 * L:], w_[2 * L:]), 0.0).astype(BF16),
                    gmat, swap)
        m_rk = each(lambda g_, w_: jnp.where(incl, jnp.where(first_head, w_[2 * L:], g_[2 * L:]), 0.0).astype(BF16),
                    gmat, swap)

        p = m_ab
        t_inv = each(lambda m_: eye + m_, m_ab)
        for _ in range(5):
            p = each(lambda p_: _dot(p_.astype(BF16), p_.astype(BF16)), p)
            t_inv = each(lambda t_, p_: t_ + _dot(t_.astype(BF16), p_.astype(BF16)), t_inv, p)
            tick()
        t16 = each(lambda t_: t_.astype(BF16), t_inv)

        v_st = each(lambda v_: stack_heads(v_).astype(BF16), v)
        mv = each(_dot, m_ak, v_st)
        tick()
        tav = each(lambda t_, xa_, mv_: _dot(t_, jnp.concatenate([xa_, mv_.astype(BF16)], axis=1)),
                   t16, xa, mv)
        tick()
        rby = each(lambda m_, tav_: _dot(m_, tav_.astype(BF16)), m_rb, tav)
        tick()
        y0 = each(lambda rby_, m_, v_: rby_[:, LANES:] + _dot(m_, v_), rby, m_rk, v_st)
        for n, idx in enumerate(idxs):
            xp_ref[idx] = jnp.concatenate([tav[n][:, :LANES], xr32[n] + rby[n][:, :LANES]], axis=0).astype(BF16)
            vp_ref[idx] = tav[n][:, LANES:]
            y0_ref[idx] = y0[n]
            bdt_ref[idx] = jnp.concatenate([stack_heads(bt32[n] * d_last[n]), stack_heads(kt32[n] * d_last[n])],
                                           axis=0).T.astype(BF16)
            vst_ref[idx] = v_st[n]
            dl_ref[idx] = jnp.broadcast_to(d_last[n], (LANES, LANES)).T
            bon_ref[rows[n], lanes[n]] = _dot_exact_rhs(r[n] * k[n] * rk_ref[:, lanes[n]], ones_bd, 2) * v[n]

    def advance_steps(ci):
        rows = pl.ds(pl.multiple_of(ci * L, L), L)
        groups = list(range(n_groups))
        lanes = [slice(gi * LANES, (gi + 1) * LANES) for gi in groups]
        idxs = [ci * n_groups + gi for gi in groups]
        each = lambda f, *cols: [f(*a) for a in zip(*cols)]
        s_kv = each(lambda gi: state_ref[gi], groups)
        xs = each(lambda i_, s_: _dot(xp_ref[i_], s_.astype(BF16)), idxs, s_kv)
        yield
        u16 = each(lambda x_, i_: (x_[:2 * L] + vp_ref[i_]).astype(BF16), xs, idxs)
        upd = each(lambda i_, u_: _dot(bdt_ref[i_], jnp.concatenate([u_, vst_ref[i_]], axis=0)), idxs, u16)
        for gi in groups:
            state_ref[gi] = s_kv[gi] * dl_ref[idxs[gi]] + upd[gi]
        yield

        inv_n = 1.0 / HEAD_DIM
        y = each(lambda x_, i_: unstack(x_[2 * L:] + y0_ref[i_]), xs, idxs)
        yc = each(lambda y_: y_ - _dot_exact_rhs(y_, ones_bd, 2) * inv_n, y)
        yield
        var = each(lambda yc_: _dot_exact_rhs(yc_ * yc_, ones_bd, 2) * inv_n, yc)
        for gi in groups:
            ls = lanes[gi]
            yn = yc[gi] * lax.rsqrt(var[gi] + GN_EPS) * lnw_ref[:, ls] + lnb_ref[:, ls]
            o_ref[0, rows, ls] = ((yn + bon_ref[rows, ls]) * g_ref[0, rows, ls].astype(F32)).astype(o_ref.dtype)
        yield

    n_prep = n_chunks // prep_chunks

    def advance_pair(pi):
        for cc in range(prep_chunks):
            yield from advance_steps(pi * prep_chunks + cc)

    def prepare_with(pi, side):
        prepare(pi, (lambda: next(side, None)) if side is not None else (lambda: None))
        if side is not None:
            for _ in side:
                pass

    prepare_with(jnp.int32(0), None)

    def overlapped(pi, carry):
        prepare_with(pi, advance_pair(pi - 1))
        return carry

    lax.fori_loop(1, n_prep, overlapped, 0)
    for _ in advance_pair(jnp.int32(n_prep - 1)):
        pass


def _rwkv_recurrence(r, lw, k, v, kkn, bb, g, r_k, lnx_w, lnx_b, tri, ones_pair, ts, lane_w):
    b, s, c = r.shape
    tok = pl.BlockSpec((1, ts, lane_w), lambda i, j, t: (i, t, j))
    vec = pl.BlockSpec((1, lane_w), lambda i, j, t: (0, j))
    small = lambda arr: pl.BlockSpec(arr.shape, lambda i, j, t: (0,) * arr.ndim)
    n_groups = lane_w // LANES
    ncg = (ts // CHUNK) * n_groups
    return pl.pallas_call(
        _rwkv_rec_kernel,
        grid=(b, c // lane_w, s // ts),
        in_specs=[tok] * 7 + [vec, vec, vec, small(tri), small(ones_pair)],
        out_specs=tok,
        out_shape=jax.ShapeDtypeStruct((b, s, c), BF16),
        scratch_shapes=[pltpu.VMEM((n_groups, LANES, LANES), F32),
                        pltpu.VMEM((ncg, 4 * CHUNK, LANES), BF16),
                        pltpu.VMEM((ncg, 2 * CHUNK, LANES), F32),
                        pltpu.VMEM((ncg, 2 * CHUNK, LANES), F32),
                        pltpu.VMEM((ncg, LANES, 4 * CHUNK), BF16),
                        pltpu.VMEM((ncg, 2 * CHUNK, LANES), BF16),
                        pltpu.VMEM((ncg, LANES, LANES), F32),
                        pltpu.VMEM((ts, lane_w), F32)],
        compiler_params=_cparams("parallel", "parallel", "arbitrary"),
        name="rwkv_recurrence",
    )(r, lw, k, v, kkn, bb, g, r_k, lnx_w, lnx_b, tri, ones_pair)


SB_QTILE = 512
SB_SUBTILE = 128
SB_STAGE_LAG = 3
SB_SKIP_BELOW = -104.0
SB_PAIRS = 4


def _sb_attn_kernel(q_ref, k_ref, v_ref, cs_ref, o_ref, qs_ref, acc_ref, run_ref, tmax_ref):
    s = q_ref.shape[1]
    tb = SB_BLOCK
    tq = min(SB_QTILE, s)
    rs = SB_SUBTILE
    sub_per_block = tb // rs
    bpt = tq // tb
    kpb = 2 if bpt % 2 == 0 else 1
    nb, nt = s // tb, s // tq
    shift = bpt.bit_length() - 1
    lane = lax.broadcasted_iota(jnp.int32, (s, LANES), 1)
    head0 = lane < HEAD_DIM
    n_heads = qs_ref.shape[0]
    for pp in range(n_heads // 2):
        q = q_ref[0, :, pp * LANES:(pp + 1) * LANES] * (HEAD_DIM ** -0.5)
        qs_ref[2 * pp] = jnp.where(head0, q, jnp.zeros_like(q))
        qs_ref[2 * pp + 1] = jnp.where(head0, jnp.zeros_like(q), q)
    for qb in range(nb):
        tmax_ref[qb] = 0.0
    cs = cs_ref[...]
    cs2 = jnp.concatenate([cs, cs], axis=0)

    def tile_step(ti, blocks, pieces):
        r0 = ti * tq
        subs = [(kbi, h, pl.ds(pl.multiple_of(r0 + i * rs, rs), rs), r0 + i * rs, diag, i // sub_per_block)
                for kbi, i, diag in pieces for h in range(n_heads)]
        n = len(subs)
        lag = SB_STAGE_LAG
        z, lb, parts, sums, att, before = ({} for _ in range(6))
        top = {}
        for step in range(n + 2 + 2 * lag):
            if step < n:
                kbi, h, rows = subs[step][:3]
                z[step] = _dot_nt(qs_ref[h, rows, :], blocks[kbi][1][h // 2])
            i = step - 1
            if 0 <= i < n:
                zz = z.pop(i)
                sp = jnp.log(1.0 + jnp.exp(-jnp.abs(zz)))
                lb[i] = jnp.minimum(zz, 0.0) - sp
                log_fail = lb[i] - zz
                if subs[i][4]:
                    t_idx = subs[i][3] + lax.broadcasted_iota(jnp.int32, (rs, tb), 0)
                    s_idx = blocks[subs[i][0]][0] + lax.broadcasted_iota(jnp.int32, (rs, tb), 1)
                    before[i] = s_idx < t_idx
                    log_fail = jnp.where(before[i], log_fail, 0.0)
                parts[i] = _split2(log_fail)
            i = step - 1 - lag
            if 0 <= i < n:
                hi, lo = parts.pop(i)
                sums[i] = _dot(jnp.concatenate([hi, lo], axis=1), cs2)
            i = step - 2 - lag
            if 0 <= i < n:
                kbi, h, rows, _, diag, _ = subs[i]
                sm = sums.pop(i)
                if diag:
                    a = jnp.where(before.pop(i), jnp.exp(lb.pop(i) + sm[:, :tb]), 0.0)
                    run = sm[:, tb:]
                else:
                    run = run_ref[h, rows, :]
                    a = jnp.exp(lb.pop(i) + run + sm[:, :tb])
                    run = run + sm[:, tb:]
                att[i] = a.astype(BF16)
                run_ref[h, rows, :] = run
                if kbi == len(blocks) - 1:
                    qb = subs[i][5]
                    top[qb] = run if qb not in top else jnp.maximum(top[qb], run)
            i = step - 2 - 2 * lag
            if 0 <= i < n:
                kbi, h, rows = subs[i][:3]
                pv = _dot(att.pop(i), blocks[kbi][2][h // 2])
                if subs[i][4]:
                    acc_ref[h, rows, :] = pv
                else:
                    acc_ref[h, rows, :] += pv
        for qb, t in top.items():
            tmax_ref[ti * bpt + qb] = jnp.max(t)

    def below_pieces(n_q):
        return [(kbi, i, False) for kbi in range(kpb) for i in range(n_q * sub_per_block)]

    def diagonal_pieces(u, q_last):
        jl = kpb * u + kpb - 1
        out = []
        for kbi in range(kpb):
            kl = jl - kbi
            for qb in range(kl, q_last + 1):
                for i in range(qb * sub_per_block, (qb + 1) * sub_per_block):
                    out.append((kbi, i, qb == kl))
        return out

    def last_alive(ti, first):
        n = jnp.int32(first)
        for qb in range(first, bpt):
            n = jnp.where(tmax_ref[ti * bpt + qb] >= SB_SKIP_BELOW, qb + 1, n)
        return n

    def key_blocks(jj, carry):
        j = nb - 1 - kpb * jj
        blocks = []
        for d in range(kpb):
            k0 = pl.multiple_of((j - d) * tb, tb)
            lanes = [slice(pp * LANES, (pp + 1) * LANES) for pp in range(n_heads // 2)]
            blocks.append((k0, [k_ref[0, pl.ds(k0, tb), ls] for ls in lanes],
                           [v_ref[0, pl.ds(k0, tb), ls] for ls in lanes]))
        t0 = lax.shift_right_logical(j, shift)
        case = lax.shift_right_logical(j - t0 * bpt, kpb.bit_length() - 1)
        for u in range(bpt // kpb):
            jl = kpb * u + kpb - 1
            n_q = last_alive(t0, jl + 1)
            for q_last in range(jl, bpt):
                @pl.when((case == u) & (n_q == q_last + 1))
                def _():
                    tile_step(t0, blocks, diagonal_pieces(u, q_last))

        def below(ti, c):
            n_q = last_alive(ti, 0)
            for m in range(1, bpt + 1):
                @pl.when(n_q == m)
                def _():
                    tile_step(ti, blocks, below_pieces(m))
            return c

        lax.fori_loop(t0 + 1, nt, below, 0)
        return carry

    lax.fori_loop(0, nb // kpb, key_blocks, 0)
    for pp in range(n_heads // 2):
        o_ref[0, :, pp * LANES:(pp + 1) * LANES] = jnp.where(head0, acc_ref[2 * pp], acc_ref[2 * pp + 1]).astype(o_ref.dtype)


def _sb_attention(qkv, cs):
    b, s, c3 = qkv.shape
    c = c3 // 3
    w = LANES * SB_PAIRS
    n_blk = c // w
    n_heads = 2 * SB_PAIRS
    return pl.pallas_call(
        _sb_attn_kernel,
        grid=(b, n_blk),
        in_specs=[
            pl.BlockSpec((1, s, w), lambda i, p: (i, 0, p)),
            pl.BlockSpec((1, s, w), lambda i, p: (i, 0, n_blk + p)),
            pl.BlockSpec((1, s, w), lambda i, p: (i, 0, 2 * n_blk + p)),
            pl.BlockSpec(cs.shape, lambda i, p: (0, 0)),
        ],
        out_specs=pl.BlockSpec((1, s, w), lambda i, p: (i, 0, p)),
        out_shape=jax.ShapeDtypeStruct((b, s, c), BF16),
        scratch_shapes=[pltpu.VMEM((n_heads, s, LANES), BF16), pltpu.VMEM((n_heads, s, LANES), F32),
                        pltpu.VMEM((n_heads, s, SB_BLOCK), F32),
                        pltpu.SMEM((s // SB_BLOCK,), F32)],
        compiler_params=_cparams("parallel", "parallel"),
        name="sb_attention",
    )(qkv, qkv, qkv, cs)


def _cross_kernel(y_ref, wy_ref, h_ref, gn_ref, wq_ref, kv_ref, wo_ref, o_ref):
    h = h_ref[0] + _dot(y_ref[0], wy_ref[...])
    c = h.shape[-1]
    xhd = c // X_HEADS
    q = _dot(_rms(h, gn_ref[...]).astype(BF16), wq_ref[...]).astype(BF16)
    outs = []
    for hd in range(X_HEADS):
        qh = q[:, hd * xhd:(hd + 1) * xhd]
        kh = kv_ref[0, :, hd * xhd:(hd + 1) * xhd]
        vh = kv_ref[0, :, c + hd * xhd:c + (hd + 1) * xhd]
        sc = _dot_nt(qh, kh) * (xhd ** -0.5)
        sc = sc - jnp.max(sc, axis=-1, keepdims=True)
        e = jnp.exp(sc)
        p = e / jnp.sum(e, axis=-1, keepdims=True)
        outs.append(_dot(p.astype(BF16), vh).astype(BF16))
    o = jnp.concatenate(outs, axis=-1)
    o_ref[0] = h + _dot(o, wo_ref[...])


def _cross_block(y, wy, h, gn, wq, memkv, wo, tq):
    b, s, c = h.shape
    tok = pl.BlockSpec((1, tq, c), lambda i, t: (i, t, 0))
    full = lambda arr: pl.BlockSpec(arr.shape, lambda i, t: (0,) * arr.ndim)
    return pl.pallas_call(
        _cross_kernel,
        grid=(b, s // tq),
        in_specs=[tok, full(wy), tok, full(gn), full(wq),
                  pl.BlockSpec((1,) + memkv.shape[1:], lambda i, t: (i, 0, 0)), full(wo)],
        out_specs=tok,
        out_shape=jax.ShapeDtypeStruct((b, s, c), F32),
        compiler_params=_cparams("parallel", "parallel"),
        name="cross_attention",
    )(y, wy, h, gn, wq, memkv, wo)


def _lane_min_index(mask, lane):
    return jnp.min(jnp.where(mask, lane, float(ROUTER_LANES)), axis=-1, keepdims=True)


MOE_ROUTE_TILE = 1024
MOE_ROW_TILE = 256


def _dot_3pass(a, b):
    ah, al = _split2(a)
    bh, bl = _split2(b)
    return _dot(ah, bl) + _dot(al, bh) + _dot(ah, bh)


def _moe_route_kernel(h_ref, gn_ref, wg_ref, bg_ref, tri_ref, dest_ref, pg_ref, cnt_ref, base_ref, *, n_tokens):
    @pl.when(pl.program_id(0) == 0)
    def _():
        base_ref[...] = jnp.zeros_like(base_ref)

    hn = _rms(h_ref[...], gn_ref[...])
    gl = _dot_3pass(hn, wg_ref[...]) + bg_ref[...]
    lane = lax.broadcasted_iota(jnp.int32, gl.shape, 1).astype(F32)
    gmax = jnp.max(gl, axis=-1, keepdims=True)
    group = _lane_min_index(gl == gmax, lane)
    p_group = 1.0 / jnp.sum(jnp.exp(gl - gmax), axis=-1, keepdims=True)
    onehot = (lane == group).astype(F32)
    earlier = _dot(tri_ref[...], onehot.astype(BF16))
    base = base_ref[0:1, :]
    rank = jnp.sum(onehot * (earlier + base), axis=-1, keepdims=True)
    dest = group * float(n_tokens) + rank
    tm = dest.shape[0]
    dest_row = jnp.broadcast_to(dest, (tm, LANES)).T[0:1, :]
    dest_ref[0] = dest_row.astype(jnp.int32)
    pg_ref[...] = jnp.broadcast_to(p_group, pg_ref.shape)
    new_base = base + earlier[tm - 1:tm, :] + onehot[tm - 1:tm, :]
    base_ref[0:1, :] = new_base
    cnt_ref[...] = jnp.broadcast_to(new_base, cnt_ref.shape)


def _moe_route(h, gn, wg128, bg128, tri, tm):
    t, c = h.shape
    full = lambda arr: pl.BlockSpec(arr.shape, lambda i: (0,) * arr.ndim)
    return pl.pallas_call(
        functools.partial(_moe_route_kernel, n_tokens=t),
        grid=(t // tm,),
        in_specs=[pl.BlockSpec((tm, c), lambda i: (i, 0)), full(gn), full(wg128), full(bg128), full(tri)],
        out_specs=[pl.BlockSpec((1, 1, tm), lambda i: (i, 0, 0)),
                   pl.BlockSpec((tm, LANES), lambda i: (i, 0)),
                   pl.BlockSpec((8, LANES), lambda i: (0, 0))],
        out_shape=[jax.ShapeDtypeStruct((t // tm, 1, tm), jnp.int32),
                   jax.ShapeDtypeStruct((t, LANES), F32),
                   jax.ShapeDtypeStruct((8, LANES), F32)],
        scratch_shapes=[pltpu.VMEM((8, LANES), F32)],
        compiler_params=_cparams("arbitrary"),
        name="moe_route",
    )(h, gn, wg128, bg128, tri)


def _moe_scatter_kernel(zt_ref, dest_ref, h_ref, gn_ref, xs_ref, buf_ref, sem_ref, zsem_ref):
    i = pl.program_id(0)
    n = pl.num_programs(0)
    tm = h_ref.shape[0]
    slot = lax.rem(i, 2)

    def row_copy(s, r, d):
        return pltpu.make_async_copy(buf_ref.at[s, pl.ds(r, 1), :], xs_ref.at[pl.ds(d, 1), :], sem_ref.at[s])

    def wait_all(s):
        pltpu.make_async_copy(buf_ref.at[s], xs_ref.at[pl.ds(0, tm), :], sem_ref.at[s]).wait()

    @pl.when(i == 0)
    def _():
        buf_ref[1] = jnp.zeros(buf_ref.shape[1:], buf_ref.dtype)
        fills = [pltpu.make_async_copy(buf_ref.at[1], xs_ref.at[pl.ds(pl.multiple_of(zt_ref[k] * tm, tm), tm), :],
                                       zsem_ref.at[0]) for k in range(zt_ref.shape[0])]
        for f in fills:
            f.start()
            f.wait()

    buf_ref[slot] = _rms(h_ref[...], gn_ref[...])

    for r in range(tm):
        row_copy(slot, r, dest_ref[0, 0, r]).start(priority=r % 2)

    @pl.when(i > 0)
    def _():
        wait_all(1 - slot)

    @pl.when(i == n - 1)
    def _():
        wait_all(slot)


def _moe_scatter(zero_tiles, h, gn, dest3, n_rows, tm):
    t, c = h.shape
    grid_spec = pltpu.PrefetchScalarGridSpec(
        num_scalar_prefetch=1,
        grid=(t // tm,),
        in_specs=[pl.BlockSpec((1, 1, tm), lambda i, zt: (i, 0, 0), memory_space=pltpu.SMEM),
                  pl.BlockSpec((tm, c), lambda i, zt: (i, 0)),
                  pl.BlockSpec(gn.shape, lambda i, zt: (0, 0))],
        out_specs=pl.BlockSpec(memory_space=pl.ANY),
        scratch_shapes=[pltpu.VMEM((2, tm, c), F32), pltpu.SemaphoreType.DMA((2,)),
                        pltpu.SemaphoreType.DMA((1,))],
    )
    return pl.pallas_call(
        _moe_scatter_kernel,
        grid_spec=grid_spec,
        out_shape=jax.ShapeDtypeStruct((n_rows, c), F32),
        compiler_params=_cparams("arbitrary"),
        name="moe_scatter",
    )(zero_tiles, dest3, h, gn)


def _moe_group_ffn_kernel(tg_ref, x_ref, we_ref, be_ref, w1_ref, w3_ref, w2_ref, o_ref):
    del tg_ref
    x = x_ref[...]
    xh, xl = _split2(x)
    el = be_ref[0] + _dot(xh, we_ref[0, 1]) + _dot(xl, we_ref[0, 0]) + _dot(xh, we_ref[0, 0])
    lane = lax.broadcasted_iota(jnp.int32, el.shape, 1).astype(F32)
    top1 = jnp.max(el, axis=-1, keepdims=True)
    idx1 = _lane_min_index(el == top1, lane)
    el2 = jnp.where(lane == idx1, -jnp.inf, el)
    top2 = jnp.max(el2, axis=-1, keepdims=True)
    idx2 = _lane_min_index(el2 == top2, lane)
    e2 = jnp.exp(top2 - top1)
    s1 = 1.0 / (1.0 + e2)
    s2 = e2 / (1.0 + e2)
    hids = []
    for e in range(EXPERTS_PER_GROUP):
        ge = jnp.where(idx1 == float(e), s1, jnp.where(idx2 == float(e), s2, 0.0))
        h1 = _dot(xh, w1_ref[0, e])
        h3 = _dot(xh, w3_ref[0, e])
        hids.append(((h1 * jax.nn.sigmoid(h1)) * h3 * ge).astype(BF16))
    o_ref[...] = _dot(jnp.concatenate(hids, axis=1), w2_ref[0])


def _moe_group_ffn(tile_group, xs, we2, be, w1, w3, w2, tm):
    rows, c = xs.shape
    g, e, _, f = w1.shape
    grid_spec = pltpu.PrefetchScalarGridSpec(
        num_scalar_prefetch=1,
        grid=(rows // tm,),
        in_specs=[pl.BlockSpec((tm, c), lambda i, tg: (i, 0)),
                  pl.BlockSpec((1, 2, c, LANES), lambda i, tg: (tg[i], 0, 0, 0)),
                  pl.BlockSpec((1, 1, LANES), lambda i, tg: (tg[i], 0, 0)),
                  pl.BlockSpec((1, e, c, f), lambda i, tg: (tg[i], 0, 0, 0)),
                  pl.BlockSpec((1, e, c, f), lambda i, tg: (tg[i], 0, 0, 0)),
                  pl.BlockSpec((1, e * f, c), lambda i, tg: (tg[i], 0, 0))],
        out_specs=pl.BlockSpec((tm, c), lambda i, tg: (i, 0)),
    )
    return pl.pallas_call(
        _moe_group_ffn_kernel,
        grid_spec=grid_spec,
        out_shape=jax.ShapeDtypeStruct((rows, c), F32),
        compiler_params=_cparams("arbitrary"),
        name="moe_group_ffn",
    )(tile_group, xs, we2, be, w1, w3, w2)


def _moe_gather_kernel(dest_ref, nxt_ref, h_ref, pg_ref, gf_ref, ys_ref, o_ref, buf_ref, sem_ref, *, final_norm):
    i = pl.program_id(0)
    n = pl.num_programs(0)
    tm = h_ref.shape[0]
    slot = lax.rem(i, 2)

    def row_copy(s, r, d):
        return pltpu.make_async_copy(ys_ref.at[pl.ds(d, 1), :], buf_ref.at[s, pl.ds(r, 1), :], sem_ref.at[s])

    def start_all(s, idx_ref):
        for r in range(tm):
            row_copy(s, r, idx_ref[0, 0, r]).start(priority=r % 2)

    @pl.when(i == 0)
    def _():
        start_all(0, dest_ref)

    @pl.when(i + 1 < n)
    def _():
        start_all(1 - slot, nxt_ref)

    pltpu.make_async_copy(ys_ref.at[pl.ds(0, tm), :], buf_ref.at[slot], sem_ref.at[slot]).wait()

    y = buf_ref[slot]
    pg = pg_ref[...]
    c = y.shape[1]
    out = h_ref[...] + y * jnp.concatenate([pg] * (c // LANES), axis=1)
    if final_norm:
        out = _rms(out, gf_ref[...])
    o_ref[...] = out


def _moe_gather(h, pg, dest3, ys, g_final, final_norm, tm):
    t, c = h.shape
    n = t // tm
    return pl.pallas_call(
        functools.partial(_moe_gather_kernel, final_norm=final_norm),
        grid=(n,),
        in_specs=[pl.BlockSpec((1, 1, tm), lambda i: (i, 0, 0), memory_space=pltpu.SMEM),
                  pl.BlockSpec((1, 1, tm), lambda i: (jnp.minimum(i + 1, n - 1), 0, 0), memory_space=pltpu.SMEM),
                  pl.BlockSpec((tm, c), lambda i: (i, 0)),
                  pl.BlockSpec((tm, LANES), lambda i: (i, 0)),
                  pl.BlockSpec(g_final.shape, lambda i: (0, 0)),
                  pl.BlockSpec(memory_space=pl.ANY)],
        out_specs=pl.BlockSpec((tm, c), lambda i: (i, 0)),
        out_shape=jax.ShapeDtypeStruct((t, c), F32),
        scratch_shapes=[pltpu.VMEM((2, tm, c), F32), pltpu.SemaphoreType.DMA((2,))],
        compiler_params=_cparams("arbitrary"),
        name="moe_gather",
    )(dest3, dest3, h, pg, g_final, ys)


def _moe_layer(h, gn, wg, bg, we, be, w1, w3, w2, g_final, final_norm):
    t, c = h.shape
    g, e = N_GROUPS, EXPERTS_PER_GROUP
    f = w1.shape[-1]
    tm_r = MOE_ROUTE_TILE if t % MOE_ROUTE_TILE == 0 else t
    tm = MOE_ROW_TILE if t % MOE_ROW_TILE == 0 else t
    neg = -1e30
    wg128 = jnp.zeros((c, LANES), F32).at[:, :g].set(wg)
    bg128 = jnp.full((1, LANES), neg, F32).at[0, :g].set(bg)
    jj = jnp.arange(tm_r)
    tri = (jj[:, None] > jj[None, :]).astype(BF16)
    dest_local, pg, counts = _moe_route(h, gn, wg128, bg128, tri, tm_r)

    cnt = counts[0, :g].astype(jnp.int32)
    tiles = (cnt + tm - 1) // tm
    tile_end = jnp.cumsum(tiles)
    offs = (tile_end - tiles) * tm
    n_tiles = t // tm + g
    dl = dest_local.reshape(t)
    grp = dl // t
    dest = (dl - grp * t + offs[grp]).reshape(t // tm, 1, tm)
    tile_group = jnp.minimum(jnp.sum(jnp.arange(n_tiles)[:, None] >= tile_end[None, :], axis=1), g - 1).astype(jnp.int32)

    zero_tiles = jnp.concatenate([jnp.maximum(tile_end - 1, 0),
                                  jnp.minimum(tile_end[g - 1] + jnp.arange(g), n_tiles - 1)]).astype(jnp.int32)
    xs = _moe_scatter(zero_tiles, h, gn, dest, n_tiles * tm, tm)
    we_g = jnp.zeros((g, c, LANES), F32).at[:, :, :e].set(we.reshape(c, g, e).transpose(1, 0, 2))
    we2 = jnp.stack(_split2(we_g), axis=1)
    be_g = jnp.full((g, 1, LANES), neg, F32).at[:, 0, :e].set(be.reshape(g, e))
    ys = _moe_group_ffn(tile_group, xs, we2, be_g, w1.astype(BF16), w3.astype(BF16),
                        w2.astype(BF16).reshape(g, e * f, c), tm)
    return _moe_gather(h, pg, dest, ys, g_final, final_norm, tm)


def _block_ones(n, blk):
    i = jnp.arange(n)
    return ((i[:, None] // blk) == (i[None, :] // blk)).astype(BF16)


def _tile(n, pref):
    return pref if n % pref == 0 else n


def kernel(x, mem, norm_mix, norm_cross, norm_ffn, norm_mem, norm_final, rw_mix, rw_wr, rw_wk, rw_wv, rw_w0, rw_w1, rw_w2, rw_a0, rw_a1, rw_a2, rw_g1, rw_g2, rw_kk, rw_ka, rw_rk, rw_lnx_w, rw_lnx_b, rw_wo, sb_wqkv, sb_wo, xa_wq, xa_wkv, xa_wo, moe_wg, moe_bg, moe_we, moe_be, moe_w1, moe_w3, moe_w2):
    b, s, c = x.shape
    t = b * s
    depth = norm_mix.shape[0]
    row = lambda vec: vec.reshape(1, -1).astype(F32)
    bf = lambda w: w.astype(BF16)

    ones_quad = _block_ones(MXU_DIM, HEAD_DIM)
    ones_pair = _block_ones(LANES, HEAD_DIM)
    tri_incl = jnp.tile((jnp.arange(CHUNK)[:, None] >= jnp.arange(CHUNK)[None, :]).astype(BF16), (1, 3))
    jj = jnp.arange(SB_BLOCK)
    sb_cs = jnp.concatenate([(jj[:, None] > jj[None, :]).astype(BF16),
                             jnp.ones((SB_BLOCK, SB_BLOCK), BF16)], axis=1)

    memkv = _norm_matmul(mem.reshape(b * N_MEM, c), row(norm_mem), bf(xa_wkv), BF16,
                         _tile(b * N_MEM, MEMKV_ROWS), 2 * c).reshape(b, N_MEM, 2 * c)

    h = x
    for i in range(depth):
        j = i // 2
        if i % 2 == 0:
            r, lw, k, v, kkn, bb, g = _rwkv_proj(
                h, row(norm_mix[i]), rw_mix[j].astype(F32), bf(rw_wr[j]), bf(rw_wk[j]), bf(rw_wv[j]),
                bf(rw_w1[j]), bf(rw_w2[j]), bf(rw_a1[j]), bf(rw_a2[j]), bf(rw_g1[j]), bf(rw_g2[j]),
                row(rw_w0[j]), row(rw_a0[j]), row(rw_kk[j]), row(rw_ka[j]), ones_quad, _tile(s, PROJ_ROWS))
            y = _rwkv_recurrence(r, lw, k, v, kkn, bb, g, row(rw_rk[j]), row(rw_lnx_w[j]),
                                 row(rw_lnx_b[j]), tri_incl, ones_pair, _tile(s, REC_ROWS), REC_LANES)
            w_mix_out = bf(rw_wo[j])
        else:
            qkv = _norm_matmul(h.reshape(t, c), row(norm_mix[i]), bf(sb_wqkv[j]), BF16,
                               _tile(t, QKV_ROWS), 3 * c).reshape(b, s, 3 * c)
            y = _sb_attention(qkv, sb_cs)
            w_mix_out = bf(sb_wo[j])
        h = _cross_block(y, w_mix_out, h, row(norm_cross[i]), bf(xa_wq[i]), memkv, bf(xa_wo[i]), _tile(s, CROSS_ROWS))

        h = _moe_layer(h.reshape(t, c), row(norm_ffn[i]), moe_wg[i], moe_bg[i], moe_we[i], moe_be[i],
                       moe_w1[i], moe_w3[i], moe_w2[i], row(norm_final), i == depth - 1).reshape(b, s, c)
    return h
```
